```python
import jax
import jax.numpy as jnp
from jax import lax
import numpy as np

D_MODEL = 2048
BATCH = 1
SEQ = 16384
DEPTH = 2
DEC_BATCH = 16
DEC_SEQ = 2048
PAST_LEN = 128

N_MEM = 256
N_FOURIER_GROUPS = 4
FOURIER_GROUP_DIM = 256
FOURIER_DIM = N_FOURIER_GROUPS * FOURIER_GROUP_DIM
MLA_HEADS = 16
Q_LORA = 512
KV_LORA = 512
QK_NOPE = 128
QK_ROPE = 64
V_HEAD = 128
ROPE_THETA = 10000.0
Q_BLOCK = 128
MEM_HEADS = 4
MEM_HEAD_DIM = 256
MEM_DIM = MEM_HEADS * MEM_HEAD_DIM
N_BRANCH = 3
D_IN = FOURIER_DIM + Q_LORA + KV_LORA + QK_ROPE + MEM_DIM + N_BRANCH * D_MODEL
D_FF = 5632
CONV_WIDTH = 3
EPS = 1e-6

kernel_name = 'hybrid_fnet_mla_memory_encoder'


def _rmsnorm(x, g):
    xf = x.astype(jnp.float32)
    xf = xf * lax.rsqrt(jnp.mean(xf * xf, axis=-1, keepdims=True) + EPS)
    return (xf * g.astype(jnp.float32)).astype(x.dtype)


def _rope_tables(seq):
    inv_freq = 1.0 / (ROPE_THETA ** (jnp.arange(0, QK_ROPE, 2, dtype=jnp.float32) / QK_ROPE))
    ang = jnp.arange(seq, dtype=jnp.float32)[:, None] * inv_freq[None, :]
    return jnp.cos(ang), jnp.sin(ang)


def _rope(x, cos, sin):
    xf = x.astype(jnp.float32)
    x1, x2 = jnp.split(xf, 2, axis=-1)
    return jnp.concatenate([x1 * cos - x2 * sin, x2 * cos + x1 * sin], axis=-1).astype(x.dtype)


def _fourier(u):
    B, S, _ = u.shape
    ug = u.astype(jnp.float32).reshape(B, S, N_FOURIER_GROUPS, FOURIER_GROUP_DIM)
    yg = jnp.fft.fftn(ug, axes=(1, 3), norm='ortho').real
    return yg.reshape(B, S, FOURIER_DIM).astype(u.dtype)


def _mla(c_q, c_kv, cos, sin, q_norm, w_uq, kv_norm, w_ukv):
    B, S, _ = c_q.shape
    q = (_rmsnorm(c_q, q_norm) @ w_uq).reshape(B, S, MLA_HEADS, QK_NOPE + QK_ROPE)
    q_nope, q_pe = q[..., :QK_NOPE], q[..., QK_NOPE:]
    q_pe = _rope(q_pe, cos[:, None, :], sin[:, None, :])
    c_lat, k_pe = c_kv[..., :KV_LORA], c_kv[..., KV_LORA:]
    kv = (_rmsnorm(c_lat, kv_norm) @ w_ukv).reshape(B, S, MLA_HEADS, QK_NOPE + V_HEAD)
    k_nope, v = kv[..., :QK_NOPE], kv[..., QK_NOPE:]
    k_pe = _rope(k_pe, cos, sin)
    scale = (QK_NOPE + QK_ROPE) ** -0.5
    n_blk = S // Q_BLOCK
    qn_b = q_nope.reshape(B, n_blk, Q_BLOCK, MLA_HEADS, QK_NOPE).transpose(1, 0, 2, 3, 4)
    qr_b = q_pe.reshape(B, n_blk, Q_BLOCK, MLA_HEADS, QK_ROPE).transpose(1, 0, 2, 3, 4)

    def block(args):
        qn, qr = args
        s = (jnp.einsum('bqhd,bkhd->bhqk', qn, k_nope, preferred_element_type=jnp.float32)
             + jnp.einsum('bqhr,bkr->bhqk', qr, k_pe, preferred_element_type=jnp.float32)) * scale
        p = jax.nn.softmax(s, axis=-1).astype(v.dtype)
        return jnp.einsum('bhqk,bkhd->bqhd', p, v)

    o = lax.map(block, (qn_b, qr_b))
    return o.transpose(1, 0, 2, 3, 4).reshape(B, S, MLA_HEADS * V_HEAD)


def _mem_attn(q_mem, mem, mem_norm, w_mem_kv):
    B, S, _ = q_mem.shape
    kv = (_rmsnorm(mem, mem_norm) @ w_mem_kv).reshape(B, mem.shape[1], 2, MEM_HEADS, MEM_HEAD_DIM)
    k, v = kv[:, :, 0], kv[:, :, 1]
    q = q_mem.reshape(B, S, MEM_HEADS, MEM_HEAD_DIM)
    s = jnp.einsum('bqhd,bkhd->bhqk', q, k, preferred_element_type=jnp.float32) * (MEM_HEAD_DIM ** -0.5)
    p = jax.nn.softmax(s, axis=-1).astype(v.dtype)
    return jnp.einsum('bhqk,bkhd->bqhd', p, v).reshape(B, S, MEM_DIM)


def _mixer(h, mem, cos, sin, w_in, q_norm, w_uq, kv_norm, w_ukv, mem_norm, w_mem_kv,
           w_fourier_out, w_attn_out, w_mem_out, w_o):
    B, S, _ = h.shape
    z = h @ w_in
    s0 = FOURIER_DIM
    s1 = s0 + Q_LORA
    s2 = s1 + KV_LORA + QK_ROPE
    s3 = s2 + MEM_DIM
    f_in, c_q, c_kv, q_mem, gate_in = jnp.split(z, [s0, s1, s2, s3], axis=-1)
    y_f = _fourier(f_in) @ w_fourier_out
    y_a = _mla(c_q, c_kv, cos, sin, q_norm, w_uq, kv_norm, w_ukv) @ w_attn_out
    y_m = _mem_attn(q_mem, mem, mem_norm, w_mem_kv) @ w_mem_out
    g = jax.nn.sigmoid(gate_in.astype(jnp.float32)).astype(h.dtype).reshape(B, S, N_BRANCH, D_MODEL)
    merged = g[:, :, 0] * y_f + g[:, :, 1] * y_a + g[:, :, 2] * y_m
    return merged @ w_o


def _conv_ffn(h, w_gate, w_up, conv_w, conv_b, w_down):
    g = h @ w_gate
    gp = jnp.pad(g, ((0, 0), (1, 1), (0, 0)))
    g = gp[:, :-2] * conv_w[0] + gp[:, 1:-1] * conv_w[1] + gp[:, 2:] * conv_w[2] + conv_b
    return (jax.nn.gelu(g, approximate=True) * (h @ w_up)) @ w_down


def _trunk(x, mem, p):
    cos, sin = _rope_tables(x.shape[1])
    for l in range(DEPTH):
        h = _rmsnorm(x, p['pre_mix_norm'][l])
        y = _mixer(h, mem, cos, sin, p['w_in'][l], p['q_norm'][l], p['w_uq'][l],
                   p['kv_norm'][l], p['w_ukv'][l], p['mem_norm'][l], p['w_mem_kv'][l],
                   p['w_fourier_out'][l], p['w_attn_out'][l], p['w_mem_out'][l], p['w_o'][l])
        x = x + _rmsnorm(y, p['post_mix_norm'][l])
        h = _rmsnorm(x, p['pre_ffn_norm'][l])
        y = _conv_ffn(h, p['w_ffn_gate'][l], p['w_ffn_up'][l], p['ffn_conv_w'][l],
                      p['ffn_conv_b'][l], p['w_ffn_down'][l])
        x = x + _rmsnorm(y, p['post_ffn_norm'][l])
    return x


def _normal(k, shape, scale):
    return jax.random.normal(k, shape, jnp.float32) * scale


def _gain(k, shape):
    return 1.0 + 0.05 * jax.random.normal(k, shape, jnp.float32)


def setup_inputs(seed: int = 0) -> dict:
    key = jax.random.key(seed)
    ks = jax.random.split(key, 24)
    L = DEPTH
    return {
        'x_prompt': _normal(ks[0], (BATCH, SEQ, D_MODEL), 1.0),
        'x_sample': _normal(ks[1], (DEC_BATCH, DEC_SEQ, D_MODEL), 1.0),
        'mem_prompt': _normal(ks[2], (BATCH, N_MEM, D_MODEL), 1.0),
        'mem_sample': _normal(ks[3], (DEC_BATCH, N_MEM, D_MODEL), 1.0),
        'pre_mix_norm': _gain(ks[4], (L, D_MODEL)),
        'w_in': _normal(ks[5], (L, D_MODEL, D_IN), D_MODEL ** -0.5),
        'q_norm': _gain(ks[6], (L, Q_LORA)),
        'w_uq': _normal(ks[7], (L, Q_LORA, MLA_HEADS * (QK_NOPE + QK_ROPE)), Q_LORA ** -0.5),
        'kv_norm': _gain(ks[8], (L, KV_LORA)),
        'w_ukv': _normal(ks[9], (L, KV_LORA, MLA_HEADS * (QK_NOPE + V_HEAD)), KV_LORA ** -0.5),
        'mem_norm': _gain(ks[10], (L, D_MODEL)),
        'w_mem_kv': _normal(ks[11], (L, D_MODEL, 2 * MEM_DIM), D_MODEL ** -0.5),
        'w_fourier_out': _normal(ks[12], (L, FOURIER_DIM, D_MODEL), FOURIER_DIM ** -0.5),
        'w_attn_out': _normal(ks[13], (L, MLA_HEADS * V_HEAD, D_MODEL), (MLA_HEADS * V_HEAD) ** -0.5),
        'w_mem_out': _normal(ks[14], (L, MEM_DIM, D_MODEL), MEM_DIM ** -0.5),
        'w_o': _normal(ks[15], (L, D_MODEL, D_MODEL), D_MODEL ** -0.5),
        'post_mix_norm': _gain(ks[16], (L, D_MODEL)),
        'pre_ffn_norm': _gain(ks[17], (L, D_MODEL)),
        'w_ffn_gate': _normal(ks[18], (L, D_MODEL, D_FF), D_MODEL ** -0.5),
        'w_ffn_up': _normal(ks[19], (L, D_MODEL, D_FF), D_MODEL ** -0.5),
        'ffn_conv_w': _normal(ks[20], (L, CONV_WIDTH, D_FF), CONV_WIDTH ** -0.5),
        'ffn_conv_b': _normal(ks[21], (L, D_FF), 0.01),
        'w_ffn_down': _normal(ks[22], (L, D_FF, D_MODEL), D_FF ** -0.5),
        'post_ffn_norm': _gain(ks[23], (L, D_MODEL)),
    }


def reference(x_prompt, x_sample, mem_prompt, mem_sample, pre_mix_norm, w_in, q_norm, w_uq,
              kv_norm, w_ukv, mem_norm, w_mem_kv, w_fourier_out, w_attn_out, w_mem_out, w_o,
              post_mix_norm, pre_ffn_norm, w_ffn_gate, w_ffn_up, ffn_conv_w, ffn_conv_b,
              w_ffn_down, post_ffn_norm):
    params = dict(pre_mix_norm=pre_mix_norm, w_in=w_in, q_norm=q_norm, w_uq=w_uq,
                  kv_norm=kv_norm, w_ukv=w_ukv, mem_norm=mem_norm, w_mem_kv=w_mem_kv,
                  w_fourier_out=w_fourier_out, w_attn_out=w_attn_out, w_mem_out=w_mem_out,
                  w_o=w_o, post_mix_norm=post_mix_norm, pre_ffn_norm=pre_ffn_norm,
                  w_ffn_gate=w_ffn_gate, w_ffn_up=w_ffn_up, ffn_conv_w=ffn_conv_w,
                  ffn_conv_b=ffn_conv_b, w_ffn_down=w_ffn_down, post_ffn_norm=post_ffn_norm)
    y_prompt = _trunk(x_prompt, mem_prompt, params)
    y_sample = _trunk(x_sample, mem_sample, params)
    return (y_prompt, y_sample)
```

```python
import functools
import math

import numpy as np
import jax
import jax.numpy as jnp
from jax import lax
from jax.experimental import pallas as pl
from jax.experimental.pallas import tpu as pltpu

D_MODEL = 2048
N_GROUPS = 4
GROUP_DIM = 256
FOURIER_DIM = N_GROUPS * GROUP_DIM
MLA_HEADS = 16
Q_LORA = 512
KV_LORA = 512
QK_NOPE = 128
QK_ROPE = 64
V_HEAD = 128
ROPE_THETA = 10000.0
MEM_HEADS = 4
MEM_HEAD_DIM = 256
MEM_DIM = MEM_HEADS * MEM_HEAD_DIM
D_FF = 5632
EPS = 1e-6

LANES = 128
QK_PAD = 256
KPE_PAD = LANES
CKV_EXT = KV_LORA + KPE_PAD
FFT_N2 = 128
VMEM_LIMIT = 56 * 1024 * 1024
LOG2E = 1.4426950408889634

BF16 = jnp.bfloat16
F32 = jnp.float32


def _params(sem):
    return pltpu.CompilerParams(dimension_semantics=sem, vmem_limit_bytes=VMEM_LIMIT)


def _rms(x, g):
    ms = jnp.mean(x * x, axis=-1, keepdims=True)
    return x * lax.rsqrt(ms + EPS) * g


def _rope128(x, c, s1, s2):
    return x * c + pltpu.roll(x, 96, 1) * s1 + pltpu.roll(x, 32, 1) * s2


def _norm_kernel(x_ref, g_ref, o_ref):
    o_ref[...] = _rms(x_ref[...].astype(F32), g_ref[...]).astype(o_ref.dtype)


def _norm(x, g, tm):
    t, d = x.shape
    return pl.pallas_call(
        _norm_kernel,
        grid=(t // tm,),
        in_specs=[pl.BlockSpec((tm, d), lambda i: (i, 0)),
                  pl.BlockSpec((1, d), lambda i: (0, 0))],
        out_specs=pl.BlockSpec((tm, d), lambda i: (i, 0)),
        out_shape=jax.ShapeDtypeStruct((t, d), BF16),
        compiler_params=_params(("parallel",)),
        name="rmsnorm",
    )(x, g.reshape(1, d))


def _mm_kernel(a_ref, w_ref, o_ref):
    o_ref[...] = jnp.dot(a_ref[...], w_ref[...],
                         preferred_element_type=F32).astype(o_ref.dtype)


def _mm(a, w, tm, tn, name):
    m, k = a.shape
    n = w.shape[1]
    tm = min(tm, m)
    tn = min(tn, n)
    return pl.pallas_call(
        _mm_kernel,
        grid=(m // tm, n // tn),
        in_specs=[pl.BlockSpec((tm, k), lambda i, j: (i, 0)),
                  pl.BlockSpec((k, tn), lambda i, j: (0, j))],
        out_specs=pl.BlockSpec((tm, tn), lambda i, j: (i, j)),
        out_shape=jax.ShapeDtypeStruct((m, n), BF16),
        compiler_params=_params(("parallel", "arbitrary")),
        name=name,
    )(a, w)


def _latent_kernel(h_ref, w_ref, qg_ref, kvg_ref, c_ref, s1_ref, s2_ref, cq_ref, ckv_ref):
    z = jnp.dot(h_ref[...], w_ref[...], preferred_element_type=F32)
    cq_ref[...] = _rms(z[:, :Q_LORA], qg_ref[...]).astype(cq_ref.dtype)
    lat = _rms(z[:, Q_LORA:Q_LORA + KV_LORA], kvg_ref[...])
    ckv_ref[:, :KV_LORA] = lat.astype(ckv_ref.dtype)
    kpe = _rope128(z[:, Q_LORA + KV_LORA:], c_ref[...], s1_ref[...], s2_ref[...])
    ckv_ref[:, KV_LORA:] = kpe.astype(ckv_ref.dtype)


def _latent(h, w, qg, kvg, tabs, seq, tm):
    t, d = h.shape
    n = w.shape[1]
    nblk = seq // tm
    tab = pl.BlockSpec((tm, LANES), lambda i: (i % nblk, 0))
    return pl.pallas_call(
        _latent_kernel,
        grid=(t // tm,),
        in_specs=[pl.BlockSpec((tm, d), lambda i: (i, 0)),
                  pl.BlockSpec((d, n), lambda i: (0, 0)),
                  pl.BlockSpec((1, Q_LORA), lambda i: (0, 0)),
                  pl.BlockSpec((1, KV_LORA), lambda i: (0, 0)),
                  tab, tab, tab],
        out_specs=[pl.BlockSpec((tm, Q_LORA), lambda i: (i, 0)),
                   pl.BlockSpec((tm, CKV_EXT), lambda i: (i, 0))],
        out_shape=[jax.ShapeDtypeStruct((t, Q_LORA), BF16),
                   jax.ShapeDtypeStruct((t, CKV_EXT), BF16)],
        compiler_params=_params(("parallel",)),
        name="latent_proj",
    )(h, w, qg.reshape(1, -1), kvg.reshape(1, -1), *tabs)


def _qup_kernel(cq_ref, w_ref, c_ref, s1_ref, s2_ref, q_ref, *, scale):
    z = jnp.dot(cq_ref[...], w_ref[...], preferred_element_type=F32) * scale
    c, s1, s2 = c_ref[...], s1_ref[...], s2_ref[...]
    for h in range(MLA_HEADS):
        lo = h * QK_PAD
        q_ref[:, lo:lo + QK_NOPE] = z[:, lo:lo + QK_NOPE].astype(q_ref.dtype)
        pe = _rope128(z[:, lo + QK_NOPE:lo + QK_PAD], c, s1, s2)
        q_ref[:, lo + QK_NOPE:lo + QK_PAD] = pe.astype(q_ref.dtype)


def _qup(cq, w, tabs, seq, tm, scale):
    t, k = cq.shape
    n = w.shape[1]
    nblk = seq // tm
    tab = pl.BlockSpec((tm, LANES), lambda i: (i % nblk, 0))
    return pl.pallas_call(
        functools.partial(_qup_kernel, scale=scale),
        grid=(t // tm,),
        in_specs=[pl.BlockSpec((tm, k), lambda i: (i, 0)),
                  pl.BlockSpec((k, n), lambda i: (0, 0)),
                  tab, tab, tab],
        out_specs=pl.BlockSpec((tm, n), lambda i: (i, 0)),
        out_shape=jax.ShapeDtypeStruct((t, n), BF16),
        compiler_params=_params(("parallel",)),
        name="q_up_rope",
    )(cq, w, *tabs)


def _mla_kernel(q_ref, k_ref, v_ref, o_ref, *, tq, tk):
    n_q = q_ref.shape[0] // tq
    n_k = k_ref.shape[0] // tk

    def q_tile(qi, carry):
        q0 = pl.multiple_of(qi * tq, tq)
        q = q_ref[pl.ds(q0, tq), :]

        def kv_step(c, mla):
            m, l, acc = mla
            k0 = pl.multiple_of(c * tk, tk)
            k = k_ref[pl.ds(k0, tk), :]
            v = v_ref[pl.ds(k0, tk), :]
            s = lax.dot_general(q, k, (((1,), (1,)), ((), ())),
                                preferred_element_type=F32)
            m_new = jnp.maximum(m, jnp.max(s, axis=-1, keepdims=True))
            alpha = jnp.exp2(m - m_new)
            p = jnp.exp2(s - m_new)
            l = alpha * l + jnp.sum(p, axis=-1, keepdims=True)
            acc = alpha * acc + jnp.dot(p.astype(BF16), v, preferred_element_type=F32)
            return m_new, l, acc

        init = (jnp.full((tq, 1), -1e30, F32), jnp.zeros((tq, 1), F32),
                jnp.zeros((tq, V_HEAD), F32))
        _, l, acc = lax.fori_loop(0, n_k, kv_step, init)
        o_ref[pl.ds(q0, tq), :] = (acc * (1.0 / l)).astype(o_ref.dtype)
        return carry

    lax.fori_loop(0, n_q, q_tile, 0)


def _mla(q, kv, tq_outer, tq, tk):
    b, s, _ = q.shape
    tq_outer = min(tq_outer, s)
    tq = min(tq, tq_outer)
    tk = min(tk, s)
    v_blk0 = MLA_HEADS * QK_PAD // V_HEAD
    return pl.pallas_call(
        functools.partial(_mla_kernel, tq=tq, tk=tk),
        grid=(b, MLA_HEADS, s // tq_outer),
        in_specs=[pl.BlockSpec((None, tq_outer, QK_PAD), lambda bi, h, qi: (bi, qi, h)),
                  pl.BlockSpec((None, s, QK_PAD), lambda bi, h, qi: (bi, 0, h)),
                  pl.BlockSpec((None, s, V_HEAD), lambda bi, h, qi: (bi, 0, v_blk0 + h))],
        out_specs=pl.BlockSpec((None, tq_outer, V_HEAD), lambda bi, h, qi: (bi, qi, h)),
        out_shape=jax.ShapeDtypeStruct((b, s, MLA_HEADS * V_HEAD), BF16),
        compiler_params=_params(("parallel", "parallel", "arbitrary")),
        name="mla_attention",
    )(q, kv, kv)


def _mem_kernel(q_ref, kv_ref, o_ref, *, scale):
    for h in range(MEM_HEADS):
        lo = h * MEM_HEAD_DIM
        q = q_ref[:, lo:lo + MEM_HEAD_DIM]
        k = kv_ref[:, lo:lo + MEM_HEAD_DIM]
        v = kv_ref[:, MEM_DIM + lo:MEM_DIM + lo + MEM_HEAD_DIM]
        s = lax.dot_general(q, k, (((1,), (1,)), ((), ())),
                            preferred_element_type=F32) * scale
        p = jnp.exp2(s - jnp.max(s, axis=-1, keepdims=True))
        l = jnp.sum(p, axis=-1, keepdims=True)
        o = jnp.dot(p.astype(BF16), v, preferred_element_type=F32)
        o_ref[:, lo:lo + MEM_HEAD_DIM] = (o * (1.0 / l)).astype(o_ref.dtype)


def _mem_attn(z1, kvm, q_blk, tq):
    b, s, _ = z1.shape
    n_mem = kvm.shape[1]
    tq = min(tq, s)
    scale = MEM_HEAD_DIM ** -0.5 * LOG2E
    return pl.pallas_call(
        functools.partial(_mem_kernel, scale=scale),
        grid=(b, s // tq),
        in_specs=[pl.BlockSpec((None, tq, MEM_DIM), lambda bi, i: (bi, i, q_blk)),
                  pl.BlockSpec((None, n_mem, 2 * MEM_DIM), lambda bi, i: (bi, 0, 0))],
        out_specs=pl.BlockSpec((None, tq, MEM_DIM), lambda bi, i: (bi, i, 0)),
        out_shape=jax.ShapeDtypeStruct((b, s, MEM_DIM), BF16),
        compiler_params=_params(("parallel", "arbitrary")),
        name="mem_attention",
    )(z1, kvm)


def _fft_a_kernel(u_ref, f_ref, tc_ref, ts_ref, o_ref, *, n1):
    f = f_ref[...]
    for j in range(u_ref.shape[0]):
        r = jnp.dot(f, u_ref[j], preferred_element_type=F32)
        br, bi = r[:n1], r[n1:]
        tc, ts = tc_ref[j], ts_ref[j]
        o_ref[j, 0] = (br * tc + bi * ts).astype(o_ref.dtype)
        o_ref[j, 1] = (bi * tc - br * ts).astype(o_ref.dtype)


def _fft_b_kernel(b_ref, m_ref, cs_ref, o_ref):
    m2 = m_ref[...]
    cs = cs_ref[...]
    n2 = o_ref.shape[1]
    for j in range(b_ref.shape[0]):
        x = b_ref[j].reshape(2 * n2, FOURIER_DIM)
        z = jnp.dot(m2, x, preferred_element_type=F32).astype(BF16)
        for g in range(N_GROUPS):
            lo = g * GROUP_DIM
            zz = jnp.concatenate([z[:n2, lo:lo + GROUP_DIM], z[n2:, lo:lo + GROUP_DIM]],
                                 axis=1)
            y = jnp.dot(zz, cs, preferred_element_type=F32)
            o_ref[j, :, lo:lo + GROUP_DIM] = y.astype(o_ref.dtype)


def _dft_tables(seq):
    n2 = FFT_N2
    n1 = seq // n2
    a1 = 2.0 * np.pi * np.outer(np.arange(n1), np.arange(n1)) / n1
    f1 = np.concatenate([np.cos(a1), -np.sin(a1)], axis=0) / math.sqrt(n1)
    th = 2.0 * np.pi * np.outer(np.arange(n2), np.arange(n1)) / seq
    a2 = 2.0 * np.pi * np.outer(np.arange(n2), np.arange(n2)) / n2
    c2, s2 = np.cos(a2), np.sin(a2)
    m2 = np.block([[c2, s2], [-s2, c2]]) / math.sqrt(n2)
    ac = 2.0 * np.pi * np.outer(np.arange(GROUP_DIM), np.arange(GROUP_DIM)) / GROUP_DIM
    cs = np.concatenate([np.cos(ac), np.sin(ac)], axis=0) / math.sqrt(GROUP_DIM)
    return (jnp.asarray(f1, BF16), jnp.asarray(np.cos(th)[:, :, None], F32),
            jnp.asarray(np.sin(th)[:, :, None], F32), jnp.asarray(m2, BF16),
            jnp.asarray(cs, BF16))


def _fourier(u, t2, tk1):
    b, s, c = u.shape
    n2 = FFT_N2
    n1 = s // n2
    t2 = min(t2, n2)
    tk1 = min(tk1, n1)
    f1, tc, ts, m2, cs = _dft_tables(s)
    ut = u.reshape(b, n1, n2, c).transpose(0, 2, 1, 3)
    bt = pl.pallas_call(
        functools.partial(_fft_a_kernel, n1=n1),
        grid=(b, n2 // t2),
        in_specs=[pl.BlockSpec((None, t2, n1, c), lambda bi, i: (bi, i, 0, 0)),
                  pl.BlockSpec((2 * n1, n1), lambda bi, i: (0, 0)),
                  pl.BlockSpec((t2, n1, 1), lambda bi, i: (i, 0, 0)),
                  pl.BlockSpec((t2, n1, 1), lambda bi, i: (i, 0, 0))],
        out_specs=pl.BlockSpec((None, t2, 2, n1, c), lambda bi, i: (bi, i, 0, 0, 0)),
        out_shape=jax.ShapeDtypeStruct((b, n2, 2, n1, c), BF16),
        compiler_params=_params(("parallel", "arbitrary")),
        name="fft_stage_a",
    )(ut, f1, tc, ts)
    bk = bt.transpose(0, 3, 2, 1, 4)
    y = pl.pallas_call(
        _fft_b_kernel,
        grid=(b, n1 // tk1),
        in_specs=[pl.BlockSpec((None, tk1, 2, n2, c), lambda bi, i: (bi, i, 0, 0, 0)),
                  pl.BlockSpec((2 * n2, 2 * n2), lambda bi, i: (0, 0)),
                  pl.BlockSpec((2 * GROUP_DIM, GROUP_DIM), lambda bi, i: (0, 0))],
        out_specs=pl.BlockSpec((None, tk1, n2, c), lambda bi, i: (bi, i, 0, 0)),
        out_shape=jax.ShapeDtypeStruct((b, n1, n2, c), BF16),
        compiler_params=_params(("parallel", "arbitrary")),
        name="fft_stage_b",
    )(bk, m2, cs)
    return y.transpose(0, 2, 1, 3).reshape(b, s, c)


def _merge_kernel(yf_ref, oa_ref, om_ref, g0_ref, g1_ref, g2_ref,
                  wf_ref, wa_ref, wm_ref, o_ref):
    def gate(g_ref):
        return jax.nn.sigmoid(g_ref[...].astype(F32))
    yf = jnp.dot(yf_ref[...], wf_ref[...], preferred_element_type=F32)
    ya = jnp.dot(oa_ref[...], wa_ref[...], preferred_element_type=F32)
    ym = jnp.dot(om_ref[...], wm_ref[...], preferred_element_type=F32)
    o_ref[...] = (gate(g0_ref) * yf + gate(g1_ref) * ya + gate(g2_ref) * ym
                  ).astype(o_ref.dtype)


def _merge(yf, oa, om, z1, wf, wa, wm, tm, tn):
    t = yf.shape[0]
    tm = min(tm, t)
    nb = D_MODEL // tn
    def gspec(k):
        return pl.BlockSpec((tm, tn), lambda i, j: (i, k * nb + j))
    def aspec(w):
        return pl.BlockSpec((tm, w), lambda i, j: (i, 0))
    def wspec(w):
        return pl.BlockSpec((w, tn), lambda i, j: (0, j))
    return pl.pallas_call(
        _merge_kernel,
        grid=(t // tm, nb),
        in_specs=[aspec(FOURIER_DIM), aspec(MLA_HEADS * V_HEAD), aspec(MEM_DIM),
                  gspec(0), gspec(1), gspec(2),
                  wspec(FOURIER_DIM), wspec(MLA_HEADS * V_HEAD), wspec(MEM_DIM)],
        out_specs=pl.BlockSpec((tm, tn), lambda i, j: (i, j)),
        out_shape=jax.ShapeDtypeStruct((t, D_MODEL), BF16),
        compiler_params=_params(("parallel", "arbitrary")),
        name="branch_merge",
    )(yf, oa, om, z1, z1, z1, wf, wa, wm)


def _oproj_kernel(a_ref, w_ref, x_ref, gpost_ref, gpre_ref, x1_ref, h_ref):
    y = jnp.dot(a_ref[...], w_ref[...], preferred_element_type=F32)
    x1 = x_ref[...] + _rms(y, gpost_ref[...])
    x1_ref[...] = x1
    h_ref[...] = _rms(x1, gpre_ref[...]).astype(h_ref.dtype)


def _oproj(a, w, x, gpost, gpre, tm):
    t, d = x.shape
    tm = min(tm, t)
    row = lambda i: (i, 0)
    fix = lambda i: (0, 0)
    return pl.pallas_call(
        _oproj_kernel,
        grid=(t // tm,),
        in_specs=[pl.BlockSpec((tm, d), row), pl.BlockSpec((d, d), fix),
                  pl.BlockSpec((tm, d), row), pl.BlockSpec((1, d), fix),
                  pl.BlockSpec((1, d), fix)],
        out_specs=[pl.BlockSpec((tm, d), row), pl.BlockSpec((tm, d), row)],
        out_shape=[jax.ShapeDtypeStruct((t, d), F32), jax.ShapeDtypeStruct((t, d), BF16)],
        compiler_params=_params(("parallel",)),
        name="out_proj_norm",
    )(a, w, x, gpost.reshape(1, d), gpre.reshape(1, d))


def _gelu_tanh(x):
    return 0.5 * x * (1.0 + jnp.tanh(0.7978845608028654 * (x + 0.044715 * x * x * x)))


def _down_kernel(g_ref, gp_ref, gn_ref, u_ref, cw_ref, cb_ref, w_ref, x_ref,
                 gpost_ref, gpre_ref, *out_refs, tiles_per_seq, halo, emit_h):
    acc_ref = out_refs[-1]
    i = pl.program_id(0)
    k = pl.program_id(1)
    tm = g_ref.shape[0]
    g = g_ref[...].astype(F32)
    pos = i % tiles_per_seq
    prev_row = jnp.where(pos == 0, 0.0, gp_ref[halo - 1:halo, :].astype(F32))
    next_row = jnp.where(pos == tiles_per_seq - 1, 0.0, gn_ref[0:1, :].astype(F32))
    rows = lax.broadcasted_iota(jnp.int32, g.shape, 0)
    g_prev = jnp.where(rows == 0, prev_row, pltpu.roll(g, 1, 0))
    g_next = jnp.where(rows == tm - 1, next_row, pltpu.roll(g, tm - 1, 0))
    cw = cw_ref[...]
    conv = g_prev * cw[0:1] + g * cw[1:2] + g_next * cw[2:3] + cb_ref[...]
    act = (_gelu_tanh(conv) * u_ref[...].astype(F32)).astype(BF16)
    part = jnp.dot(act, w_ref[...], preferred_element_type=F32)

    @pl.when(k == 0)
    def _():
        acc_ref[...] = part

    @pl.when(k > 0)
    def _():
        acc_ref[...] += part

    @pl.when(k == pl.num_programs(1) - 1)
    def _():
        x2 = x_ref[...] + _rms(acc_ref[...], gpost_ref[...])
        out_refs[0][...] = x2
        if emit_h:
            out_refs[1][...] = _rms(x2, gpre_ref[...]).astype(out_refs[1].dtype)


def _down(gu, cw, cb, w, x, gpost, gpre, seq, tm, tk, emit_h):
    t, d = x.shape
    tm = min(tm, seq)
    halo = 16
    nk = D_FF // tk
    tiles_per_seq = seq // tm
    hb = tm // halo
    last_hb = t // halo - 1
    row = lambda i, k: (i, 0)
    fix = lambda i, k: (0, 0)
    out_specs = [pl.BlockSpec((tm, d), row)]
    out_shape = [jax.ShapeDtypeStruct((t, d), F32)]
    if emit_h:
        out_specs.append(pl.BlockSpec((tm, d), row))
        out_shape.append(jax.ShapeDtypeStruct((t, d), BF16))
    outs = pl.pallas_call(
        functools.partial(_down_kernel, tiles_per_seq=tiles_per_seq, halo=halo,
                          emit_h=emit_h),
        grid=(t // tm, nk),
        in_specs=[pl.BlockSpec((tm, tk), lambda i, k: (i, k)),
                  pl.BlockSpec((halo, tk), lambda i, k: (jnp.maximum(i * hb - 1, 0), k)),
                  pl.BlockSpec((halo, tk),
                               lambda i, k: (jnp.minimum((i + 1) * hb, last_hb), k)),
                  pl.BlockSpec((tm, tk), lambda i, k: (i, nk + k)),
                  pl.BlockSpec((3, tk), lambda i, k: (0, k)),
                  pl.BlockSpec((1, tk), lambda i, k: (0, k)),
                  pl.BlockSpec((tk, d), lambda i, k: (k, 0)),
                  pl.BlockSpec((tm, d), row),
                  pl.BlockSpec((1, d), fix), pl.BlockSpec((1, d), fix)],
        out_specs=out_specs,
        out_shape=out_shape,
        scratch_shapes=[pltpu.VMEM((tm, d), F32)],
        compiler_params=_params(("parallel", "arbitrary")),
        name="ffn_down_norm",
    )(gu, gu, gu, gu, cw, cb.reshape(1, -1), w, x, gpost.reshape(1, d), gpre.reshape(1, d))
    return outs if emit_h else (outs[0], None)


def _pack_weights(p, l):
    s0 = FOURIER_DIM
    s1 = s0 + Q_LORA
    s2 = s1 + KV_LORA
    s3 = s2 + QK_ROPE
    s4 = s3 + MEM_DIM
    w_in = p["w_in"][l]
    d = w_in.shape[0]
    bf = lambda a: a.astype(BF16)
    w_gq = bf(jnp.concatenate([w_in[:, s4:], w_in[:, s3:s4]], axis=1))
    w_f = bf(w_in[:, :s0])
    w_lat = bf(jnp.concatenate([w_in[:, s0:s3], jnp.zeros((d, KPE_PAD - QK_ROPE), F32)],
                               axis=1))
    wq = p["w_uq"][l].reshape(Q_LORA, MLA_HEADS, QK_NOPE + QK_ROPE)
    wq = jnp.pad(wq, ((0, 0), (0, 0), (0, QK_PAD - QK_NOPE - QK_ROPE)))
    wq = bf(wq.reshape(Q_LORA, MLA_HEADS * QK_PAD))
    wkv = p["w_ukv"][l].reshape(KV_LORA, MLA_HEADS, QK_NOPE + V_HEAD)
    wk = jnp.pad(wkv[:, :, :QK_NOPE], ((0, KPE_PAD), (0, 0), (0, QK_PAD - QK_NOPE)))
    eye = jnp.eye(KPE_PAD, QK_PAD - QK_NOPE, dtype=F32)[:, None, :]
    eye = eye * (jnp.arange(KPE_PAD) < QK_ROPE)[:, None, None]
    wk = wk.at[KV_LORA:, :, QK_NOPE:].set(jnp.broadcast_to(
        eye, (KPE_PAD, MLA_HEADS, QK_PAD - QK_NOPE)))
    wv = jnp.pad(wkv[:, :, QK_NOPE:], ((0, KPE_PAD), (0, 0), (0, 0)))
    w_kv = bf(jnp.concatenate([wk.reshape(CKV_EXT, -1), wv.reshape(CKV_EXT, -1)], axis=1))
    w_gu = bf(jnp.concatenate([p["w_ffn_gate"][l], p["w_ffn_up"][l]], axis=1))
    return dict(
        w_gq=w_gq, w_f=w_f, w_lat=w_lat, wq=wq, w_kv=w_kv, w_gu=w_gu,
        w_mem_kv=bf(p["w_mem_kv"][l]), w_fo=bf(p["w_fourier_out"][l]),
        w_ao=bf(p["w_attn_out"][l]), w_mo=bf(p["w_mem_out"][l]), w_o=bf(p["w_o"][l]),
        w_down=bf(p["w_ffn_down"][l]), conv_w=p["ffn_conv_w"][l], conv_b=p["ffn_conv_b"][l],
        q_norm=p["q_norm"][l], kv_norm=p["kv_norm"][l], mem_norm=p["mem_norm"][l],
        pre_mix=p["pre_mix_norm"][l], post_mix=p["post_mix_norm"][l],
        pre_ffn=p["pre_ffn_norm"][l], post_ffn=p["post_ffn_norm"][l])


def _rope_tables(seq):
    inv_freq = 1.0 / (ROPE_THETA ** (jnp.arange(0, QK_ROPE, 2, dtype=F32) / QK_ROPE))
    ang = jnp.arange(seq, dtype=F32)[:, None] * inv_freq[None, :]
    cos, sin = jnp.cos(ang), jnp.sin(ang)
    half = QK_ROPE // 2
    zh = jnp.zeros((seq, half), F32)
    zr = jnp.zeros((seq, LANES - QK_ROPE), F32)
    c = jnp.concatenate([cos, cos, zr], axis=1)
    s1 = jnp.concatenate([-sin, zh, zr], axis=1)
    s2 = jnp.concatenate([zh, sin, zr], axis=1)
    return c, s1, s2


def _trunk(x, mem, layers):
    b, s, d = x.shape
    t = b * s
    n_mem = mem.shape[1]
    tm = min(512, s)
    tabs = _rope_tables(s)
    q_scale = (QK_NOPE + QK_ROPE) ** -0.5 * LOG2E
    gate_w = 3 * D_MODEL
    xf = x.reshape(t, d)
    memf = mem.reshape(b * n_mem, d)
    h = _norm(xf, layers[0]["pre_mix"], tm)
    for l, w in enumerate(layers):
        z1 = _mm(h, w["w_gq"], 1024, 1024, "in_proj_gates")
        f_in = _mm(h, w["w_f"], 1024, 1024, "in_proj_fourier")
        cq, ckv = _latent(h, w["w_lat"], w["q_norm"], w["kv_norm"], tabs, s, tm)
        q = _qup(cq, w["wq"], tabs, s, tm, q_scale)
        kv = _mm(ckv, w["w_kv"], 1024, 1024, "kv_up")
        oa = _mla(q.reshape(b, s, -1), kv.reshape(b, s, -1), 2048, 512, 512)
        mem_n = _norm(memf, w["mem_norm"], n_mem)
        kvm = _mm(mem_n, w["w_mem_kv"], 1024, 1024, "mem_kv")
        om = _mem_attn(z1.reshape(b, s, -1), kvm.reshape(b, n_mem, -1),
                       gate_w // MEM_DIM, 1024)
        yf = _fourier(f_in.reshape(b, s, -1), 16, 16)
        merged = _merge(yf.reshape(t, -1), oa.reshape(t, -1), om.reshape(t, -1), z1,
                        w["w_fo"], w["w_ao"], w["w_mo"], 1024, 512)
        x1, h2 = _oproj(merged, w["w_o"], xf, w["post_mix"], w["pre_ffn"], 256)
        gu = _mm(h2, w["w_gu"], 1024, 1024, "ffn_gate_up")
        last = l == len(layers) - 1
        nxt = layers[0 if last else l + 1]["pre_mix"]
        xf, h = _down(gu, w["conv_w"], w["conv_b"], w["w_down"], x1, w["post_ffn"], nxt,
                      s, 512, 512, not last)
    return xf.reshape(b, s, d)


def kernel(x_prompt, x_sample, mem_prompt, mem_sample, pre_mix_norm, w_in, q_norm, w_uq,
           kv_norm, w_ukv, mem_norm, w_mem_kv, w_fourier_out, w_attn_out, w_mem_out, w_o,
           post_mix_norm, pre_ffn_norm, w_ffn_gate, w_ffn_up, ffn_conv_w, ffn_conv_b,
           w_ffn_down, post_ffn_norm):
    p = dict(pre_mix_norm=pre_mix_norm, w_in=w_in, q_norm=q_norm, w_uq=w_uq,
             kv_norm=kv_norm, w_ukv=w_ukv, mem_norm=mem_norm, w_mem_kv=w_mem_kv,
             w_fourier_out=w_fourier_out, w_attn_out=w_attn_out, w_mem_out=w_mem_out,
             w_o=w_o, post_mix_norm=post_mix_norm, pre_ffn_norm=pre_ffn_norm,
             w_ffn_gate=w_ffn_gate, w_ffn_up=w_ffn_up, ffn_conv_w=ffn_conv_w,
             ffn_conv_b=ffn_conv_b, w_ffn_down=w_ffn_down, post_ffn_norm=post_ffn_norm)
    layers = [_pack_weights(p, l) for l in range(w_in.shape[0])]
    return (_trunk(x_prompt, mem_prompt, layers), _trunk(x_sample, mem_sample, layers))
```

```python
import functools
import math

import numpy as np
import jax
import jax.numpy as jnp
from jax import lax
from jax.experimental import pallas as pl
from jax.experimental.pallas import tpu as pltpu

D_MODEL = 2048
N_GROUPS = 4
GROUP_DIM = 256
FOURIER_DIM = N_GROUPS * GROUP_DIM
MLA_HEADS = 16
Q_LORA = 512
KV_LORA = 512
QK_NOPE = 128
QK_ROPE = 64
V_HEAD = 128
ROPE_THETA = 10000.0
MEM_HEADS = 4
MEM_HEAD_DIM = 256
MEM_DIM = MEM_HEADS * MEM_HEAD_DIM
D_FF = 5632
EPS = 1e-6

LANES = 128
QK_PAD = 256
KPE_PAD = LANES
CKV_EXT = KV_LORA + KPE_PAD
ONE_LANE = KV_LORA + QK_ROPE
V_ROWS = V_HEAD + 16
FFT_N2 = 128
VMEM_LIMIT = 56 * 1024 * 1024
LOG2E = 1.4426950408889634

BF16 = jnp.bfloat16
F32 = jnp.float32


def _params(sem):
    return pltpu.CompilerParams(dimension_semantics=sem, vmem_limit_bytes=VMEM_LIMIT)


def _rms(x, g):
    ms = jnp.mean(x * x, axis=-1, keepdims=True)
    return x * lax.rsqrt(ms + EPS) * g


def _rope128(x, c, s1, s2):
    return x * c + pltpu.roll(x, 96, 1) * s1 + pltpu.roll(x, 32, 1) * s2


def _norm_kernel(x_ref, g_ref, o_ref):
    o_ref[...] = _rms(x_ref[...].astype(F32), g_ref[...]).astype(o_ref.dtype)


def _norm(x, g, tm):
    t, d = x.shape
    return pl.pallas_call(
        _norm_kernel,
        grid=(t // tm,),
        in_specs=[pl.BlockSpec((tm, d), lambda i: (i, 0)),
                  pl.BlockSpec((1, d), lambda i: (0, 0))],
        out_specs=pl.BlockSpec((tm, d), lambda i: (i, 0)),
        out_shape=jax.ShapeDtypeStruct((t, d), BF16),
        compiler_params=_params(("parallel",)),
        name="rmsnorm",
    )(x, g.reshape(1, d))


def _mm_kernel(a_ref, w_ref, o_ref):
    o_ref[...] = jnp.dot(a_ref[...], w_ref[...],
                         preferred_element_type=F32).astype(o_ref.dtype)


def _mm(a, w, tm, tn, name):
    m, k = a.shape
    n = w.shape[1]
    tm = min(tm, m)
    tn = min(tn, n)
    return pl.pallas_call(
        _mm_kernel,
        grid=(m // tm, n // tn),
        in_specs=[pl.BlockSpec((tm, k), lambda i, j: (i, 0)),
                  pl.BlockSpec((k, tn), lambda i, j: (0, j))],
        out_specs=pl.BlockSpec((tm, tn), lambda i, j: (i, j)),
        out_shape=jax.ShapeDtypeStruct((m, n), BF16),
        compiler_params=_params(("parallel", "arbitrary")),
        name=name,
    )(a, w)


def _latent_kernel(h_ref, w_ref, qg_ref, kvg_ref, c_ref, s1_ref, s2_ref, cq_ref, ckv_ref):
    z = jnp.dot(h_ref[...], w_ref[...], preferred_element_type=F32)
    cq_ref[...] = _rms(z[:, :Q_LORA], qg_ref[...]).astype(cq_ref.dtype)
    lat = _rms(z[:, Q_LORA:Q_LORA + KV_LORA], kvg_ref[...])
    ckv_ref[:, :KV_LORA] = lat.astype(ckv_ref.dtype)
    kpe = _rope128(z[:, Q_LORA + KV_LORA:], c_ref[...], s1_ref[...], s2_ref[...])
    lane = lax.broadcasted_iota(jnp.int32, kpe.shape, 1)
    kpe = jnp.where(lane == ONE_LANE - KV_LORA, 1.0, kpe)
    ckv_ref[:, KV_LORA:] = kpe.astype(ckv_ref.dtype)


def _latent(h, w, qg, kvg, tabs, seq, tm):
    t, d = h.shape
    n = w.shape[1]
    nblk = seq // tm
    tab = pl.BlockSpec((tm, LANES), lambda i: (i % nblk, 0))
    return pl.pallas_call(
        _latent_kernel,
        grid=(t // tm,),
        in_specs=[pl.BlockSpec((tm, d), lambda i: (i, 0)),
                  pl.BlockSpec((d, n), lambda i: (0, 0)),
                  pl.BlockSpec((1, Q_LORA), lambda i: (0, 0)),
                  pl.BlockSpec((1, KV_LORA), lambda i: (0, 0)),
                  tab, tab, tab],
        out_specs=[pl.BlockSpec((tm, Q_LORA), lambda i: (i, 0)),
                   pl.BlockSpec((tm, CKV_EXT), lambda i: (i, 0))],
        out_shape=[jax.ShapeDtypeStruct((t, Q_LORA), BF16),
                   jax.ShapeDtypeStruct((t, CKV_EXT), BF16)],
        compiler_params=_params(("parallel",)),
        name="latent_proj",
    )(h, w, qg.reshape(1, -1), kvg.reshape(1, -1), *tabs)


def _qup_kernel(cq_ref, w_ref, c_ref, s1_ref, s2_ref, q_ref, *, scale):
    z = jnp.dot(cq_ref[...], w_ref[...], preferred_element_type=F32) * scale
    c, s1, s2 = c_ref[...], s1_ref[...], s2_ref[...]
    for h in range(MLA_HEADS):
        lo = h * QK_PAD
        q_ref[:, lo:lo + QK_NOPE] = z[:, lo:lo + QK_NOPE].astype(q_ref.dtype)
        pe = _rope128(z[:, lo + QK_NOPE:lo + QK_PAD], c, s1, s2)
        q_ref[:, lo + QK_NOPE:lo + QK_PAD] = pe.astype(q_ref.dtype)


def _qup(cq, w, tabs, seq, tm, scale):
    t, k = cq.shape
    n = w.shape[1]
    nblk = seq // tm
    tab = pl.BlockSpec((tm, LANES), lambda i: (i % nblk, 0))
    return pl.pallas_call(
        functools.partial(_qup_kernel, scale=scale),
        grid=(t // tm,),
        in_specs=[pl.BlockSpec((tm, k), lambda i: (i, 0)),
                  pl.BlockSpec((k, n), lambda i: (0, 0)),
                  tab, tab, tab],
        out_specs=pl.BlockSpec((tm, n), lambda i: (i, 0)),
        out_shape=jax.ShapeDtypeStruct((t, n), BF16),
        compiler_params=_params(("parallel",)),
        name="q_up_rope",
    )(cq, w, *tabs)


def _vup_kernel(c_ref, w_ref, o_ref):
    o_ref[...] = lax.dot_general(w_ref[...], c_ref[...], (((1,), (1,)), ((), ())),
                                 preferred_element_type=F32).astype(o_ref.dtype)


def _vup(ckv, w_vt, ts):
    b, s, c = ckv.shape
    n = w_vt.shape[0]
    ts = min(ts, s)
    return pl.pallas_call(
        _vup_kernel,
        grid=(b, s // ts),
        in_specs=[pl.BlockSpec((None, ts, c), lambda bi, i: (bi, i, 0)),
                  pl.BlockSpec((n, c), lambda bi, i: (0, 0))],
        out_specs=pl.BlockSpec((None, n, ts), lambda bi, i: (bi, 0, i)),
        out_shape=jax.ShapeDtypeStruct((b, n, s), BF16),
        compiler_params=_params(("parallel", "arbitrary")),
        name="v_up_t",
    )(ckv, w_vt)


def _mla_kernel(q_ref, k_ref, vt_ref, o_ref, s_ref, p_ref, *, tq, tk):
    n_q = q_ref.shape[0] // tq
    n_k = k_ref.shape[0] // tk

    def q_tile(qi, carry):
        q0 = pl.multiple_of(qi * tq, tq)
        q = q_ref[pl.ds(q0, tq), :]

        def scores(c, slot):
            k = k_ref[pl.ds(pl.multiple_of(c * tk, tk), tk), :]
            s = lax.dot_general(k, q, (((1,), (1,)), ((), ())), preferred_element_type=F32)
            s_ref[slot] = s
            return jnp.max(s, axis=0, keepdims=True)

        def softmax(slot, cmax, m):
            m_new = jnp.maximum(m, cmax)
            p_ref[slot] = jnp.exp2(s_ref[slot] - m_new).astype(BF16)
            return m_new, jnp.exp2(m - m_new)

        def pv(c, slot):
            vt = vt_ref[:, pl.ds(pl.multiple_of(c * tk, tk), tk)]
            return jnp.dot(vt, p_ref[slot], preferred_element_type=F32)

        def stage(c, slot, m, acc, cmax):
            part = pv(c - 1, 1 - slot)
            cmax_next = scores(c + 1, 1 - slot)
            m, alpha = softmax(slot, cmax, m)
            return m, (acc + part) * alpha, cmax_next

        cmax = scores(0, 0)
        m, _ = softmax(0, cmax, jnp.full((1, tq), -1e30, F32))
        cmax = scores(1, 1)
        acc = jnp.zeros((V_ROWS, tq), F32)

        def pair(j, st):
            m, acc, cmax = st
            c = 2 * j + 1
            m, acc, cmax = stage(c, 1, m, acc, cmax)
            return stage(c + 1, 0, m, acc, cmax)

        m, acc, cmax = lax.fori_loop(0, (n_k - 2) // 2, pair, (m, acc, cmax))
        part = pv(n_k - 2, 0)
        m, alpha = softmax(1, cmax, m)
        acc = (acc + part) * alpha + pv(n_k - 1, 1)
        out = acc[:V_HEAD] * (1.0 / acc[V_HEAD:V_HEAD + 1])
        o_ref[pl.ds(q0, tq), :] = out.T.astype(o_ref.dtype)
        return carry

    lax.fori_loop(0, n_q, q_tile, 0)


def _mla(q, k, vt, tq_outer, tq, tk):
    b, s, _ = q.shape
    tq_outer = min(tq_outer, s)
    tq = min(tq, tq_outer)
    tk = min(tk, s // 2)
    assert (s // tk) % 2 == 0
    return pl.pallas_call(
        functools.partial(_mla_kernel, tq=tq, tk=tk),
        grid=(b, MLA_HEADS, s // tq_outer),
        in_specs=[pl.BlockSpec((None, tq_outer, QK_PAD), lambda bi, h, qi: (bi, qi, h)),
                  pl.BlockSpec((None, s, QK_PAD), lambda bi, h, qi: (bi, 0, h)),
                  pl.BlockSpec((None, V_ROWS, s), lambda bi, h, qi: (bi, h, 0))],
        out_specs=pl.BlockSpec((None, tq_outer, V_HEAD), lambda bi, h, qi: (bi, qi, h)),
        out_shape=jax.ShapeDtypeStruct((b, s, MLA_HEADS * V_HEAD), BF16),
        scratch_shapes=[pltpu.VMEM((2, tk, tq), F32), pltpu.VMEM((2, tk, tq), BF16)],
        compiler_params=_params(("parallel", "parallel", "arbitrary")),
        name="mla_attention",
    )(q, k, vt)


def _mem_kernel(q_ref, kv_ref, o_ref, *, scale):
    for h in range(MEM_HEADS):
        lo = h * MEM_HEAD_DIM
        q = q_ref[:, lo:lo + MEM_HEAD_DIM]
        k = kv_ref[:, lo:lo + MEM_HEAD_DIM]
        v = kv_ref[:, MEM_DIM + lo:MEM_DIM + lo + MEM_HEAD_DIM]
        s = lax.dot_general(q, k, (((1,), (1,)), ((), ())),
                            preferred_element_type=F32) * scale
        p = jnp.exp2(s - jnp.max(s, axis=-1, keepdims=True))
        l = jnp.sum(p, axis=-1, keepdims=True)
        o = jnp.dot(p.astype(BF16), v, preferred_element_type=F32)
        o_ref[:, lo:lo + MEM_HEAD_DIM] = (o * (1.0 / l)).astype(o_ref.dtype)


def _mem_attn(z1, kvm, q_blk, tq):
    b, s, _ = z1.shape
    n_mem = kvm.shape[1]
    tq = min(tq, s)
    scale = MEM_HEAD_DIM ** -0.5 * LOG2E
    return pl.pallas_call(
        functools.partial(_mem_kernel, scale=scale),
        grid=(b, s // tq),
        in_specs=[pl.BlockSpec((None, tq, MEM_DIM), lambda bi, i: (bi, i, q_blk)),
                  pl.BlockSpec((None, n_mem, 2 * MEM_DIM), lambda bi, i: (bi, 0, 0))],
        out_specs=pl.BlockSpec((None, tq, MEM_DIM), lambda bi, i: (bi, i, 0)),
        out_shape=jax.ShapeDtypeStruct((b, s, MEM_DIM), BF16),
        compiler_params=_params(("parallel", "arbitrary")),
        name="mem_attention",
    )(z1, kvm)


def _fft_a_kernel(u_ref, f_ref, tc_ref, ts_ref, o_ref, *, n1):
    f = f_ref[...]
    for j in range(u_ref.shape[0]):
        r = jnp.dot(f, u_ref[j], preferred_element_type=F32)
        br, bi = r[:n1], r[n1:]
        tc, ts = tc_ref[j], ts_ref[j]
        o_ref[j, 0] = (br * tc + bi * ts).astype(o_ref.dtype)
        o_ref[j, 1] = (bi * tc - br * ts).astype(o_ref.dtype)


def _fft_b_kernel(b_ref, m_ref, cs_ref, o_ref):
    m2 = m_ref[...]
    cs = cs_ref[...]
    n2 = o_ref.shape[1]
    for j in range(b_ref.shape[0]):
        x = b_ref[j].reshape(2 * n2, FOURIER_DIM)
        z = jnp.dot(m2, x, preferred_element_type=F32).astype(BF16)
        for g in range(N_GROUPS):
            lo = g * GROUP_DIM
            zz = jnp.concatenate([z[:n2, lo:lo + GROUP_DIM], z[n2:, lo:lo + GROUP_DIM]],
                                 axis=1)
            y = jnp.dot(zz, cs, preferred_element_type=F32)
            o_ref[j, :, lo:lo + GROUP_DIM] = y.astype(o_ref.dtype)


def _dft_tables(seq):
    n2 = FFT_N2
    n1 = seq // n2
    a1 = 2.0 * np.pi * np.outer(np.arange(n1), np.arange(n1)) / n1
    f1 = np.concatenate([np.cos(a1), -np.sin(a1)], axis=0) / math.sqrt(n1)
    th = 2.0 * np.pi * np.outer(np.arange(n2), np.arange(n1)) / seq
    a2 = 2.0 * np.pi * np.outer(np.arange(n2), np.arange(n2)) / n2
    c2, s2 = np.cos(a2), np.sin(a2)
    m2 = np.block([[c2, s2], [-s2, c2]]) / math.sqrt(n2)
    ac = 2.0 * np.pi * np.outer(np.arange(GROUP_DIM), np.arange(GROUP_DIM)) / GROUP_DIM
    cs = np.concatenate([np.cos(ac), np.sin(ac)], axis=0) / math.sqrt(GROUP_DIM)
    return (jnp.asarray(f1, BF16), jnp.asarray(np.cos(th)[:, :, None], F32),
            jnp.asarray(np.sin(th)[:, :, None], F32), jnp.asarray(m2, BF16),
            jnp.asarray(cs, BF16))


def _fourier(u, t2, tk1):
    b, s, c = u.shape
    n2 = FFT_N2
    n1 = s // n2
    t2 = min(t2, n2)
    tk1 = min(tk1, n1)
    f1, tc, ts, m2, cs = _dft_tables(s)
    ut = u.reshape(b, n1, n2, c).transpose(0, 2, 1, 3)
    bt = pl.pallas_call(
        functools.partial(_fft_a_kernel, n1=n1),
        grid=(b, n2 // t2),
        in_specs=[pl.BlockSpec((None, t2, n1, c), lambda bi, i: (bi, i, 0, 0)),
                  pl.BlockSpec((2 * n1, n1), lambda bi, i: (0, 0)),
                  pl.BlockSpec((t2, n1, 1), lambda bi, i: (i, 0, 0)),
                  pl.BlockSpec((t2, n1, 1), lambda bi, i: (i, 0, 0))],
        out_specs=pl.BlockSpec((None, t2, 2, n1, c), lambda bi, i: (bi, i, 0, 0, 0)),
        out_shape=jax.ShapeDtypeStruct((b, n2, 2, n1, c), BF16),
        compiler_params=_params(("parallel", "arbitrary")),
        name="fft_stage_a",
    )(ut, f1, tc, ts)
    bk = bt.transpose(0, 3, 2, 1, 4)
    y = pl.pallas_call(
        _fft_b_kernel,
        grid=(b, n1 // tk1),
        in_specs=[pl.BlockSpec((None, tk1, 2, n2, c), lambda bi, i: (bi, i, 0, 0, 0)),
                  pl.BlockSpec((2 * n2, 2 * n2), lambda bi, i: (0, 0)),
                  pl.BlockSpec((2 * GROUP_DIM, GROUP_DIM), lambda bi, i: (0, 0))],
        out_specs=pl.BlockSpec((None, tk1, n2, c), lambda bi, i: (bi, i, 0, 0)),
        out_shape=jax.ShapeDtypeStruct((b, n1, n2, c), BF16),
        compiler_params=_params(("parallel", "arbitrary")),
        name="fft_stage_b",
    )(bk, m2, cs)
    return y.transpose(0, 2, 1, 3).reshape(b, s, c)


def _merge_kernel(yf_ref, oa_ref, om_ref, g0_ref, g1_ref, g2_ref,
                  wf_ref, wa_ref, wm_ref, o_ref):
    def gate(g_ref):
        return jax.nn.sigmoid(g_ref[...].astype(F32))
    yf = jnp.dot(yf_ref[...], wf_ref[...], preferred_element_type=F32)
    ya = jnp.dot(oa_ref[...], wa_ref[...], preferred_element_type=F32)
    ym = jnp.dot(om_ref[...], wm_ref[...], preferred_element_type=F32)
    o_ref[...] = (gate(g0_ref) * yf + gate(g1_ref) * ya + gate(g2_ref) * ym
                  ).astype(o_ref.dtype)


def _merge(yf, oa, om, z1, wf, wa, wm, tm, tn):
    t = yf.shape[0]
    tm = min(tm, t)
    nb = D_MODEL // tn
    def gspec(k):
        return pl.BlockSpec((tm, tn), lambda i, j: (i, k * nb + j))
    def aspec(w):
        return pl.BlockSpec((tm, w), lambda i, j: (i, 0))
    def wspec(w):
        return pl.BlockSpec((w, tn), lambda i, j: (0, j))
    return pl.pallas_call(
        _merge_kernel,
        grid=(t // tm, nb),
        in_specs=[aspec(FOURIER_DIM), aspec(MLA_HEADS * V_HEAD), aspec(MEM_DIM),
                  gspec(0), gspec(1), gspec(2),
                  wspec(FOURIER_DIM), wspec(MLA_HEADS * V_HEAD), wspec(MEM_DIM)],
        out_specs=pl.BlockSpec((tm, tn), lambda i, j: (i, j)),
        out_shape=jax.ShapeDtypeStruct((t, D_MODEL), BF16),
        compiler_params=_params(("parallel", "arbitrary")),
        name="branch_merge",
    )(yf, oa, om, z1, z1, z1, wf, wa, wm)


def _oproj_kernel(a_ref, w_ref, x_ref, gpost_ref, gpre_ref, x1_ref, h_ref):
    y = jnp.dot(a_ref[...], w_ref[...], preferred_element_type=F32)
    x1 = x_ref[...] + _rms(y, gpost_ref[...])
    x1_ref[...] = x1
    h_ref[...] = _rms(x1, gpre_ref[...]).astype(h_ref.dtype)


def _oproj(a, w, x, gpost, gpre, tm):
    t, d = x.shape
    tm = min(tm, t)
    row = lambda i: (i, 0)
    fix = lambda i: (0, 0)
    return pl.pallas_call(
        _oproj_kernel,
        grid=(t // tm,),
        in_specs=[pl.BlockSpec((tm, d), row), pl.BlockSpec((d, d), fix),
                  pl.BlockSpec((tm, d), row), pl.BlockSpec((1, d), fix),
                  pl.BlockSpec((1, d), fix)],
        out_specs=[pl.BlockSpec((tm, d), row), pl.BlockSpec((tm, d), row)],
        out_shape=[jax.ShapeDtypeStruct((t, d), F32), jax.ShapeDtypeStruct((t, d), BF16)],
        compiler_params=_params(("parallel",)),
        name="out_proj_norm",
    )(a, w, x, gpost.reshape(1, d), gpre.reshape(1, d))


def _gelu_tanh(x):
    return 0.5 * x * (1.0 + jnp.tanh(0.7978845608028654 * (x + 0.044715 * x * x * x)))


def _down_kernel(g_ref, gp_ref, gn_ref, u_ref, cw_ref, cb_ref, w_ref, x_ref,
                 gpost_ref, gpre_ref, *out_refs, tiles_per_seq, halo, emit_h):
    acc_ref = out_refs[-1]
    i = pl.program_id(0)
    k = pl.program_id(1)
    tm = g_ref.shape[0]
    g = g_ref[...].astype(F32)
    pos = i % tiles_per_seq
    prev_row = jnp.where(pos == 0, 0.0, gp_ref[halo - 1:halo, :].astype(F32))
    next_row = jnp.where(pos == tiles_per_seq - 1, 0.0, gn_ref[0:1, :].astype(F32))
    rows = lax.broadcasted_iota(jnp.int32, g.shape, 0)
    g_prev = jnp.where(rows == 0, prev_row, pltpu.roll(g, 1, 0))
    g_next = jnp.where(rows == tm - 1, next_row, pltpu.roll(g, tm - 1, 0))
    cw = cw_ref[...]
    conv = g_prev * cw[0:1] + g * cw[1:2] + g_next * cw[2:3] + cb_ref[...]
    act = (_gelu_tanh(conv) * u_ref[...].astype(F32)).astype(BF16)
    part = jnp.dot(act, w_ref[...], preferred_element_type=F32)

    @pl.when(k == 0)
    def _():
        acc_ref[...] = part

    @pl.when(k > 0)
    def _():
        acc_ref[...] += part

    @pl.when(k == pl.num_programs(1) - 1)
    def _():
        x2 = x_ref[...] + _rms(acc_ref[...], gpost_ref[...])
        out_refs[0][...] = x2
        if emit_h:
            out_refs[1][...] = _rms(x2, gpre_ref[...]).astype(out_refs[1].dtype)


def _down(gu, cw, cb, w, x, gpost, gpre, seq, tm, tk, emit_h):
    t, d = x.shape
    tm = min(tm, seq)
    halo = 16
    nk = D_FF // tk
    tiles_per_seq = seq // tm
    hb = tm // halo
    last_hb = t // halo - 1
    row = lambda i, k: (i, 0)
    fix = lambda i, k: (0, 0)
    out_specs = [pl.BlockSpec((tm, d), row)]
    out_shape = [jax.ShapeDtypeStruct((t, d), F32)]
    if emit_h:
        out_specs.append(pl.BlockSpec((tm, d), row))
        out_shape.append(jax.ShapeDtypeStruct((t, d), BF16))
    outs = pl.pallas_call(
        functools.partial(_down_kernel, tiles_per_seq=tiles_per_seq, halo=halo,
                          emit_h=emit_h),
        grid=(t // tm, nk),
        in_specs=[pl.BlockSpec((tm, tk), lambda i, k: (i, k)),
                  pl.BlockSpec((halo, tk), lambda i, k: (jnp.maximum(i * hb - 1, 0), k)),
                  pl.BlockSpec((halo, tk),
                               lambda i, k: (jnp.minimum((i + 1) * hb, last_hb), k)),
                  pl.BlockSpec((tm, tk), lambda i, k: (i, nk + k)),
                  pl.BlockSpec((3, tk), lambda i, k: (0, k)),
                  pl.BlockSpec((1, tk), lambda i, k: (0, k)),
                  pl.BlockSpec((tk, d), lambda i, k: (k, 0)),
                  pl.BlockSpec((tm, d), row),
                  pl.BlockSpec((1, d), fix), pl.BlockSpec((1, d), fix)],
        out_specs=out_specs,
        out_shape=out_shape,
        scratch_shapes=[pltpu.VMEM((tm, d), F32)],
        compiler_params=_params(("parallel", "arbitrary")),
        name="ffn_down_norm",
    )(gu, gu, gu, gu, cw, cb.reshape(1, -1), w, x, gpost.reshape(1, d), gpre.reshape(1, d))
    return outs if emit_h else (outs[0], None)


def _pack_weights(p, l):
    s0 = FOURIER_DIM
    s1 = s0 + Q_LORA
    s2 = s1 + KV_LORA
    s3 = s2 + QK_ROPE
    s4 = s3 + MEM_DIM
    w_in = p["w_in"][l]
    d = w_in.shape[0]
    bf = lambda a: a.astype(BF16)
    w_gq = bf(jnp.concatenate([w_in[:, s4:], w_in[:, s3:s4]], axis=1))
    w_f = bf(w_in[:, :s0])
    w_lat = bf(jnp.concatenate([w_in[:, s0:s3], jnp.zeros((d, KPE_PAD - QK_ROPE), F32)],
                               axis=1))
    wq = p["w_uq"][l].reshape(Q_LORA, MLA_HEADS, QK_NOPE + QK_ROPE)
    wq = jnp.pad(wq, ((0, 0), (0, 0), (0, QK_PAD - QK_NOPE - QK_ROPE)))
    wq = bf(wq.reshape(Q_LORA, MLA_HEADS * QK_PAD))
    wkv = p["w_ukv"][l].reshape(KV_LORA, MLA_HEADS, QK_NOPE + V_HEAD)
    wk = jnp.pad(wkv[:, :, :QK_NOPE], ((0, KPE_PAD), (0, 0), (0, QK_PAD - QK_NOPE)))
    eye = jnp.eye(KPE_PAD, QK_PAD - QK_NOPE, dtype=F32)[:, None, :]
    eye = eye * (jnp.arange(KPE_PAD) < QK_ROPE)[:, None, None]
    wk = wk.at[KV_LORA:, :, QK_NOPE:].set(jnp.broadcast_to(
        eye, (KPE_PAD, MLA_HEADS, QK_PAD - QK_NOPE)))
    w_k = bf(wk.reshape(CKV_EXT, -1))
    wvt = jnp.pad(wkv[:, :, QK_NOPE:].transpose(1, 2, 0),
                  ((0, 0), (0, V_ROWS - V_HEAD), (0, KPE_PAD)))
    wvt = wvt.at[:, V_HEAD, ONE_LANE].set(1.0)
    w_vt = bf(wvt.reshape(MLA_HEADS * V_ROWS, CKV_EXT))
    w_gu = bf(jnp.concatenate([p["w_ffn_gate"][l], p["w_ffn_up"][l]], axis=1))
    return dict(
        w_gq=w_gq, w_f=w_f, w_lat=w_lat, wq=wq, w_k=w_k, w_vt=w_vt, w_gu=w_gu,
        w_mem_kv=bf(p["w_mem_kv"][l]), w_fo=bf(p["w_fourier_out"][l]),
        w_ao=bf(p["w_attn_out"][l]), w_mo=bf(p["w_mem_out"][l]), w_o=bf(p["w_o"][l]),
        w_down=bf(p["w_ffn_down"][l]), conv_w=p["ffn_conv_w"][l], conv_b=p["ffn_conv_b"][l],
        q_norm=p["q_norm"][l], kv_norm=p["kv_norm"][l], mem_norm=p["mem_norm"][l],
        pre_mix=p["pre_mix_norm"][l], post_mix=p["post_mix_norm"][l],
        pre_ffn=p["pre_ffn_norm"][l], post_ffn=p["post_ffn_norm"][l])


def _rope_tables(seq):
    inv_freq = 1.0 / (ROPE_THETA ** (jnp.arange(0, QK_ROPE, 2, dtype=F32) / QK_ROPE))
    ang = jnp.arange(seq, dtype=F32)[:, None] * inv_freq[None, :]
    cos, sin = jnp.cos(ang), jnp.sin(ang)
    half = QK_ROPE // 2
    zh = jnp.zeros((seq, half), F32)
    zr = jnp.zeros((seq, LANES - QK_ROPE), F32)
    c = jnp.concatenate([cos, cos, zr], axis=1)
    s1 = jnp.concatenate([-sin, zh, zr], axis=1)
    s2 = jnp.concatenate([zh, sin, zr], axis=1)
    return c, s1, s2


def _trunk(x, mem, layers):
    b, s, d = x.shape
    t = b * s
    n_mem = mem.shape[1]
    tm = min(512, s)
    tabs = _rope_tables(s)
    q_scale = (QK_NOPE + QK_ROPE) ** -0.5 * LOG2E
    gate_w = 3 * D_MODEL
    xf = x.reshape(t, d)
    memf = mem.reshape(b * n_mem, d)
    h = _norm(xf, layers[0]["pre_mix"], tm)
    for l, w in enumerate(layers):
        z1 = _mm(h, w["w_gq"], 1024, 1024, "in_proj_gates")
        f_in = _mm(h, w["w_f"], 1024, 1024, "in_proj_fourier")
        cq, ckv = _latent(h, w["w_lat"], w["q_norm"], w["kv_norm"], tabs, s, tm)
        q = _qup(cq, w["wq"], tabs, s, tm, q_scale)
        kc = _mm(ckv, w["w_k"], 1024, 1024, "k_up")
        vt = _vup(ckv.reshape(b, s, -1), w["w_vt"], 512)
        oa = _mla(q.reshape(b, s, -1), kc.reshape(b, s, -1), vt, 2048, 512, 512)
        mem_n = _norm(memf, w["mem_norm"], n_mem)
        kvm = _mm(mem_n, w["w_mem_kv"], 1024, 1024, "mem_kv")
        om = _mem_attn(z1.reshape(b, s, -1), kvm.reshape(b, n_mem, -1),
                       gate_w // MEM_DIM, 1024)
        yf = _fourier(f_in.reshape(b, s, -1), 16, 16)
        merged = _merge(yf.reshape(t, -1), oa.reshape(t, -1), om.reshape(t, -1), z1,
                        w["w_fo"], w["w_ao"], w["w_mo"], 1024, 512)
        x1, h2 = _oproj(merged, w["w_o"], xf, w["post_mix"], w["pre_ffn"], 256)
        gu = _mm(h2, w["w_gu"], 1024, 1024, "ffn_gate_up")
        last = l == len(layers) - 1
        nxt = layers[0 if last else l + 1]["pre_mix"]
        xf, h = _down(gu, w["conv_w"], w["conv_b"], w["w_down"], x1, w["post_ffn"], nxt,
                      s, 512, 512, not last)
    return xf.reshape(b, s, d)


def kernel(x_prompt, x_sample, mem_prompt, mem_sample, pre_mix_norm, w_in, q_norm, w_uq,
           kv_norm, w_ukv, mem_norm, w_mem_kv, w_fourier_out, w_attn_out, w_mem_out, w_o,
           post_mix_norm, pre_ffn_norm, w_ffn_gate, w_ffn_up, ffn_conv_w, ffn_conv_b,
           w_ffn_down, post_ffn_norm):
    p = dict(pre_mix_norm=pre_mix_norm, w_in=w_in, q_norm=q_norm, w_uq=w_uq,
             kv_norm=kv_norm, w_ukv=w_ukv, mem_norm=mem_norm, w_mem_kv=w_mem_kv,
             w_fourier_out=w_fourier_out, w_attn_out=w_attn_out, w_mem_out=w_mem_out,
             w_o=w_o, post_mix_norm=post_mix_norm, pre_ffn_norm=pre_ffn_norm,
             w_ffn_gate=w_ffn_gate, w_ffn_up=w_ffn_up, ffn_conv_w=ffn_conv_w,
             ffn_conv_b=ffn_conv_b, w_ffn_down=w_ffn_down, post_ffn_norm=post_ffn_norm)
    layers = [_pack_weights(p, l) for l in range(w_in.shape[0])]
    return (_trunk(x_prompt, mem_prompt, layers), _trunk(x_sample, mem_sample, layers))
```

```python
import functools
import math

import numpy as np
import jax
import jax.numpy as jnp
from jax import lax
from jax.experimental import pallas as pl
from jax.experimental.pallas import tpu as pltpu

D_MODEL = 2048
N_GROUPS = 4
GROUP_DIM = 256
FOURIER_DIM = N_GROUPS * GROUP_DIM
MLA_HEADS = 16
Q_LORA = 512
KV_LORA = 512
QK_NOPE = 128
QK_ROPE = 64
V_HEAD = 128
ROPE_THETA = 10000.0
MEM_HEADS = 4
MEM_HEAD_DIM = 256
MEM_DIM = MEM_HEADS * MEM_HEAD_DIM
D_FF = 5632
EPS = 1e-6

LANES = 128
QK_PAD = 256
KPE_PAD = LANES
CKV_EXT = KV_LORA + KPE_PAD
ONE_LANE = KV_LORA + QK_ROPE
V_ROWS = V_HEAD + 16
N_SLOTS = 3
FFT_N2 = 128
VMEM_LIMIT = 56 * 1024 * 1024
FF_CHUNK = 512
LOG2E = 1.4426950408889634

BF16 = jnp.bfloat16
F32 = jnp.float32


def _params(sem):
    return pltpu.CompilerParams(dimension_semantics=sem, vmem_limit_bytes=VMEM_LIMIT)


def _rms(x, g):
    ms = jnp.mean(x * x, axis=-1, keepdims=True)
    return x * lax.rsqrt(ms + EPS) * g


def _rope128(x, c, s1, s2):
    return x * c + pltpu.roll(x, 96, 1) * s1 + pltpu.roll(x, 32, 1) * s2


def _norm_kernel(x_ref, g_ref, o_ref):
    o_ref[...] = _rms(x_ref[...].astype(F32), g_ref[...]).astype(o_ref.dtype)


def _norm(x, g, tm):
    t, d = x.shape
    return pl.pallas_call(
        _norm_kernel,
        grid=(t // tm,),
        in_specs=[pl.BlockSpec((tm, d), lambda i: (i, 0)),
                  pl.BlockSpec((1, d), lambda i: (0, 0))],
        out_specs=pl.BlockSpec((tm, d), lambda i: (i, 0)),
        out_shape=jax.ShapeDtypeStruct((t, d), BF16),
        compiler_params=_params(("parallel",)),
        name="rmsnorm",
    )(x, g.reshape(1, d))


def _mm_kernel(a_ref, w_ref, o_ref):
    o_ref[...] = jnp.dot(a_ref[...], w_ref[...],
                         preferred_element_type=F32).astype(o_ref.dtype)


def _mm(a, w, tm, tn, name):
    m, k = a.shape
    n = w.shape[1]
    tm = min(tm, m)
    tn = min(tn, n)
    return pl.pallas_call(
        _mm_kernel,
        grid=(m // tm, n // tn),
        in_specs=[pl.BlockSpec((tm, k), lambda i, j: (i, 0)),
                  pl.BlockSpec((k, tn), lambda i, j: (0, j))],
        out_specs=pl.BlockSpec((tm, tn), lambda i, j: (i, j)),
        out_shape=jax.ShapeDtypeStruct((m, n), BF16),
        compiler_params=_params(("parallel", "arbitrary")),
        name=name,
    )(a, w)


def _mm_chunked(a, w, tm, tn, name):
    m, k = a.shape
    n = w.shape[1]
    tm = min(tm, m)
    return pl.pallas_call(
        _mm_kernel,
        grid=(m // tm, n // tn),
        in_specs=[pl.BlockSpec((tm, k), lambda i, j: (i, 0)),
                  pl.BlockSpec((k, tn), lambda i, j: (0, j))],
        out_specs=pl.BlockSpec((None, tm, tn), lambda i, j: (j, i, 0)),
        out_shape=jax.ShapeDtypeStruct((n // tn, m, tn), BF16),
        compiler_params=_params(("parallel", "arbitrary")),
        name=name,
    )(a, w)


def _latent_kernel(h_ref, w_ref, qg_ref, kvg_ref, c_ref, s1_ref, s2_ref, cq_ref, ckv_ref):
    z = jnp.dot(h_ref[...], w_ref[...], preferred_element_type=F32)
    cq_ref[...] = _rms(z[:, :Q_LORA], qg_ref[...]).astype(cq_ref.dtype)
    lat = _rms(z[:, Q_LORA:Q_LORA + KV_LORA], kvg_ref[...])
    ckv_ref[:, :KV_LORA] = lat.astype(ckv_ref.dtype)
    kpe = _rope128(z[:, Q_LORA + KV_LORA:], c_ref[...], s1_ref[...], s2_ref[...])
    lane = lax.broadcasted_iota(jnp.int32, kpe.shape, 1)
    kpe = jnp.where(lane == ONE_LANE - KV_LORA, 1.0, kpe)
    ckv_ref[:, KV_LORA:] = kpe.astype(ckv_ref.dtype)


def _latent(h, w, qg, kvg, tabs, seq, tm):
    t, d = h.shape
    n = w.shape[1]
    nblk = seq // tm
    tab = pl.BlockSpec((tm, LANES), lambda i: (i % nblk, 0))
    return pl.pallas_call(
        _latent_kernel,
        grid=(t // tm,),
        in_specs=[pl.BlockSpec((tm, d), lambda i: (i, 0)),
                  pl.BlockSpec((d, n), lambda i: (0, 0)),
                  pl.BlockSpec((1, Q_LORA), lambda i: (0, 0)),
                  pl.BlockSpec((1, KV_LORA), lambda i: (0, 0)),
                  tab, tab, tab],
        out_specs=[pl.BlockSpec((tm, Q_LORA), lambda i: (i, 0)),
                   pl.BlockSpec((tm, CKV_EXT), lambda i: (i, 0))],
        out_shape=[jax.ShapeDtypeStruct((t, Q_LORA), BF16),
                   jax.ShapeDtypeStruct((t, CKV_EXT), BF16)],
        compiler_params=_params(("parallel",)),
        name="latent_proj",
    )(h, w, qg.reshape(1, -1), kvg.reshape(1, -1), *tabs)


def _qup_kernel(cq_ref, w_ref, c_ref, s1_ref, s2_ref, q_ref, *, scale):
    z = jnp.dot(cq_ref[...], w_ref[...], preferred_element_type=F32) * scale
    c, s1, s2 = c_ref[...], s1_ref[...], s2_ref[...]
    for h in range(MLA_HEADS):
        lo = h * QK_PAD
        q_ref[:, lo:lo + QK_NOPE] = z[:, lo:lo + QK_NOPE].astype(q_ref.dtype)
        pe = _rope128(z[:, lo + QK_NOPE:lo + QK_PAD], c, s1, s2)
        q_ref[:, lo + QK_NOPE:lo + QK_PAD] = pe.astype(q_ref.dtype)


def _qup(cq, w, tabs, seq, tm, scale):
    t, k = cq.shape
    n = w.shape[1]
    nblk = seq // tm
    tab = pl.BlockSpec((tm, LANES), lambda i: (i % nblk, 0))
    return pl.pallas_call(
        functools.partial(_qup_kernel, scale=scale),
        grid=(t // tm,),
        in_specs=[pl.BlockSpec((tm, k), lambda i: (i, 0)),
                  pl.BlockSpec((k, n), lambda i: (0, 0)),
                  tab, tab, tab],
        out_specs=pl.BlockSpec((tm, n), lambda i: (i, 0)),
        out_shape=jax.ShapeDtypeStruct((t, n), BF16),
        compiler_params=_params(("parallel",)),
        name="q_up_rope",
    )(cq, w, *tabs)


def _vup_kernel(c_ref, w_ref, o_ref):
    o_ref[...] = lax.dot_general(w_ref[...], c_ref[...], (((1,), (1,)), ((), ())),
                                 preferred_element_type=F32).astype(o_ref.dtype)


def _vup(ckv, w_vt, ts):
    b, s, c = ckv.shape
    n = w_vt.shape[0]
    ts = min(ts, s)
    return pl.pallas_call(
        _vup_kernel,
        grid=(b, s // ts),
        in_specs=[pl.BlockSpec((None, ts, c), lambda bi, i: (bi, i, 0)),
                  pl.BlockSpec((n, c), lambda bi, i: (0, 0))],
        out_specs=pl.BlockSpec((None, n, ts), lambda bi, i: (bi, 0, i)),
        out_shape=jax.ShapeDtypeStruct((b, n, s), BF16),
        compiler_params=_params(("parallel", "arbitrary")),
        name="v_up_t",
    )(ckv, w_vt)


def _mla_kernel(q_ref, k_ref, vt_ref, o_ref, s_ref, p_ref, qt_ref, *, tq, tk):
    n_q = q_ref.shape[0] // tq
    n_k = k_ref.shape[0] // tk

    def q_tile(qi, carry):
        q0 = pl.multiple_of(qi * tq, tq)
        qt_ref[...] = q_ref[pl.ds(q0, tq), :].T

        def scores(c, slot):
            k = k_ref[pl.ds(pl.multiple_of(c * tk, tk), tk), :]
            s = jnp.dot(k, qt_ref[...], preferred_element_type=F32)
            s_ref[slot] = s
            return jnp.max(s, axis=0, keepdims=True)

        def softmax(slot, cmax, m):
            m_new = jnp.maximum(m, cmax)
            p_ref[slot] = jnp.exp2((s_ref[slot] - m_new).astype(BF16))
            return m_new, jnp.exp2(m - m_new)

        def pv(c, slot):
            vt = vt_ref[:, pl.ds(pl.multiple_of(c * tk, tk), tk)]
            return jnp.dot(vt, p_ref[slot], preferred_element_type=F32)

        def stage(c, u, m, acc, cmax):
            part = pv(c - 1, (u - 1) % N_SLOTS)
            cmax_next = scores(c + 1, (u + 1) % N_SLOTS)
            m, alpha = softmax(u % N_SLOTS, cmax, m)
            return m, (acc + part) * alpha, cmax_next

        cmax = scores(0, 0)
        m, _ = softmax(0, cmax, jnp.full((1, tq), -1e30, F32))
        cmax = scores(1, 1)
        acc = jnp.zeros((V_ROWS, tq), F32)

        n_mid = n_k - 2

        def group(j, st):
            m, acc, cmax = st
            for u in range(1, unroll + 1):
                m, acc, cmax = stage(unroll * j + u, u, m, acc, cmax)
            return m, acc, cmax

        if n_mid % (2 * N_SLOTS) == 0:
            unroll = 2 * N_SLOTS
        elif n_mid % N_SLOTS == 0:
            unroll = N_SLOTS
        else:
            unroll = n_mid
        if unroll == n_mid:
            m, acc, cmax = group(0, (m, acc, cmax))
        elif n_mid:
            m, acc, cmax = lax.fori_loop(0, n_mid // unroll, group, (m, acc, cmax))
        part = pv(n_k - 2, (n_k - 2) % N_SLOTS)
        m, alpha = softmax((n_k - 1) % N_SLOTS, cmax, m)
        acc = (acc + part) * alpha + pv(n_k - 1, (n_k - 1) % N_SLOTS)
        out = acc[:V_HEAD] * (1.0 / acc[V_HEAD:V_HEAD + 1])
        o_ref[pl.ds(q0, tq), :] = out.T.astype(o_ref.dtype)
        return carry

    lax.fori_loop(0, n_q, q_tile, 0)


def _mla(q, k, vt, tq_outer, tq, tk):
    b, s, _ = q.shape
    tq_outer = min(tq_outer, s)
    tq = min(tq, tq_outer)
    tk = min(tk, s // 2)
    assert (s // tk) % 2 == 0
    return pl.pallas_call(
        functools.partial(_mla_kernel, tq=tq, tk=tk),
        grid=(b, MLA_HEADS, s // tq_outer),
        in_specs=[pl.BlockSpec((None, tq_outer, QK_PAD), lambda bi, h, qi: (bi, qi, h)),
                  pl.BlockSpec((None, s, QK_PAD), lambda bi, h, qi: (bi, 0, h)),
                  pl.BlockSpec((None, V_ROWS, s), lambda bi, h, qi: (bi, h, 0))],
        out_specs=pl.BlockSpec((None, tq_outer, V_HEAD), lambda bi, h, qi: (bi, qi, h)),
        out_shape=jax.ShapeDtypeStruct((b, s, MLA_HEADS * V_HEAD), BF16),
        scratch_shapes=[pltpu.VMEM((N_SLOTS, tk, tq), F32),
                        pltpu.VMEM((N_SLOTS, tk, tq), BF16),
                        pltpu.VMEM((QK_PAD, tq), BF16)],
        compiler_params=_params(("parallel", "parallel", "arbitrary")),
        name="mla_attention",
    )(q, k, vt)


def _mem_kernel(q_ref, kv_ref, o_ref, *, scale):
    for h in range(MEM_HEADS):
        lo = h * MEM_HEAD_DIM
        q = q_ref[:, lo:lo + MEM_HEAD_DIM]
        k = kv_ref[:, lo:lo + MEM_HEAD_DIM]
        v = kv_ref[:, MEM_DIM + lo:MEM_DIM + lo + MEM_HEAD_DIM]
        s = lax.dot_general(q, k, (((1,), (1,)), ((), ())),
                            preferred_element_type=F32) * scale
        p = jnp.exp2(s - jnp.max(s, axis=-1, keepdims=True))
        l = jnp.sum(p, axis=-1, keepdims=True)
        o = jnp.dot(p.astype(BF16), v, preferred_element_type=F32)
        o_ref[:, lo:lo + MEM_HEAD_DIM] = (o * (1.0 / l)).astype(o_ref.dtype)


def _mem_attn(z1, kvm, q_blk, tq):
    b, s, _ = z1.shape
    n_mem = kvm.shape[1]
    tq = min(tq, s)
    scale = MEM_HEAD_DIM ** -0.5 * LOG2E
    return pl.pallas_call(
        functools.partial(_mem_kernel, scale=scale),
        grid=(b, s // tq),
        in_specs=[pl.BlockSpec((None, tq, MEM_DIM), lambda bi, i: (bi, i, q_blk)),
                  pl.BlockSpec((None, n_mem, 2 * MEM_DIM), lambda bi, i: (bi, 0, 0))],
        out_specs=pl.BlockSpec((None, tq, MEM_DIM), lambda bi, i: (bi, i, 0)),
        out_shape=jax.ShapeDtypeStruct((b, s, MEM_DIM), BF16),
        compiler_params=_params(("parallel", "arbitrary")),
        name="mem_attention",
    )(z1, kvm)


def _fft_a_kernel(u_ref, f_ref, tc_ref, ts_ref, o_ref, *, n1):
    f = f_ref[...]
    for j in range(u_ref.shape[0]):
        r = jnp.dot(f, u_ref[j], preferred_element_type=F32)
        br, bi = r[:n1], r[n1:]
        tc, ts = tc_ref[j], ts_ref[j]
        o_ref[j, 0] = (br * tc + bi * ts).astype(o_ref.dtype)
        o_ref[j, 1] = (bi * tc - br * ts).astype(o_ref.dtype)


def _fft_b_kernel(b_ref, m_ref, cs_ref, o_ref):
    m2 = m_ref[...]
    cs = cs_ref[...]
    n2 = o_ref.shape[1]
    for j in range(b_ref.shape[0]):
        x = b_ref[j].reshape(2 * n2, FOURIER_DIM)
        z = jnp.dot(m2, x, preferred_element_type=F32).astype(BF16)
        for g in range(N_GROUPS):
            lo = g * GROUP_DIM
            zz = jnp.concatenate([z[:n2, lo:lo + GROUP_DIM], z[n2:, lo:lo + GROUP_DIM]],
                                 axis=1)
            y = jnp.dot(zz, cs, preferred_element_type=F32)
            o_ref[j, :, lo:lo + GROUP_DIM] = y.astype(o_ref.dtype)


def _dft_tables(seq):
    n2 = FFT_N2
    n1 = seq // n2
    a1 = 2.0 * np.pi * np.outer(np.arange(n1), np.arange(n1)) / n1
    f1 = np.concatenate([np.cos(a1), -np.sin(a1)], axis=0) / math.sqrt(n1)
    th = 2.0 * np.pi * np.outer(np.arange(n2), np.arange(n1)) / seq
    a2 = 2.0 * np.pi * np.outer(np.arange(n2), np.arange(n2)) / n2
    c2, s2 = np.cos(a2), np.sin(a2)
    m2 = np.block([[c2, s2], [-s2, c2]]) / math.sqrt(n2)
    ac = 2.0 * np.pi * np.outer(np.arange(GROUP_DIM), np.arange(GROUP_DIM)) / GROUP_DIM
    cs = np.concatenate([np.cos(ac), np.sin(ac)], axis=0) / math.sqrt(GROUP_DIM)
    return (jnp.asarray(f1, BF16), jnp.asarray(np.cos(th)[:, :, None], F32),
            jnp.asarray(np.sin(th)[:, :, None], F32), jnp.asarray(m2, BF16),
            jnp.asarray(cs, BF16))


def _fourier(u, t2, tk1):
    b, s, c = u.shape
    n2 = FFT_N2
    n1 = s // n2
    t2 = min(t2, n2)
    tk1 = min(tk1, n1)
    f1, tc, ts, m2, cs = _dft_tables(s)
    ut = u.reshape(b, n1, n2, c).transpose(0, 2, 1, 3)
    bt = pl.pallas_call(
        functools.partial(_fft_a_kernel, n1=n1),
        grid=(b, n2 // t2),
        in_specs=[pl.BlockSpec((None, t2, n1, c), lambda bi, i: (bi, i, 0, 0)),
                  pl.BlockSpec((2 * n1, n1), lambda bi, i: (0, 0)),
                  pl.BlockSpec((t2, n1, 1), lambda bi, i: (i, 0, 0)),
                  pl.BlockSpec((t2, n1, 1), lambda bi, i: (i, 0, 0))],
        out_specs=pl.BlockSpec((None, t2, 2, n1, c), lambda bi, i: (bi, i, 0, 0, 0)),
        out_shape=jax.ShapeDtypeStruct((b, n2, 2, n1, c), BF16),
        compiler_params=_params(("parallel", "arbitrary")),
        name="fft_stage_a",
    )(ut, f1, tc, ts)
    bk = bt.transpose(0, 3, 2, 1, 4)
    y = pl.pallas_call(
        _fft_b_kernel,
        grid=(b, n1 // tk1),
        in_specs=[pl.BlockSpec((None, tk1, 2, n2, c), lambda bi, i: (bi, i, 0, 0, 0)),
                  pl.BlockSpec((2 * n2, 2 * n2), lambda bi, i: (0, 0)),
                  pl.BlockSpec((2 * GROUP_DIM, GROUP_DIM), lambda bi, i: (0, 0))],
        out_specs=pl.BlockSpec((None, tk1, n2, c), lambda bi, i: (bi, i, 0, 0)),
        out_shape=jax.ShapeDtypeStruct((b, n1, n2, c), BF16),
        compiler_params=_params(("parallel", "arbitrary")),
        name="fft_stage_b",
    )(bk, m2, cs)
    return y.transpose(0, 2, 1, 3).reshape(b, s, c)


def _merge_kernel(yf_ref, oa_ref, om_ref, g0_ref, g1_ref, g2_ref,
                  wf_ref, wa_ref, wm_ref, o_ref):
    def gate(g_ref):
        return jax.nn.sigmoid(g_ref[...].astype(F32))
    yf = jnp.dot(yf_ref[...], wf_ref[...], preferred_element_type=F32)
    ya = jnp.dot(oa_ref[...], wa_ref[...], preferred_element_type=F32)
    ym = jnp.dot(om_ref[...], wm_ref[...], preferred_element_type=F32)
    o_ref[...] = (gate(g0_ref) * yf + gate(g1_ref) * ya + gate(g2_ref) * ym
                  ).astype(o_ref.dtype)


def _merge(yf, oa, om, z1, wf, wa, wm, tm, tn):
    t = yf.shape[0]
    tm = min(tm, t)
    nb = D_MODEL // tn
    def gspec(k):
        return pl.BlockSpec((tm, tn), lambda i, j: (i, k * nb + j))
    def aspec(w):
        return pl.BlockSpec((tm, w), lambda i, j: (i, 0))
    def wspec(w):
        return pl.BlockSpec((w, tn), lambda i, j: (0, j))
    return pl.pallas_call(
        _merge_kernel,
        grid=(t // tm, nb),
        in_specs=[aspec(FOURIER_DIM), aspec(MLA_HEADS * V_HEAD), aspec(MEM_DIM),
                  gspec(0), gspec(1), gspec(2),
                  wspec(FOURIER_DIM), wspec(MLA_HEADS * V_HEAD), wspec(MEM_DIM)],
        out_specs=pl.BlockSpec((tm, tn), lambda i, j: (i, j)),
        out_shape=jax.ShapeDtypeStruct((t, D_MODEL), BF16),
        compiler_params=_params(("parallel", "arbitrary")),
        name="branch_merge",
    )(yf, oa, om, z1, z1, z1, wf, wa, wm)


def _oproj_kernel(a_ref, w_ref, x_ref, gpost_ref, gpre_ref, x1_ref, h_ref):
    y = jnp.dot(a_ref[...], w_ref[...], preferred_element_type=F32)
    x1 = x_ref[...] + _rms(y, gpost_ref[...])
    x1_ref[...] = x1
    h_ref[...] = _rms(x1, gpre_ref[...]).astype(h_ref.dtype)


def _oproj(a, w, x, gpost, gpre, tm):
    t, d = x.shape
    tm = min(tm, t)
    row = lambda i: (i, 0)
    fix = lambda i: (0, 0)
    return pl.pallas_call(
        _oproj_kernel,
        grid=(t // tm,),
        in_specs=[pl.BlockSpec((tm, d), row), pl.BlockSpec((d, d), fix),
                  pl.BlockSpec((tm, d), row), pl.BlockSpec((1, d), fix),
                  pl.BlockSpec((1, d), fix)],
        out_specs=[pl.BlockSpec((tm, d), row), pl.BlockSpec((tm, d), row)],
        out_shape=[jax.ShapeDtypeStruct((t, d), F32), jax.ShapeDtypeStruct((t, d), BF16)],
        compiler_params=_params(("parallel",)),
        name="out_proj_norm",
    )(a, w, x, gpost.reshape(1, d), gpre.reshape(1, d))


def _gelu_tanh(x):
    return 0.5 * x * (1.0 + jnp.tanh(0.7978845608028654 * (x + 0.044715 * x * x * x)))


def _down_kernel(gu_ref, gp_ref, gn_ref, cw_ref, cb_ref, w_ref, x_ref,
                 gpost_ref, gpre_ref, *refs, tiles_per_seq, halo, emit_h):
    act_ref, acc_ref = refs[-2:]
    nk = w_ref.shape[0]
    tm = x_ref.shape[0]
    pos = pl.program_id(0) % tiles_per_seq
    first = pos == 0
    last = pos == tiles_per_seq - 1

    def act(k, slot):
        g = gu_ref[k].astype(F32)
        prev_row = jnp.where(first, 0.0, gp_ref[k, halo - 1:halo, :].astype(F32))
        next_row = jnp.where(last, 0.0, gn_ref[k, 0:1, :].astype(F32))
        rows = lax.broadcasted_iota(jnp.int32, g.shape, 0)
        g_prev = jnp.where(rows == 0, prev_row, pltpu.roll(g, 1, 0))
        g_next = jnp.where(rows == tm - 1, next_row, pltpu.roll(g, tm - 1, 0))
        cw = cw_ref[k]
        conv = g_prev * cw[0:1] + g * cw[1:2] + g_next * cw[2:3] + cb_ref[k]
        act_ref[slot] = (_gelu_tanh(conv) * gu_ref[nk + k].astype(F32)).astype(BF16)

    def down(k, slot):
        acc_ref[...] += jnp.dot(act_ref[slot], w_ref[k], preferred_element_type=F32)

    acc_ref[...] = jnp.zeros_like(acc_ref)
    act(0, 0)

    def pair(j, carry):
        k = 2 * j
        act(k + 1, 1)
        down(k, 0)
        act(k + 2, 0)
        down(k + 1, 1)
        return carry

    lax.fori_loop(0, nk // 2, pair, 0)
    down(nk - 1, 0)
    x2 = x_ref[...] + _rms(acc_ref[...], gpost_ref[...])
    refs[0][...] = x2
    if emit_h:
        refs[1][...] = _rms(x2, gpre_ref[...]).astype(refs[1].dtype)


def _down(gu, cw, cb, w, x, gpost, gpre, seq, tm, emit_h):
    t, d = x.shape
    nk, tk, _ = w.shape
    assert nk % 2 == 1
    tm = min(tm, seq)
    halo = 16
    tiles_per_seq = seq // tm
    hb = tm // halo
    last_hb = t // halo - 1
    row = lambda i: (i, 0)
    fix = lambda i: (0, 0)
    fix3 = lambda i: (0, 0, 0)
    out_specs = [pl.BlockSpec((tm, d), row)]
    out_shape = [jax.ShapeDtypeStruct((t, d), F32)]
    if emit_h:
        out_specs.append(pl.BlockSpec((tm, d), row))
        out_shape.append(jax.ShapeDtypeStruct((t, d), BF16))
    outs = pl.pallas_call(
        functools.partial(_down_kernel, tiles_per_seq=tiles_per_seq, halo=halo,
                          emit_h=emit_h),
        grid=(t // tm,),
        in_specs=[pl.BlockSpec((2 * nk, tm, tk), lambda i: (0, i, 0)),
                  pl.BlockSpec((nk, halo, tk), lambda i: (0, jnp.maximum(i * hb - 1, 0), 0)),
                  pl.BlockSpec((nk, halo, tk),
                               lambda i: (0, jnp.minimum((i + 1) * hb, last_hb), 0)),
                  pl.BlockSpec((nk, 3, tk), fix3),
                  pl.BlockSpec((nk, 1, tk), fix3),
                  pl.BlockSpec((nk, tk, d), fix3, pipeline_mode=pl.Buffered(1)),
                  pl.BlockSpec((tm, d), row),
                  pl.BlockSpec((1, d), fix), pl.BlockSpec((1, d), fix)],
        out_specs=out_specs,
        out_shape=out_shape,
        scratch_shapes=[pltpu.VMEM((2, tm, tk), BF16), pltpu.VMEM((tm, d), F32)],
        compiler_params=_params(("parallel",)),
        name="ffn_down_norm",
    )(gu, gu, gu, cw, cb, w, x, gpost.reshape(1, d), gpre.reshape(1, d))
    return outs if emit_h else (outs[0], None)


def _pack_weights(p, l):
    s0 = FOURIER_DIM
    s1 = s0 + Q_LORA
    s2 = s1 + KV_LORA
    s3 = s2 + QK_ROPE
    s4 = s3 + MEM_DIM
    w_in = p["w_in"][l]
    d = w_in.shape[0]
    bf = lambda a: a.astype(BF16)
    w_gq = bf(jnp.concatenate([w_in[:, s4:], w_in[:, s3:s4]], axis=1))
    w_f = bf(w_in[:, :s0])
    w_lat = bf(jnp.concatenate([w_in[:, s0:s3], jnp.zeros((d, KPE_PAD - QK_ROPE), F32)],
                               axis=1))
    wq = p["w_uq"][l].reshape(Q_LORA, MLA_HEADS, QK_NOPE + QK_ROPE)
    wq = jnp.pad(wq, ((0, 0), (0, 0), (0, QK_PAD - QK_NOPE - QK_ROPE)))
    wq = bf(wq.reshape(Q_LORA, MLA_HEADS * QK_PAD))
    wkv = p["w_ukv"][l].reshape(KV_LORA, MLA_HEADS, QK_NOPE + V_HEAD)
    wk = jnp.pad(wkv[:, :, :QK_NOPE], ((0, KPE_PAD), (0, 0), (0, QK_PAD - QK_NOPE)))
    eye = jnp.eye(KPE_PAD, QK_PAD - QK_NOPE, dtype=F32)[:, None, :]
    eye = eye * (jnp.arange(KPE_PAD) < QK_ROPE)[:, None, None]
    wk = wk.at[KV_LORA:, :, QK_NOPE:].set(jnp.broadcast_to(
        eye, (KPE_PAD, MLA_HEADS, QK_PAD - QK_NOPE)))
    w_k = bf(wk.reshape(CKV_EXT, -1))
    wvt = jnp.pad(wkv[:, :, QK_NOPE:].transpose(1, 2, 0),
                  ((0, 0), (0, V_ROWS - V_HEAD), (0, KPE_PAD)))
    wvt = wvt.at[:, V_HEAD, ONE_LANE].set(1.0)
    w_vt = bf(wvt.reshape(MLA_HEADS * V_ROWS, CKV_EXT))
    w_gu = bf(jnp.concatenate([p["w_ffn_gate"][l], p["w_ffn_up"][l]], axis=1))
    return dict(
        w_gq=w_gq, w_f=w_f, w_lat=w_lat, wq=wq, w_k=w_k, w_vt=w_vt, w_gu=w_gu,
        w_mem_kv=bf(p["w_mem_kv"][l]), w_fo=bf(p["w_fourier_out"][l]),
        w_ao=bf(p["w_attn_out"][l]), w_mo=bf(p["w_mem_out"][l]), w_o=bf(p["w_o"][l]),
        w_down=bf(p["w_ffn_down"][l]).reshape(D_FF // FF_CHUNK, FF_CHUNK, -1),
        conv_w=p["ffn_conv_w"][l].reshape(-1, D_FF // FF_CHUNK, FF_CHUNK).transpose(1, 0, 2),
        conv_b=p["ffn_conv_b"][l].reshape(D_FF // FF_CHUNK, 1, FF_CHUNK),
        q_norm=p["q_norm"][l], kv_norm=p["kv_norm"][l], mem_norm=p["mem_norm"][l],
        pre_mix=p["pre_mix_norm"][l], post_mix=p["post_mix_norm"][l],
        pre_ffn=p["pre_ffn_norm"][l], post_ffn=p["post_ffn_norm"][l])


def _rope_tables(seq):
    inv_freq = 1.0 / (ROPE_THETA ** (jnp.arange(0, QK_ROPE, 2, dtype=F32) / QK_ROPE))
    ang = jnp.arange(seq, dtype=F32)[:, None] * inv_freq[None, :]
    cos, sin = jnp.cos(ang), jnp.sin(ang)
    half = QK_ROPE // 2
    zh = jnp.zeros((seq, half), F32)
    zr = jnp.zeros((seq, LANES - QK_ROPE), F32)
    c = jnp.concatenate([cos, cos, zr], axis=1)
    s1 = jnp.concatenate([-sin, zh, zr], axis=1)
    s2 = jnp.concatenate([zh, sin, zr], axis=1)
    return c, s1, s2


def _trunk(x, mem, layers):
    b, s, d = x.shape
    t = b * s
    n_mem = mem.shape[1]
    tm = min(512, s)
    tabs = _rope_tables(s)
    q_scale = (QK_NOPE + QK_ROPE) ** -0.5 * LOG2E
    gate_w = 3 * D_MODEL
    xf = x.reshape(t, d)
    memf = mem.reshape(b * n_mem, d)
    h = _norm(xf, layers[0]["pre_mix"], tm)
    for l, w in enumerate(layers):
        z1 = _mm(h, w["w_gq"], 1024, 1024, "in_proj_gates")
        f_in = _mm(h, w["w_f"], 1024, 1024, "in_proj_fourier")
        cq, ckv = _latent(h, w["w_lat"], w["q_norm"], w["kv_norm"], tabs, s, tm)
        q = _qup(cq, w["wq"], tabs, s, tm, q_scale)
        kc = _mm(ckv, w["w_k"], 1024, 1024, "k_up")
        vt = _vup(ckv.reshape(b, s, -1), w["w_vt"], 512)
        oa = _mla(q.reshape(b, s, -1), kc.reshape(b, s, -1), vt, 2048, 512, 512)
        mem_n = _norm(memf, w["mem_norm"], n_mem)
        kvm = _mm(mem_n, w["w_mem_kv"], 1024, 1024, "mem_kv")
        om = _mem_attn(z1.reshape(b, s, -1), kvm.reshape(b, n_mem, -1),
                       gate_w // MEM_DIM, 1024)
        yf = _fourier(f_in.reshape(b, s, -1), 16, 16)
        merged = _merge(yf.reshape(t, -1), oa.reshape(t, -1), om.reshape(t, -1), z1,
                        w["w_fo"], w["w_ao"], w["w_mo"], 1024, 512)
        x1, h2 = _oproj(merged, w["w_o"], xf, w["post_mix"], w["pre_ffn"], 256)
        gu = _mm_chunked(h2, w["w_gu"], 1024, FF_CHUNK, "ffn_gate_up")
        last = l == len(layers) - 1
        nxt = layers[0 if last else l + 1]["pre_mix"]
        xf, h = _down(gu, w["conv_w"], w["conv_b"], w["w_down"], x1, w["post_ffn"], nxt,
                      s, 256, not last)
    return xf.reshape(b, s, d)


def kernel(x_prompt, x_sample, mem_prompt, mem_sample, pre_mix_norm, w_in, q_norm, w_uq,
           kv_norm, w_ukv, mem_norm, w_mem_kv, w_fourier_out, w_attn_out, w_mem_out, w_o,
           post_mix_norm, pre_ffn_norm, w_ffn_gate, w_ffn_up, ffn_conv_w, ffn_conv_b,
           w_ffn_down, post_ffn_norm):
    p = dict(pre_mix_norm=pre_mix_norm, w_in=w_in, q_norm=q_norm, w_uq=w_uq,
             kv_norm=kv_norm, w_ukv=w_ukv, mem_norm=mem_norm, w_mem_kv=w_mem_kv,
             w_fourier_out=w_fourier_out, w_attn_out=w_attn_out, w_mem_out=w_mem_out,
             w_o=w_o, post_mix_norm=post_mix_norm, pre_ffn_norm=pre_ffn_norm,
             w_ffn_gate=w_ffn_gate, w_ffn_up=w_ffn_up, ffn_conv_w=ffn_conv_w,
             ffn_conv_b=ffn_conv_b, w_ffn_down=w_ffn_down, post_ffn_norm=post_ffn_norm)
    layers = [_pack_weights(p, l) for l in range(w_in.shape[0])]
    return (_trunk(x_prompt, mem_prompt, layers), _trunk(x_sample, mem_sample, layers))
```

```python
import functools
import math

import numpy as np
import jax
import jax.numpy as jnp
from jax import lax
from jax.experimental import pallas as pl
from jax.experimental.pallas import tpu as pltpu

D_MODEL = 2048
N_GROUPS = 4
GROUP_DIM = 256
FOURIER_DIM = N_GROUPS * GROUP_DIM
MLA_HEADS = 16
Q_LORA = 512
KV_LORA = 512
QK_NOPE = 128
QK_ROPE = 64
V_HEAD = 128
ROPE_THETA = 10000.0
MEM_HEADS = 4
MEM_HEAD_DIM = 256
MEM_DIM = MEM_HEADS * MEM_HEAD_DIM
D_FF = 5632
EPS = 1e-6

LANES = 128
QK_PAD = 256
KPE_PAD = LANES
CKV_EXT = KV_LORA + KPE_PAD
ONE_LANE = KV_LORA + QK_ROPE
V_ROWS = V_HEAD + 16
N_SLOTS = 3
FFT_N2 = 128
VMEM_LIMIT = 56 * 1024 * 1024
FF_CHUNK = 512
LOG2E = 1.4426950408889634

BF16 = jnp.bfloat16
F32 = jnp.float32


def _params(sem):
    return pltpu.CompilerParams(dimension_semantics=sem, vmem_limit_bytes=VMEM_LIMIT)


def _rms(x, g):
    ms = jnp.mean(x * x, axis=-1, keepdims=True)
    return x * lax.rsqrt(ms + EPS) * g


def _rope128(x, c, s1, s2):
    return x * c + pltpu.roll(x, 96, 1) * s1 + pltpu.roll(x, 32, 1) * s2


def _norm_kernel(x_ref, g_ref, o_ref):
    o_ref[...] = _rms(x_ref[...].astype(F32), g_ref[...]).astype(o_ref.dtype)


def _norm(x, g, tm):
    t, d = x.shape
    return pl.pallas_call(
        _norm_kernel,
        grid=(t // tm,),
        in_specs=[pl.BlockSpec((tm, d), lambda i: (i, 0)),
                  pl.BlockSpec((1, d), lambda i: (0, 0))],
        out_specs=pl.BlockSpec((tm, d), lambda i: (i, 0)),
        out_shape=jax.ShapeDtypeStruct((t, d), BF16),
        compiler_params=_params(("parallel",)),
        name="rmsnorm",
    )(x, g.reshape(1, d))


def _mm_kernel(a_ref, w_ref, o_ref):
    o_ref[...] = jnp.dot(a_ref[...], w_ref[...],
                         preferred_element_type=F32).astype(o_ref.dtype)


def _mm(a, w, tm, tn, name):
    m, k = a.shape
    n = w.shape[1]
    tm = min(tm, m)
    tn = min(tn, n)
    return pl.pallas_call(
        _mm_kernel,
        grid=(m // tm, n // tn),
        in_specs=[pl.BlockSpec((tm, k), lambda i, j: (i, 0)),
                  pl.BlockSpec((k, tn), lambda i, j: (0, j))],
        out_specs=pl.BlockSpec((tm, tn), lambda i, j: (i, j)),
        out_shape=jax.ShapeDtypeStruct((m, n), BF16),
        compiler_params=_params(("parallel", "arbitrary")),
        name=name,
    )(a, w)


def _mm_chunked_kernel(a_ref, w_ref, o_ref):
    r = jnp.dot(a_ref[...], w_ref[...], preferred_element_type=F32).astype(o_ref.dtype)
    tc = o_ref.shape[2]
    for c in range(o_ref.shape[0]):
        o_ref[c] = r[:, c * tc:(c + 1) * tc]


def _mm_chunked(a, w, tm, tn, tc, name):
    m, k = a.shape
    n = w.shape[1]
    tm = min(tm, m)
    return pl.pallas_call(
        _mm_chunked_kernel,
        grid=(m // tm, n // tn),
        in_specs=[pl.BlockSpec((tm, k), lambda i, j: (i, 0)),
                  pl.BlockSpec((k, tn), lambda i, j: (0, j))],
        out_specs=pl.BlockSpec((tn // tc, tm, tc), lambda i, j: (j, i, 0)),
        out_shape=jax.ShapeDtypeStruct((n // tc, m, tc), BF16),
        compiler_params=_params(("parallel", "arbitrary")),
        name=name,
    )(a, w)


def _latent_kernel(h_ref, w_ref, qg_ref, kvg_ref, c_ref, s1_ref, s2_ref, cq_ref, ckv_ref):
    z = jnp.dot(h_ref[...], w_ref[...], preferred_element_type=F32)
    cq_ref[...] = _rms(z[:, :Q_LORA], qg_ref[...]).astype(cq_ref.dtype)
    lat = _rms(z[:, Q_LORA:Q_LORA + KV_LORA], kvg_ref[...])
    ckv_ref[:, :KV_LORA] = lat.astype(ckv_ref.dtype)
    kpe = _rope128(z[:, Q_LORA + KV_LORA:], c_ref[...], s1_ref[...], s2_ref[...])
    lane = lax.broadcasted_iota(jnp.int32, kpe.shape, 1)
    kpe = jnp.where(lane == ONE_LANE - KV_LORA, 1.0, kpe)
    ckv_ref[:, KV_LORA:] = kpe.astype(ckv_ref.dtype)


def _latent(h, w, qg, kvg, tabs, seq, tm):
    t, d = h.shape
    n = w.shape[1]
    nblk = seq // tm
    tab = pl.BlockSpec((tm, LANES), lambda i: (i % nblk, 0))
    return pl.pallas_call(
        _latent_kernel,
        grid=(t // tm,),
        in_specs=[pl.BlockSpec((tm, d), lambda i: (i, 0)),
                  pl.BlockSpec((d, n), lambda i: (0, 0)),
                  pl.BlockSpec((1, Q_LORA), lambda i: (0, 0)),
                  pl.BlockSpec((1, KV_LORA), lambda i: (0, 0)),
                  tab, tab, tab],
        out_specs=[pl.BlockSpec((tm, Q_LORA), lambda i: (i, 0)),
                   pl.BlockSpec((tm, CKV_EXT), lambda i: (i, 0))],
        out_shape=[jax.ShapeDtypeStruct((t, Q_LORA), BF16),
                   jax.ShapeDtypeStruct((t, CKV_EXT), BF16)],
        compiler_params=_params(("parallel",)),
        name="latent_proj",
    )(h, w, qg.reshape(1, -1), kvg.reshape(1, -1), *tabs)


def _qup_kernel(cq_ref, w_ref, c_ref, s1_ref, s2_ref, q_ref, *, scale):
    z = jnp.dot(cq_ref[...], w_ref[...], preferred_element_type=F32) * scale
    c, s1, s2 = c_ref[...], s1_ref[...], s2_ref[...]
    for h in range(MLA_HEADS):
        lo = h * QK_PAD
        q_ref[:, lo:lo + QK_NOPE] = z[:, lo:lo + QK_NOPE].astype(q_ref.dtype)
        pe = _rope128(z[:, lo + QK_NOPE:lo + QK_PAD], c, s1, s2)
        q_ref[:, lo + QK_NOPE:lo + QK_PAD] = pe.astype(q_ref.dtype)


def _qup(cq, w, tabs, seq, tm, scale):
    t, k = cq.shape
    n = w.shape[1]
    nblk = seq // tm
    tab = pl.BlockSpec((tm, LANES), lambda i: (i % nblk, 0))
    return pl.pallas_call(
        functools.partial(_qup_kernel, scale=scale),
        grid=(t // tm,),
        in_specs=[pl.BlockSpec((tm, k), lambda i: (i, 0)),
                  pl.BlockSpec((k, n), lambda i: (0, 0)),
                  tab, tab, tab],
        out_specs=pl.BlockSpec((tm, n), lambda i: (i, 0)),
        out_shape=jax.ShapeDtypeStruct((t, n), BF16),
        compiler_params=_params(("parallel",)),
        name="q_up_rope",
    )(cq, w, *tabs)


def _vup_kernel(c_ref, w_ref, o_ref):
    o_ref[...] = lax.dot_general(w_ref[...], c_ref[...], (((1,), (1,)), ((), ())),
                                 preferred_element_type=F32).astype(o_ref.dtype)


def _vup(ckv, w_vt, ts):
    b, s, c = ckv.shape
    n = w_vt.shape[0]
    ts = min(ts, s)
    return pl.pallas_call(
        _vup_kernel,
        grid=(b, s // ts),
        in_specs=[pl.BlockSpec((None, ts, c), lambda bi, i: (bi, i, 0)),
                  pl.BlockSpec((n, c), lambda bi, i: (0, 0))],
        out_specs=pl.BlockSpec((None, n, ts), lambda bi, i: (bi, 0, i)),
        out_shape=jax.ShapeDtypeStruct((b, n, s), BF16),
        compiler_params=_params(("parallel", "arbitrary")),
        name="v_up_t",
    )(ckv, w_vt)


def _mla_kernel(q_ref, k_ref, vt_ref, o_ref, s_ref, p_ref, qt_ref, *, tq, tk):
    n_q = q_ref.shape[0] // tq
    n_k = k_ref.shape[0] // tk

    def q_tile(qi, carry):
        q0 = pl.multiple_of(qi * tq, tq)
        qt_ref[...] = q_ref[pl.ds(q0, tq), :].T

        def scores(c, slot):
            k = k_ref[pl.ds(pl.multiple_of(c * tk, tk), tk), :]
            s = jnp.dot(k, qt_ref[...], preferred_element_type=F32)
            s_ref[slot] = s
            return jnp.max(s, axis=0, keepdims=True)

        def softmax(slot, cmax, m):
            m_new = jnp.maximum(m, cmax)
            p_ref[slot] = jnp.exp2((s_ref[slot] - m_new).astype(BF16))
            return m_new, jnp.exp2(m - m_new)

        def pv(c, slot):
            vt = vt_ref[:, pl.ds(pl.multiple_of(c * tk, tk), tk)]
            return jnp.dot(vt, p_ref[slot], preferred_element_type=F32)

        def stage(c, u, m, acc, cmax):
            part = pv(c - 1, (u - 1) % N_SLOTS)
            cmax_next = scores(c + 1, (u + 1) % N_SLOTS)
            m, alpha = softmax(u % N_SLOTS, cmax, m)
            return m, (acc + part) * alpha, cmax_next

        cmax = scores(0, 0)
        m, _ = softmax(0, cmax, jnp.full((1, tq), -1e30, F32))
        cmax = scores(1, 1)
        acc = jnp.zeros((V_ROWS, tq), F32)

        n_mid = n_k - 2
        unroll = 4 * N_SLOTS
        peel = n_mid % unroll
        st = (m, acc, cmax)
        for u in range(1, peel + 1):
            st = stage(u, u, *st)

        def group(j, st):
            for u in range(1, unroll + 1):
                st = stage(peel + unroll * j + u, peel + u, *st)
            return st

        trips = n_mid // unroll
        if trips == 1:
            st = group(0, st)
        elif trips > 1:
            st = lax.fori_loop(0, trips, group, st)
        m, acc, cmax = st
        part = pv(n_k - 2, (n_k - 2) % N_SLOTS)
        m, alpha = softmax((n_k - 1) % N_SLOTS, cmax, m)
        acc = (acc + part) * alpha + pv(n_k - 1, (n_k - 1) % N_SLOTS)
        out = acc[:V_HEAD] * (1.0 / acc[V_HEAD:V_HEAD + 1])
        o_ref[pl.ds(q0, tq), :] = out.T.astype(o_ref.dtype)
        return carry

    lax.fori_loop(0, n_q, q_tile, 0)


def _mla(q, k, vt, tq_outer, tq, tk):
    b, s, _ = q.shape
    tq_outer = min(tq_outer, s)
    tq = min(tq, tq_outer)
    tk = min(tk, s // 2)
    assert (s // tk) % 2 == 0
    return pl.pallas_call(
        functools.partial(_mla_kernel, tq=tq, tk=tk),
        grid=(b, MLA_HEADS, s // tq_outer),
        in_specs=[pl.BlockSpec((None, tq_outer, QK_PAD), lambda bi, h, qi: (bi, qi, h)),
                  pl.BlockSpec((None, s, QK_PAD), lambda bi, h, qi: (bi, 0, h)),
                  pl.BlockSpec((None, V_ROWS, s), lambda bi, h, qi: (bi, h, 0))],
        out_specs=pl.BlockSpec((None, tq_outer, V_HEAD), lambda bi, h, qi: (bi, qi, h)),
        out_shape=jax.ShapeDtypeStruct((b, s, MLA_HEADS * V_HEAD), BF16),
        scratch_shapes=[pltpu.VMEM((N_SLOTS, tk, tq), F32),
                        pltpu.VMEM((N_SLOTS, tk, tq), BF16),
                        pltpu.VMEM((QK_PAD, tq), BF16)],
        compiler_params=_params(("parallel", "parallel", "arbitrary")),
        name="mla_attention",
    )(q, k, vt)


def _mem_kernel(q_ref, kv_ref, o_ref, *, scale):
    for h in range(MEM_HEADS):
        lo = h * MEM_HEAD_DIM
        q = q_ref[:, lo:lo + MEM_HEAD_DIM]
        k = kv_ref[:, lo:lo + MEM_HEAD_DIM]
        v = kv_ref[:, MEM_DIM + lo:MEM_DIM + lo + MEM_HEAD_DIM]
        s = lax.dot_general(q, k, (((1,), (1,)), ((), ())),
                            preferred_element_type=F32) * scale
        p = jnp.exp2(s - jnp.max(s, axis=-1, keepdims=True))
        l = jnp.sum(p, axis=-1, keepdims=True)
        o = jnp.dot(p.astype(BF16), v, preferred_element_type=F32)
        o_ref[:, lo:lo + MEM_HEAD_DIM] = (o * (1.0 / l)).astype(o_ref.dtype)


def _mem_attn(z1, kvm, q_blk, tq):
    b, s, _ = z1.shape
    n_mem = kvm.shape[1]
    tq = min(tq, s)
    scale = MEM_HEAD_DIM ** -0.5 * LOG2E
    return pl.pallas_call(
        functools.partial(_mem_kernel, scale=scale),
        grid=(b, s // tq),
        in_specs=[pl.BlockSpec((None, tq, MEM_DIM), lambda bi, i: (bi, i, q_blk)),
                  pl.BlockSpec((None, n_mem, 2 * MEM_DIM), lambda bi, i: (bi, 0, 0))],
        out_specs=pl.BlockSpec((None, tq, MEM_DIM), lambda bi, i: (bi, i, 0)),
        out_shape=jax.ShapeDtypeStruct((b, s, MEM_DIM), BF16),
        compiler_params=_params(("parallel", "arbitrary")),
        name="mem_attention",
    )(z1, kvm)


def _fft_a_kernel(u_ref, f_ref, tc_ref, ts_ref, o_ref, *, n1):
    f = f_ref[...]
    for j in range(u_ref.shape[0]):
        r = jnp.dot(f, u_ref[j], preferred_element_type=F32)
        br, bi = r[:n1], r[n1:]
        tc, ts = tc_ref[j], ts_ref[j]
        o_ref[j, 0] = (br * tc + bi * ts).astype(o_ref.dtype)
        o_ref[j, 1] = (bi * tc - br * ts).astype(o_ref.dtype)


def _fft_b_kernel(b_ref, m_ref, cs_ref, o_ref):
    m2 = m_ref[...]
    cs = cs_ref[...]
    n2 = o_ref.shape[1]
    for j in range(b_ref.shape[0]):
        x = b_ref[j].reshape(2 * n2, FOURIER_DIM)
        z = jnp.dot(m2, x, preferred_element_type=F32).astype(BF16)
        for g in range(N_GROUPS):
            lo = g * GROUP_DIM
            zz = jnp.concatenate([z[:n2, lo:lo + GROUP_DIM], z[n2:, lo:lo + GROUP_DIM]],
                                 axis=1)
            y = jnp.dot(zz, cs, preferred_element_type=F32)
            o_ref[j, :, lo:lo + GROUP_DIM] = y.astype(o_ref.dtype)


def _dft_tables(seq):
    n2 = FFT_N2
    n1 = seq // n2
    a1 = 2.0 * np.pi * np.outer(np.arange(n1), np.arange(n1)) / n1
    f1 = np.concatenate([np.cos(a1), -np.sin(a1)], axis=0) / math.sqrt(n1)
    th = 2.0 * np.pi * np.outer(np.arange(n2), np.arange(n1)) / seq
    a2 = 2.0 * np.pi * np.outer(np.arange(n2), np.arange(n2)) / n2
    c2, s2 = np.cos(a2), np.sin(a2)
    m2 = np.block([[c2, s2], [-s2, c2]]) / math.sqrt(n2)
    ac = 2.0 * np.pi * np.outer(np.arange(GROUP_DIM), np.arange(GROUP_DIM)) / GROUP_DIM
    cs = np.concatenate([np.cos(ac), np.sin(ac)], axis=0) / math.sqrt(GROUP_DIM)
    return (jnp.asarray(f1, BF16), jnp.asarray(np.cos(th)[:, :, None], F32),
            jnp.asarray(np.sin(th)[:, :, None], F32), jnp.asarray(m2, BF16),
            jnp.asarray(cs, BF16))


def _fourier(u, t2, tk1):
    b, s, c = u.shape
    n2 = FFT_N2
    n1 = s // n2
    t2 = min(t2, n2)
    tk1 = min(tk1, n1)
    f1, tc, ts, m2, cs = _dft_tables(s)
    ut = u.reshape(b, n1, n2, c).transpose(0, 2, 1, 3)
    bt = pl.pallas_call(
        functools.partial(_fft_a_kernel, n1=n1),
        grid=(b, n2 // t2),
        in_specs=[pl.BlockSpec((None, t2, n1, c), lambda bi, i: (bi, i, 0, 0)),
                  pl.BlockSpec((2 * n1, n1), lambda bi, i: (0, 0)),
                  pl.BlockSpec((t2, n1, 1), lambda bi, i: (i, 0, 0)),
                  pl.BlockSpec((t2, n1, 1), lambda bi, i: (i, 0, 0))],
        out_specs=pl.BlockSpec((None, t2, 2, n1, c), lambda bi, i: (bi, i, 0, 0, 0)),
        out_shape=jax.ShapeDtypeStruct((b, n2, 2, n1, c), BF16),
        compiler_params=_params(("parallel", "arbitrary")),
        name="fft_stage_a",
    )(ut, f1, tc, ts)
    bk = bt.transpose(0, 3, 2, 1, 4)
    y = pl.pallas_call(
        _fft_b_kernel,
        grid=(b, n1 // tk1),
        in_specs=[pl.BlockSpec((None, tk1, 2, n2, c), lambda bi, i: (bi, i, 0, 0, 0)),
                  pl.BlockSpec((2 * n2, 2 * n2), lambda bi, i: (0, 0)),
                  pl.BlockSpec((2 * GROUP_DIM, GROUP_DIM), lambda bi, i: (0, 0))],
        out_specs=pl.BlockSpec((None, tk1, n2, c), lambda bi, i: (bi, i, 0, 0)),
        out_shape=jax.ShapeDtypeStruct((b, n1, n2, c), BF16),
        compiler_params=_params(("parallel", "arbitrary")),
        name="fft_stage_b",
    )(bk, m2, cs)
    return y.transpose(0, 2, 1, 3).reshape(b, s, c)


def _merge_kernel(yf_ref, oa_ref, om_ref, g0_ref, g1_ref, g2_ref,
                  wf_ref, wa_ref, wm_ref, o_ref):
    def gate(g_ref):
        return jax.nn.sigmoid(g_ref[...].astype(F32))
    yf = jnp.dot(yf_ref[...], wf_ref[...], preferred_element_type=F32)
    ya = jnp.dot(oa_ref[...], wa_ref[...], preferred_element_type=F32)
    ym = jnp.dot(om_ref[...], wm_ref[...], preferred_element_type=F32)
    o_ref[...] = (gate(g0_ref) * yf + gate(g1_ref) * ya + gate(g2_ref) * ym
                  ).astype(o_ref.dtype)


def _merge(yf, oa, om, z1, wf, wa, wm, tm, tn):
    t = yf.shape[0]
    tm = min(tm, t)
    nb = D_MODEL // tn
    def gspec(k):
        return pl.BlockSpec((tm, tn), lambda i, j: (i, k * nb + j))
    def aspec(w):
        return pl.BlockSpec((tm, w), lambda i, j: (i, 0))
    def wspec(w):
        return pl.BlockSpec((w, tn), lambda i, j: (0, j))
    return pl.pallas_call(
        _merge_kernel,
        grid=(t // tm, nb),
        in_specs=[aspec(FOURIER_DIM), aspec(MLA_HEADS * V_HEAD), aspec(MEM_DIM),
                  gspec(0), gspec(1), gspec(2),
                  wspec(FOURIER_DIM), wspec(MLA_HEADS * V_HEAD), wspec(MEM_DIM)],
        out_specs=pl.BlockSpec((tm, tn), lambda i, j: (i, j)),
        out_shape=jax.ShapeDtypeStruct((t, D_MODEL), BF16),
        compiler_params=_params(("parallel", "arbitrary")),
        name="branch_merge",
    )(yf, oa, om, z1, z1, z1, wf, wa, wm)


def _oproj_kernel(a_ref, w_ref, x_ref, gpost_ref, gpre_ref, x1_ref, h_ref):
    y = jnp.dot(a_ref[...], w_ref[...], preferred_element_type=F32)
    x1 = x_ref[...] + _rms(y, gpost_ref[...])
    x1_ref[...] = x1
    h_ref[...] = _rms(x1, gpre_ref[...]).astype(h_ref.dtype)


def _oproj(a, w, x, gpost, gpre, tm):
    t, d = x.shape
    tm = min(tm, t)
    row = lambda i: (i, 0)
    fix = lambda i: (0, 0)
    return pl.pallas_call(
        _oproj_kernel,
        grid=(t // tm,),
        in_specs=[pl.BlockSpec((tm, d), row), pl.BlockSpec((d, d), fix),
                  pl.BlockSpec((tm, d), row), pl.BlockSpec((1, d), fix),
                  pl.BlockSpec((1, d), fix)],
        out_specs=[pl.BlockSpec((tm, d), row), pl.BlockSpec((tm, d), row)],
        out_shape=[jax.ShapeDtypeStruct((t, d), F32), jax.ShapeDtypeStruct((t, d), BF16)],
        compiler_params=_params(("parallel",)),
        name="out_proj_norm",
    )(a, w, x, gpost.reshape(1, d), gpre.reshape(1, d))


def _gelu_tanh(x):
    return 0.5 * x * (1.0 + jnp.tanh(0.7978845608028654 * (x + 0.044715 * x * x * x)))


def _down_kernel(gu_ref, gp_ref, gn_ref, cw_ref, cb_ref, w_ref, x_ref,
                 gpost_ref, gpre_ref, *refs, tiles_per_seq, halo, emit_h):
    act_ref, acc_ref = refs[-2:]
    nk = w_ref.shape[0]
    tm = x_ref.shape[0]
    pos = pl.program_id(0) % tiles_per_seq
    first = pos == 0
    last = pos == tiles_per_seq - 1

    def act(k, slot):
        g = gu_ref[k].astype(F32)
        prev_row = jnp.where(first, 0.0, gp_ref[k, halo - 1:halo, :].astype(F32))
        next_row = jnp.where(last, 0.0, gn_ref[k, 0:1, :].astype(F32))
        rows = lax.broadcasted_iota(jnp.int32, g.shape, 0)
        g_prev = jnp.where(rows == 0, prev_row, pltpu.roll(g, 1, 0))
        g_next = jnp.where(rows == tm - 1, next_row, pltpu.roll(g, tm - 1, 0))
        cw = cw_ref[k]
        conv = g_prev * cw[0:1] + g * cw[1:2] + g_next * cw[2:3] + cb_ref[k]
        act_ref[slot] = (_gelu_tanh(conv) * gu_ref[nk + k].astype(F32)).astype(BF16)

    def down(k, slot):
        acc_ref[...] += jnp.dot(act_ref[slot], w_ref[k], preferred_element_type=F32)

    acc_ref[...] = jnp.zeros_like(acc_ref)
    act(0, 0)

    def pair(j, carry):
        k = 2 * j
        act(k + 1, 1)
        down(k, 0)
        act(k + 2, 0)
        down(k + 1, 1)
        return carry

    lax.fori_loop(0, nk // 2, pair, 0)
    down(nk - 1, 0)
    x2 = x_ref[...] + _rms(acc_ref[...], gpost_ref[...])
    refs[0][...] = x2
    if emit_h:
        refs[1][...] = _rms(x2, gpre_ref[...]).astype(refs[1].dtype)


def _down(gu, cw, cb, w, x, gpost, gpre, seq, tm, emit_h):
    t, d = x.shape
    nk, tk, _ = w.shape
    assert nk % 2 == 1
    tm = min(tm, seq)
    halo = 16
    tiles_per_seq = seq // tm
    hb = tm // halo
    last_hb = t // halo - 1
    row = lambda i: (i, 0)
    fix = lambda i: (0, 0)
    fix3 = lambda i: (0, 0, 0)
    out_specs = [pl.BlockSpec((tm, d), row)]
    out_shape = [jax.ShapeDtypeStruct((t, d), F32)]
    if emit_h:
        out_specs.append(pl.BlockSpec((tm, d), row))
        out_shape.append(jax.ShapeDtypeStruct((t, d), BF16))
    outs = pl.pallas_call(
        functools.partial(_down_kernel, tiles_per_seq=tiles_per_seq, halo=halo,
                          emit_h=emit_h),
        grid=(t // tm,),
        in_specs=[pl.BlockSpec((2 * nk, tm, tk), lambda i: (0, i, 0)),
                  pl.BlockSpec((nk, halo, tk), lambda i: (0, jnp.maximum(i * hb - 1, 0), 0)),
                  pl.BlockSpec((nk, halo, tk),
                               lambda i: (0, jnp.minimum((i + 1) * hb, last_hb), 0)),
                  pl.BlockSpec((nk, 3, tk), fix3),
                  pl.BlockSpec((nk, 1, tk), fix3),
                  pl.BlockSpec((nk, tk, d), fix3, pipeline_mode=pl.Buffered(1)),
                  pl.BlockSpec((tm, d), row),
                  pl.BlockSpec((1, d), fix), pl.BlockSpec((1, d), fix)],
        out_specs=out_specs,
        out_shape=out_shape,
        scratch_shapes=[pltpu.VMEM((2, tm, tk), BF16), pltpu.VMEM((tm, d), F32)],
        compiler_params=_params(("parallel",)),
        name="ffn_down_norm",
    )(gu, gu, gu, cw, cb, w, x, gpost.reshape(1, d), gpre.reshape(1, d))
    return outs if emit_h else (outs[0], None)


def _pack_weights(p, l):
    s0 = FOURIER_DIM
    s1 = s0 + Q_LORA
    s2 = s1 + KV_LORA
    s3 = s2 + QK_ROPE
    s4 = s3 + MEM_DIM
    w_in = p["w_in"][l]
    d = w_in.shape[0]
    bf = lambda a: a.astype(BF16)
    w_gq = bf(jnp.concatenate([w_in[:, s4:], w_in[:, s3:s4]], axis=1))
    w_f = bf(w_in[:, :s0])
    w_lat = bf(jnp.concatenate([w_in[:, s0:s3], jnp.zeros((d, KPE_PAD - QK_ROPE), F32)],
                               axis=1))
    wq = p["w_uq"][l].reshape(Q_LORA, MLA_HEADS, QK_NOPE + QK_ROPE)
    wq = jnp.pad(wq, ((0, 0), (0, 0), (0, QK_PAD - QK_NOPE - QK_ROPE)))
    wq = bf(wq.reshape(Q_LORA, MLA_HEADS * QK_PAD))
    wkv = p["w_ukv"][l].reshape(KV_LORA, MLA_HEADS, QK_NOPE + V_HEAD)
    wk = jnp.pad(wkv[:, :, :QK_NOPE], ((0, KPE_PAD), (0, 0), (0, QK_PAD - QK_NOPE)))
    eye = jnp.eye(KPE_PAD, QK_PAD - QK_NOPE, dtype=F32)[:, None, :]
    eye = eye * (jnp.arange(KPE_PAD) < QK_ROPE)[:, None, None]
    wk = wk.at[KV_LORA:, :, QK_NOPE:].set(jnp.broadcast_to(
        eye, (KPE_PAD, MLA_HEADS, QK_PAD - QK_NOPE)))
    w_k = bf(wk.reshape(CKV_EXT, -1))
    wvt = jnp.pad(wkv[:, :, QK_NOPE:].transpose(1, 2, 0),
                  ((0, 0), (0, V_ROWS - V_HEAD), (0, KPE_PAD)))
    wvt = wvt.at[:, V_HEAD, ONE_LANE].set(1.0)
    w_vt = bf(wvt.reshape(MLA_HEADS * V_ROWS, CKV_EXT))
    w_gu = bf(jnp.concatenate([p["w_ffn_gate"][l], p["w_ffn_up"][l]], axis=1))
    return dict(
        w_gq=w_gq, w_f=w_f, w_lat=w_lat, wq=wq, w_k=w_k, w_vt=w_vt, w_gu=w_gu,
        w_mem_kv=bf(p["w_mem_kv"][l]), w_fo=bf(p["w_fourier_out"][l]),
        w_ao=bf(p["w_attn_out"][l]), w_mo=bf(p["w_mem_out"][l]), w_o=bf(p["w_o"][l]),
        w_down=bf(p["w_ffn_down"][l]).reshape(D_FF // FF_CHUNK, FF_CHUNK, -1),
        conv_w=p["ffn_conv_w"][l].reshape(-1, D_FF // FF_CHUNK, FF_CHUNK).transpose(1, 0, 2),
        conv_b=p["ffn_conv_b"][l].reshape(D_FF // FF_CHUNK, 1, FF_CHUNK),
        q_norm=p["q_norm"][l], kv_norm=p["kv_norm"][l], mem_norm=p["mem_norm"][l],
        pre_mix=p["pre_mix_norm"][l], post_mix=p["post_mix_norm"][l],
        pre_ffn=p["pre_ffn_norm"][l], post_ffn=p["post_ffn_norm"][l])


def _rope_tables(seq):
    inv_freq = 1.0 / (ROPE_THETA ** (jnp.arange(0, QK_ROPE, 2, dtype=F32) / QK_ROPE))
    ang = jnp.arange(seq, dtype=F32)[:, None] * inv_freq[None, :]
    cos, sin = jnp.cos(ang), jnp.sin(ang)
    half = QK_ROPE // 2
    zh = jnp.zeros((seq, half), F32)
    zr = jnp.zeros((seq, LANES - QK_ROPE), F32)
    c = jnp.concatenate([cos, cos, zr], axis=1)
    s1 = jnp.concatenate([-sin, zh, zr], axis=1)
    s2 = jnp.concatenate([zh, sin, zr], axis=1)
    return c, s1, s2


def _trunk(x, mem, layers):
    b, s, d = x.shape
    t = b * s
    n_mem = mem.shape[1]
    tm = min(512, s)
    tabs = _rope_tables(s)
    q_scale = (QK_NOPE + QK_ROPE) ** -0.5 * LOG2E
    gate_w = 3 * D_MODEL
    xf = x.reshape(t, d)
    memf = mem.reshape(b * n_mem, d)
    h = _norm(xf, layers[0]["pre_mix"], tm)
    for l, w in enumerate(layers):
        z1 = _mm(h, w["w_gq"], 1024, 1024, "in_proj_gates")
        f_in = _mm(h, w["w_f"], 1024, 1024, "in_proj_fourier")
        cq, ckv = _latent(h, w["w_lat"], w["q_norm"], w["kv_norm"], tabs, s, tm)
        q = _qup(cq, w["wq"], tabs, s, tm, q_scale)
        kc = _mm(ckv, w["w_k"], 1024, 1024, "k_up")
        vt = _vup(ckv.reshape(b, s, -1), w["w_vt"], 512)
        oa = _mla(q.reshape(b, s, -1), kc.reshape(b, s, -1), vt, 2048, 1024, 256)
        mem_n = _norm(memf, w["mem_norm"], n_mem)
        kvm = _mm(mem_n, w["w_mem_kv"], 1024, 1024, "mem_kv")
        om = _mem_attn(z1.reshape(b, s, -1), kvm.reshape(b, n_mem, -1),
                       gate_w // MEM_DIM, 1024)
        yf = _fourier(f_in.reshape(b, s, -1), 16, 16)
        merged = _merge(yf.reshape(t, -1), oa.reshape(t, -1), om.reshape(t, -1), z1,
                        w["w_fo"], w["w_ao"], w["w_mo"], 1024, 512)
        x1, h2 = _oproj(merged, w["w_o"], xf, w["post_mix"], w["pre_ffn"], 256)
        gu = _mm_chunked(h2, w["w_gu"], 1024, 2 * FF_CHUNK, FF_CHUNK, "ffn_gate_up")
        last = l == len(layers) - 1
        nxt = layers[0 if last else l + 1]["pre_mix"]
        xf, h = _down(gu, w["conv_w"], w["conv_b"], w["w_down"], x1, w["post_ffn"], nxt,
                      s, 256, not last)
    return xf.reshape(b, s, d)


def kernel(x_prompt, x_sample, mem_prompt, mem_sample, pre_mix_norm, w_in, q_norm, w_uq,
           kv_norm, w_ukv, mem_norm, w_mem_kv, w_fourier_out, w_attn_out, w_mem_out, w_o,
           post_mix_norm, pre_ffn_norm, w_ffn_gate, w_ffn_up, ffn_conv_w, ffn_conv_b,
           w_ffn_down, post_ffn_norm):
    p = dict(pre_mix_norm=pre_mix_norm, w_in=w_in, q_norm=q_norm, w_uq=w_uq,
             kv_norm=kv_norm, w_ukv=w_ukv, mem_norm=mem_norm, w_mem_kv=w_mem_kv,
             w_fourier_out=w_fourier_out, w_attn_out=w_attn_out, w_mem_out=w_mem_out,
             w_o=w_o, post_mix_norm=post_mix_norm, pre_ffn_norm=pre_ffn_norm,
             w_ffn_gate=w_ffn_gate, w_ffn_up=w_ffn_up, ffn_conv_w=ffn_conv_w,
             ffn_conv_b=ffn_conv_b, w_ffn_down=w_ffn_down, post_ffn_norm=post_ffn_norm)
    layers = [_pack_weights(p, l) for l in range(w_in.shape[0])]
    return (_trunk(x_prompt, mem_prompt, layers), _trunk(x_sample, mem_sample, layers))
```

```python
import functools
import math

import numpy as np
import jax
import jax.numpy as jnp
from jax import lax
from jax.experimental import pallas as pl
from jax.experimental.pallas import tpu as pltpu

D_MODEL = 2048
N_GROUPS = 4
GROUP_DIM = 256
FOURIER_DIM = N_GROUPS * GROUP_DIM
MLA_HEADS = 16
Q_LORA = 512
KV_LORA = 512
QK_NOPE = 128
QK_ROPE = 64
V_HEAD = 128
ROPE_THETA = 10000.0
MEM_HEADS = 4
MEM_HEAD_DIM = 256
MEM_DIM = MEM_HEADS * MEM_HEAD_DIM
D_FF = 5632
EPS = 1e-6

LANES = 128
QK_PAD = 256
KPE_PAD = LANES
V_ROWS = V_HEAD + 16
N_SLOTS = 3
FFT_N2 = 128
VMEM_LIMIT = 56 * 1024 * 1024
FF_CHUNK = 512
LOG2E = 1.4426950408889634

BF16 = jnp.bfloat16
F32 = jnp.float32


def _params(sem):
    return pltpu.CompilerParams(dimension_semantics=sem, vmem_limit_bytes=VMEM_LIMIT)


def _rms(x, g):
    ms = jnp.mean(x * x, axis=-1, keepdims=True)
    return x * lax.rsqrt(ms + EPS) * g


def _rope128(x, c, s1, s2):
    return x * c + pltpu.roll(x, 96, 1) * s1 + pltpu.roll(x, 32, 1) * s2


def _norm_kernel(x_ref, g_ref, o_ref):
    o_ref[...] = _rms(x_ref[...].astype(F32), g_ref[...]).astype(o_ref.dtype)


def _norm(x, g, tm):
    t, d = x.shape
    return pl.pallas_call(
        _norm_kernel,
        grid=(t // tm,),
        in_specs=[pl.BlockSpec((tm, d), lambda i: (i, 0)),
                  pl.BlockSpec((1, d), lambda i: (0, 0))],
        out_specs=pl.BlockSpec((tm, d), lambda i: (i, 0)),
        out_shape=jax.ShapeDtypeStruct((t, d), BF16),
        compiler_params=_params(("parallel",)),
        name="rmsnorm",
    )(x, g.reshape(1, d))


def _mm_kernel(a_ref, w_ref, o_ref):
    o_ref[...] = jnp.dot(a_ref[...], w_ref[...],
                         preferred_element_type=F32).astype(o_ref.dtype)


def _mm(a, w, tm, tn, name):
    m, k = a.shape
    n = w.shape[1]
    tm = min(tm, m)
    tn = min(tn, n)
    return pl.pallas_call(
        _mm_kernel,
        grid=(m // tm, n // tn),
        in_specs=[pl.BlockSpec((tm, k), lambda i, j: (i, 0)),
                  pl.BlockSpec((k, tn), lambda i, j: (0, j))],
        out_specs=pl.BlockSpec((tm, tn), lambda i, j: (i, j)),
        out_shape=jax.ShapeDtypeStruct((m, n), BF16),
        compiler_params=_params(("parallel", "arbitrary")),
        name=name,
    )(a, w)


def _attn_prep_kernel(h_ref, wl_ref, qg_ref, kvg_ref, wq_ref, wk_ref, wvt_ref,
                      c_ref, s1_ref, s2_ref, q_ref, k_ref, vt_ref, *, scale):
    c, s1, s2 = c_ref[...], s1_ref[...], s2_ref[...]
    z = jnp.dot(h_ref[...], wl_ref[...], preferred_element_type=F32)
    cq = _rms(z[:, :Q_LORA], qg_ref[...]).astype(BF16)
    lat = _rms(z[:, Q_LORA:Q_LORA + KV_LORA], kvg_ref[...]).astype(BF16)
    kpe = _rope128(z[:, Q_LORA + KV_LORA:], c, s1, s2).astype(BF16)
    zq = jnp.dot(cq, wq_ref[...], preferred_element_type=F32) * scale
    kn = jnp.dot(lat, wk_ref[...], preferred_element_type=F32).astype(BF16)
    vt = lax.dot_general(wvt_ref[...], lat, (((1,), (1,)), ((), ())),
                         preferred_element_type=F32).astype(BF16)
    pad_rows = V_ROWS - V_HEAD
    row = lax.broadcasted_iota(jnp.int32, (pad_rows, vt.shape[1]), 0)
    ones_rows = jnp.where(row == 0, 1.0, 0.0).astype(BF16)
    for h in range(MLA_HEADS):
        lo = h * QK_PAD
        q_ref[:, lo:lo + QK_NOPE] = zq[:, lo:lo + QK_NOPE].astype(BF16)
        pe = _rope128(zq[:, lo + QK_NOPE:lo + QK_PAD], c, s1, s2)
        q_ref[:, lo + QK_NOPE:lo + QK_PAD] = pe.astype(BF16)
        k_ref[:, lo:lo + QK_NOPE] = kn[:, h * QK_NOPE:(h + 1) * QK_NOPE]
        k_ref[:, lo + QK_NOPE:lo + QK_PAD] = kpe
        vt_ref[h * V_ROWS:h * V_ROWS + V_HEAD, :] = vt[h * V_HEAD:(h + 1) * V_HEAD, :]
        vt_ref[h * V_ROWS + V_HEAD:(h + 1) * V_ROWS, :] = ones_rows


def _attn_prep(h, w, tabs, b, seq, tm, scale):
    d = h.shape[1]
    tm = min(tm, seq)
    nblk = seq // tm
    nq = MLA_HEADS * QK_PAD
    nv = MLA_HEADS * V_ROWS
    row = lambda bi, i: (bi * nblk + i, 0)
    fix = lambda bi, i: (0, 0)
    tab = pl.BlockSpec((tm, LANES), lambda bi, i: (i, 0))
    def wspec(arr):
        return pl.BlockSpec(arr.shape, fix, pipeline_mode=pl.Buffered(1))
    return pl.pallas_call(
        functools.partial(_attn_prep_kernel, scale=scale),
        grid=(b, nblk),
        in_specs=[pl.BlockSpec((tm, d), row), wspec(w["w_lat"]),
                  pl.BlockSpec((1, Q_LORA), fix), pl.BlockSpec((1, KV_LORA), fix),
                  wspec(w["wq"]), wspec(w["wk"]), wspec(w["wvt"]), tab, tab, tab],
        out_specs=[pl.BlockSpec((tm, nq), row), pl.BlockSpec((tm, nq), row),
                   pl.BlockSpec((None, nv, tm), lambda bi, i: (bi, 0, i))],
        out_shape=[jax.ShapeDtypeStruct((b * seq, nq), BF16),
                   jax.ShapeDtypeStruct((b * seq, nq), BF16),
                   jax.ShapeDtypeStruct((b, nv, seq), BF16)],
        compiler_params=_params(("parallel", "arbitrary")),
        name="attn_prep",
    )(h, w["w_lat"], w["q_norm"].reshape(1, -1), w["kv_norm"].reshape(1, -1),
      w["wq"], w["wk"], w["wvt"], *tabs)


def _mla_kernel(q_ref, k_ref, vt_ref, o_ref, s_ref, p_ref, qt_ref, *, tq, tk):
    n_q = q_ref.shape[0] // tq
    n_k = k_ref.shape[0] // tk

    def q_tile(qi, carry):
        q0 = pl.multiple_of(qi * tq, tq)
        qt_ref[...] = q_ref[pl.ds(q0, tq), :].T

        def scores(c, slot):
            k = k_ref[pl.ds(pl.multiple_of(c * tk, tk), tk), :]
            s = jnp.dot(k, qt_ref[...], preferred_element_type=F32)
            s_ref[slot] = s
            return jnp.max(s, axis=0, keepdims=True)

        def softmax(slot, cmax, m):
            m_new = jnp.maximum(m, cmax)
            p_ref[slot] = jnp.exp2((s_ref[slot] - m_new).astype(BF16))
            return m_new, jnp.exp2(m - m_new)

        def pv(c, slot):
            vt = vt_ref[:, pl.ds(pl.multiple_of(c * tk, tk), tk)]
            return jnp.dot(vt, p_ref[slot], preferred_element_type=F32)

        def stage(c, u, m, acc, cmax):
            part = pv(c - 1, (u - 1) % N_SLOTS)
            cmax_next = scores(c + 1, (u + 1) % N_SLOTS)
            m, alpha = softmax(u % N_SLOTS, cmax, m)
            return m, (acc + part) * alpha, cmax_next

        cmax = scores(0, 0)
        m, _ = softmax(0, cmax, jnp.full((1, tq), -1e30, F32))
        cmax = scores(1, 1)
        acc = jnp.zeros((V_ROWS, tq), F32)

        n_mid = n_k - 2
        unroll = 4 * N_SLOTS
        peel = n_mid % unroll
        st = (m, acc, cmax)
        for u in range(1, peel + 1):
            st = stage(u, u, *st)

        def group(j, st):
            for u in range(1, unroll + 1):
                st = stage(peel + unroll * j + u, peel + u, *st)
            return st

        trips = n_mid // unroll
        if trips == 1:
            st = group(0, st)
        elif trips > 1:
            st = lax.fori_loop(0, trips, group, st)
        m, acc, cmax = st
        part = pv(n_k - 2, (n_k - 2) % N_SLOTS)
        m, alpha = softmax((n_k - 1) % N_SLOTS, cmax, m)
        acc = (acc + part) * alpha + pv(n_k - 1, (n_k - 1) % N_SLOTS)
        out = acc[:V_HEAD] * (1.0 / acc[V_HEAD:V_HEAD + 1])
        o_ref[pl.ds(q0, tq), :] = out.T.astype(o_ref.dtype)
        return carry

    lax.fori_loop(0, n_q, q_tile, 0)


def _mla(q, k, vt, tq_outer, tq, tk):
    b, s, _ = q.shape
    tq_outer = min(tq_outer, s)
    tq = min(tq, tq_outer)
    tk = min(tk, s // 2)
    assert (s // tk) % 2 == 0
    return pl.pallas_call(
        functools.partial(_mla_kernel, tq=tq, tk=tk),
        grid=(b, MLA_HEADS, s // tq_outer),
        in_specs=[pl.BlockSpec((None, tq_outer, QK_PAD), lambda bi, h, qi: (bi, qi, h)),
                  pl.BlockSpec((None, s, QK_PAD), lambda bi, h, qi: (bi, 0, h)),
                  pl.BlockSpec((None, V_ROWS, s), lambda bi, h, qi: (bi, h, 0))],
        out_specs=pl.BlockSpec((None, tq_outer, V_HEAD), lambda bi, h, qi: (bi, qi, h)),
        out_shape=jax.ShapeDtypeStruct((b, s, MLA_HEADS * V_HEAD), BF16),
        scratch_shapes=[pltpu.VMEM((N_SLOTS, tk, tq), F32),
                        pltpu.VMEM((N_SLOTS, tk, tq), BF16),
                        pltpu.VMEM((QK_PAD, tq), BF16)],
        compiler_params=_params(("parallel", "parallel", "arbitrary")),
        name="mla_attention",
    )(q, k, vt)


def _mem_kernel(q_ref, kv_ref, o_ref, *, scale):
    for h in range(MEM_HEADS):
        lo = h * MEM_HEAD_DIM
        q = q_ref[:, lo:lo + MEM_HEAD_DIM]
        k = kv_ref[:, lo:lo + MEM_HEAD_DIM]
        v = kv_ref[:, MEM_DIM + lo:MEM_DIM + lo + MEM_HEAD_DIM]
        s = lax.dot_general(q, k, (((1,), (1,)), ((), ())),
                            preferred_element_type=F32) * scale
        p = jnp.exp2(s - jnp.max(s, axis=-1, keepdims=True))
        l = jnp.sum(p, axis=-1, keepdims=True)
        o = jnp.dot(p.astype(BF16), v, preferred_element_type=F32)
        o_ref[:, lo:lo + MEM_HEAD_DIM] = (o * (1.0 / l)).astype(o_ref.dtype)


def _mem_attn(z1, kvm, q_blk, tq):
    b, s, _ = z1.shape
    n_mem = kvm.shape[1]
    tq = min(tq, s)
    scale = MEM_HEAD_DIM ** -0.5 * LOG2E
    return pl.pallas_call(
        functools.partial(_mem_kernel, scale=scale),
        grid=(b, s // tq),
        in_specs=[pl.BlockSpec((None, tq, MEM_DIM), lambda bi, i: (bi, i, q_blk)),
                  pl.BlockSpec((None, n_mem, 2 * MEM_DIM), lambda bi, i: (bi, 0, 0))],
        out_specs=pl.BlockSpec((None, tq, MEM_DIM), lambda bi, i: (bi, i, 0)),
        out_shape=jax.ShapeDtypeStruct((b, s, MEM_DIM), BF16),
        compiler_params=_params(("parallel", "arbitrary")),
        name="mem_attention",
    )(z1, kvm)


def _fft_a_kernel(u_ref, f_ref, tc_ref, ts_ref, o_ref, *, n1):
    f = f_ref[...]
    for j in range(u_ref.shape[0]):
        r = jnp.dot(f, u_ref[j], preferred_element_type=F32)
        br, bi = r[:n1], r[n1:]
        tc, ts = tc_ref[j], ts_ref[j]
        o_ref[j, 0] = (br * tc + bi * ts).astype(o_ref.dtype)
        o_ref[j, 1] = (bi * tc - br * ts).astype(o_ref.dtype)


def _fft_b_kernel(b_ref, m_ref, cs_ref, o_ref):
    m2 = m_ref[...]
    cs = cs_ref[...]
    n2 = o_ref.shape[1]
    for j in range(b_ref.shape[0]):
        x = b_ref[j].reshape(2 * n2, FOURIER_DIM)
        z = jnp.dot(m2, x, preferred_element_type=F32).astype(BF16)
        for g in range(N_GROUPS):
            lo = g * GROUP_DIM
            zz = jnp.concatenate([z[:n2, lo:lo + GROUP_DIM], z[n2:, lo:lo + GROUP_DIM]],
                                 axis=1)
            y = jnp.dot(zz, cs, preferred_element_type=F32)
            o_ref[j, :, lo:lo + GROUP_DIM] = y.astype(o_ref.dtype)


def _dft_tables(seq):
    n2 = FFT_N2
    n1 = seq // n2
    a1 = 2.0 * np.pi * np.outer(np.arange(n1), np.arange(n1)) / n1
    f1 = np.concatenate([np.cos(a1), -np.sin(a1)], axis=0) / math.sqrt(n1)
    th = 2.0 * np.pi * np.outer(np.arange(n2), np.arange(n1)) / seq
    a2 = 2.0 * np.pi * np.outer(np.arange(n2), np.arange(n2)) / n2
    c2, s2 = np.cos(a2), np.sin(a2)
    m2 = np.block([[c2, s2], [-s2, c2]]) / math.sqrt(n2)
    ac = 2.0 * np.pi * np.outer(np.arange(GROUP_DIM), np.arange(GROUP_DIM)) / GROUP_DIM
    cs = np.concatenate([np.cos(ac), np.sin(ac)], axis=0) / math.sqrt(GROUP_DIM)
    return (jnp.asarray(f1, BF16), jnp.asarray(np.cos(th)[:, :, None], F32),
            jnp.asarray(np.sin(th)[:, :, None], F32), jnp.asarray(m2, BF16),
            jnp.asarray(cs, BF16))


def _fourier(u, t2, tk1):
    b, s, c = u.shape
    n2 = FFT_N2
    n1 = s // n2
    t2 = min(t2, n2)
    tk1 = min(tk1, n1)
    f1, tc, ts, m2, cs = _dft_tables(s)
    ut = u.reshape(b, n1, n2, c).transpose(0, 2, 1, 3)
    bt = pl.pallas_call(
        functools.partial(_fft_a_kernel, n1=n1),
        grid=(b, n2 // t2),
        in_specs=[pl.BlockSpec((None, t2, n1, c), lambda bi, i: (bi, i, 0, 0)),
                  pl.BlockSpec((2 * n1, n1), lambda bi, i: (0, 0)),
                  pl.BlockSpec((t2, n1, 1), lambda bi, i: (i, 0, 0)),
                  pl.BlockSpec((t2, n1, 1), lambda bi, i: (i, 0, 0))],
        out_specs=pl.BlockSpec((None, t2, 2, n1, c), lambda bi, i: (bi, i, 0, 0, 0)),
        out_shape=jax.ShapeDtypeStruct((b, n2, 2, n1, c), BF16),
        compiler_params=_params(("parallel", "arbitrary")),
        name="fft_stage_a",
    )(ut, f1, tc, ts)
    bk = bt.transpose(0, 3, 2, 1, 4)
    y = pl.pallas_call(
        _fft_b_kernel,
        grid=(b, n1 // tk1),
        in_specs=[pl.BlockSpec((None, tk1, 2, n2, c), lambda bi, i: (bi, i, 0, 0, 0)),
                  pl.BlockSpec((2 * n2, 2 * n2), lambda bi, i: (0, 0)),
                  pl.BlockSpec((2 * GROUP_DIM, GROUP_DIM), lambda bi, i: (0, 0))],
        out_specs=pl.BlockSpec((None, tk1, n2, c), lambda bi, i: (bi, i, 0, 0)),
        out_shape=jax.ShapeDtypeStruct((b, n1, n2, c), BF16),
        compiler_params=_params(("parallel", "arbitrary")),
        name="fft_stage_b",
    )(bk, m2, cs)
    return y.transpose(0, 2, 1, 3).reshape(b, s, c)


def _merge_kernel(yf_ref, oa_ref, om_ref, g0_ref, g1_ref, g2_ref,
                  wf_ref, wa_ref, wm_ref, o_ref):
    def gate(g_ref):
        return jax.nn.sigmoid(g_ref[...].astype(F32))
    yf = jnp.dot(yf_ref[...], wf_ref[...], preferred_element_type=F32)
    ya = jnp.dot(oa_ref[...], wa_ref[...], preferred_element_type=F32)
    ym = jnp.dot(om_ref[...], wm_ref[...], preferred_element_type=F32)
    o_ref[...] = (gate(g0_ref) * yf + gate(g1_ref) * ya + gate(g2_ref) * ym
                  ).astype(o_ref.dtype)


def _merge(yf, oa, om, z1, wf, wa, wm, tm, tn):
    t = yf.shape[0]
    tm = min(tm, t)
    nb = D_MODEL // tn
    def gspec(k):
        return pl.BlockSpec((tm, tn), lambda i, j: (i, k * nb + j))
    def aspec(w):
        return pl.BlockSpec((tm, w), lambda i, j: (i, 0))
    def wspec(w):
        return pl.BlockSpec((w, tn), lambda i, j: (0, j))
    return pl.pallas_call(
        _merge_kernel,
        grid=(t // tm, nb),
        in_specs=[aspec(FOURIER_DIM), aspec(MLA_HEADS * V_HEAD), aspec(MEM_DIM),
                  gspec(0), gspec(1), gspec(2),
                  wspec(FOURIER_DIM), wspec(MLA_HEADS * V_HEAD), wspec(MEM_DIM)],
        out_specs=pl.BlockSpec((tm, tn), lambda i, j: (i, j)),
        out_shape=jax.ShapeDtypeStruct((t, D_MODEL), BF16),
        compiler_params=_params(("parallel", "arbitrary")),
        name="branch_merge",
    )(yf, oa, om, z1, z1, z1, wf, wa, wm)


def _proj_norm_kernel(a_ref, w_ref, x_ref, gpost_ref, gpre_ref, x1_ref, *h_ref):
    y = jnp.dot(a_ref[...], w_ref[...], preferred_element_type=F32)
    x1 = x_ref[...] + _rms(y, gpost_ref[...])
    x1_ref[...] = x1
    if h_ref:
        h_ref[0][...] = _rms(x1, gpre_ref[...]).astype(h_ref[0].dtype)


def _proj_norm(a, w, x, gpost, gpre, tm, emit_h, name):
    t, d = x.shape
    k = a.shape[1]
    tm = min(tm, t)
    row = lambda i: (i, 0)
    fix = lambda i: (0, 0)
    out_specs = [pl.BlockSpec((tm, d), row)]
    out_shape = [jax.ShapeDtypeStruct((t, d), F32)]
    if emit_h:
        out_specs.append(pl.BlockSpec((tm, d), row))
        out_shape.append(jax.ShapeDtypeStruct((t, d), BF16))
    outs = pl.pallas_call(
        _proj_norm_kernel,
        grid=(t // tm,),
        in_specs=[pl.BlockSpec((tm, k), row),
                  pl.BlockSpec((k, d), fix, pipeline_mode=pl.Buffered(1)),
                  pl.BlockSpec((tm, d), row), pl.BlockSpec((1, d), fix),
                  pl.BlockSpec((1, d), fix)],
        out_specs=out_specs,
        out_shape=out_shape,
        compiler_params=_params(("parallel",)),
        name=name,
    )(a, w, x, gpost.reshape(1, d), gpre.reshape(1, d))
    return outs if emit_h else (outs[0], None)


def _gelu_tanh(x):
    return 0.5 * x * (1.0 + jnp.tanh(0.7978845608028654 * (x + 0.044715 * x * x * x)))


def _ffn_act_kernel(a_ref, ap_ref, an_ref, w_ref, cw_ref, cb_ref, o_ref, ax_ref, *,
                    tiles_per_seq, halo):
    tm = a_ref.shape[0]
    tc = o_ref.shape[1]

    pos = pl.program_id(0) % tiles_per_seq

    @pl.when(pl.program_id(1) == 0)
    def _():
        zero = jnp.zeros_like(ap_ref[...])
        ax_ref[0:halo] = jnp.where(pos == 0, zero, ap_ref[...])
        ax_ref[halo:halo + tm] = a_ref[...]
        ax_ref[halo + tm:] = jnp.where(pos == tiles_per_seq - 1, zero, an_ref[...])

    r = jnp.dot(ax_ref[...], w_ref[...], preferred_element_type=F32)
    g = r[:, :tc]
    g_prev = pltpu.roll(g, 1, 0)[halo:halo + tm]
    g_next = pltpu.roll(g, tm + 2 * halo - 1, 0)[halo:halo + tm]
    cw = cw_ref[...]
    conv = g_prev * cw[0:1] + g[halo:halo + tm] * cw[1:2] + g_next * cw[2:3] + cb_ref[...]
    o_ref[...] = (_gelu_tanh(conv) * r[halo:halo + tm, tc:]).astype(o_ref.dtype)


def _ffn_act(h, w_gu, cw, cb, seq, tm, tc):
    t, d = h.shape
    n = cw.shape[1]
    tm = min(tm, seq)
    halo = 16
    tiles_per_seq = seq // tm
    hb = tm // halo
    last_hb = t // halo - 1
    return pl.pallas_call(
        functools.partial(_ffn_act_kernel, tiles_per_seq=tiles_per_seq, halo=halo),
        grid=(t // tm, n // tc),
        in_specs=[pl.BlockSpec((tm, d), lambda i, j: (i, 0)),
                  pl.BlockSpec((halo, d), lambda i, j: (jnp.maximum(i * hb - 1, 0), 0)),
                  pl.BlockSpec((halo, d), lambda i, j: (jnp.minimum((i + 1) * hb, last_hb), 0)),
                  pl.BlockSpec((d, 2 * tc), lambda i, j: (0, j)),
                  pl.BlockSpec((3, tc), lambda i, j: (0, j)),
                  pl.BlockSpec((1, tc), lambda i, j: (0, j))],
        out_specs=pl.BlockSpec((tm, tc), lambda i, j: (i, j)),
        out_shape=jax.ShapeDtypeStruct((t, n), BF16),
        scratch_shapes=[pltpu.VMEM((tm + 2 * halo, d), BF16)],
        compiler_params=_params(("parallel", "arbitrary")),
        name="ffn_gate_up_act",
    )(h, h, h, w_gu, cw, cb.reshape(1, n))


def _pack_weights(p, l):
    s0 = FOURIER_DIM
    s1 = s0 + Q_LORA
    s2 = s1 + KV_LORA
    s3 = s2 + QK_ROPE
    s4 = s3 + MEM_DIM
    w_in = p["w_in"][l]
    d = w_in.shape[0]
    bf = lambda a: a.astype(BF16)
    w_gq = bf(jnp.concatenate([w_in[:, s4:], w_in[:, s3:s4]], axis=1))
    w_f = bf(w_in[:, :s0])
    w_lat = bf(jnp.concatenate([w_in[:, s0:s3], jnp.zeros((d, KPE_PAD - QK_ROPE), F32)],
                               axis=1))
    wq = p["w_uq"][l].reshape(Q_LORA, MLA_HEADS, QK_NOPE + QK_ROPE)
    wq = jnp.pad(wq, ((0, 0), (0, 0), (0, QK_PAD - QK_NOPE - QK_ROPE)))
    wq = bf(wq.reshape(Q_LORA, MLA_HEADS * QK_PAD))
    wkv = p["w_ukv"][l].reshape(KV_LORA, MLA_HEADS, QK_NOPE + V_HEAD)
    wk = bf(wkv[:, :, :QK_NOPE].reshape(KV_LORA, MLA_HEADS * QK_NOPE))
    wvt = bf(wkv[:, :, QK_NOPE:].reshape(KV_LORA, MLA_HEADS * V_HEAD).T)
    d_ff = p["w_ffn_gate"].shape[2]
    w_gu = bf(jnp.stack([p["w_ffn_gate"][l].reshape(d, d_ff // FF_CHUNK, FF_CHUNK),
                         p["w_ffn_up"][l].reshape(d, d_ff // FF_CHUNK, FF_CHUNK)],
                        axis=2).reshape(d, 2 * d_ff))
    return dict(
        w_gq=w_gq, w_f=w_f, w_lat=w_lat, wq=wq, wk=wk, wvt=wvt, w_gu=w_gu,
        w_mem_kv=bf(p["w_mem_kv"][l]), w_fo=bf(p["w_fourier_out"][l]),
        w_ao=bf(p["w_attn_out"][l]), w_mo=bf(p["w_mem_out"][l]), w_o=bf(p["w_o"][l]),
        w_down=bf(p["w_ffn_down"][l]), conv_w=p["ffn_conv_w"][l], conv_b=p["ffn_conv_b"][l],
        q_norm=p["q_norm"][l], kv_norm=p["kv_norm"][l], mem_norm=p["mem_norm"][l],
        pre_mix=p["pre_mix_norm"][l], post_mix=p["post_mix_norm"][l],
        pre_ffn=p["pre_ffn_norm"][l], post_ffn=p["post_ffn_norm"][l])


def _rope_tables(seq):
    inv_freq = 1.0 / (ROPE_THETA ** (jnp.arange(0, QK_ROPE, 2, dtype=F32) / QK_ROPE))
    ang = jnp.arange(seq, dtype=F32)[:, None] * inv_freq[None, :]
    cos, sin = jnp.cos(ang), jnp.sin(ang)
    half = QK_ROPE // 2
    zh = jnp.zeros((seq, half), F32)
    zr = jnp.zeros((seq, LANES - QK_ROPE), F32)
    c = jnp.concatenate([cos, cos, zr], axis=1)
    s1 = jnp.concatenate([-sin, zh, zr], axis=1)
    s2 = jnp.concatenate([zh, sin, zr], axis=1)
    return c, s1, s2


def _trunk(x, mem, layers):
    b, s, d = x.shape
    t = b * s
    n_mem = mem.shape[1]
    tm = min(512, s)
    tabs = _rope_tables(s)
    q_scale = (QK_NOPE + QK_ROPE) ** -0.5 * LOG2E
    gate_w = 3 * D_MODEL
    xf = x.reshape(t, d)
    memf = mem.reshape(b * n_mem, d)
    h = _norm(xf, layers[0]["pre_mix"], tm)
    for l, w in enumerate(layers):
        z1 = _mm(h, w["w_gq"], 1024, 1024, "in_proj_gates")
        f_in = _mm(h, w["w_f"], 1024, 1024, "in_proj_fourier")
        q, kc, vt = _attn_prep(h, w, tabs, b, s, 256, q_scale)
        oa = _mla(q.reshape(b, s, -1), kc.reshape(b, s, -1), vt, 2048, 1024, 256)
        mem_n = _norm(memf, w["mem_norm"], n_mem)
        kvm = _mm(mem_n, w["w_mem_kv"], 1024, 1024, "mem_kv")
        om = _mem_attn(z1.reshape(b, s, -1), kvm.reshape(b, n_mem, -1),
                       gate_w // MEM_DIM, 1024)
        yf = _fourier(f_in.reshape(b, s, -1), 16, 16)
        merged = _merge(yf.reshape(t, -1), oa.reshape(t, -1), om.reshape(t, -1), z1,
                        w["w_fo"], w["w_ao"], w["w_mo"], 1024, 512)
        x1, h2 = _proj_norm(merged, w["w_o"], xf, w["post_mix"], w["pre_ffn"], 256, True,
                            "out_proj_norm")
        act = _ffn_act(h2, w["w_gu"], w["conv_w"], w["conv_b"], s, 1024, FF_CHUNK)
        last = l == len(layers) - 1
        nxt = layers[0 if last else l + 1]["pre_mix"]
        xf, h = _proj_norm(act, w["w_down"], x1, w["post_ffn"], nxt, 256, not last,
                           "ffn_down_norm")
    return xf.reshape(b, s, d)


def kernel(x_prompt, x_sample, mem_prompt, mem_sample, pre_mix_norm, w_in, q_norm, w_uq,
           kv_norm, w_ukv, mem_norm, w_mem_kv, w_fourier_out, w_attn_out, w_mem_out, w_o,
           post_mix_norm, pre_ffn_norm, w_ffn_gate, w_ffn_up, ffn_conv_w, ffn_conv_b,
           w_ffn_down, post_ffn_norm):
    p = dict(pre_mix_norm=pre_mix_norm, w_in=w_in, q_norm=q_norm, w_uq=w_uq,
             kv_norm=kv_norm, w_ukv=w_ukv, mem_norm=mem_norm, w_mem_kv=w_mem_kv,
             w_fourier_out=w_fourier_out, w_attn_out=w_attn_out, w_mem_out=w_mem_out,
             w_o=w_o, post_mix_norm=post_mix_norm, pre_ffn_norm=pre_ffn_norm,
             w_ffn_gate=w_ffn_gate, w_ffn_up=w_ffn_up, ffn_conv_w=ffn_conv_w,
             ffn_conv_b=ffn_conv_b, w_ffn_down=w_ffn_down, post_ffn_norm=post_ffn_norm)
    layers = [_pack_weights(p, l) for l in range(w_in.shape[0])]
    return (_trunk(x_prompt, mem_prompt, layers), _trunk(x_sample, mem_sample, layers))
```

```python
import functools
import math

import numpy as np
import jax
import jax.numpy as jnp
from jax import lax
from jax.experimental import pallas as pl
from jax.experimental.pallas import tpu as pltpu

D_MODEL = 2048
N_GROUPS = 4
GROUP_DIM = 256
FOURIER_DIM = N_GROUPS * GROUP_DIM
MLA_HEADS = 16
Q_LORA = 512
KV_LORA = 512
QK_NOPE = 128
QK_ROPE = 64
V_HEAD = 128
ROPE_THETA = 10000.0
MEM_HEADS = 4
MEM_HEAD_DIM = 256
MEM_DIM = MEM_HEADS * MEM_HEAD_DIM
D_FF = 5632
EPS = 1e-6

LANES = 128
QK_PAD = 256
KPE_PAD = LANES
V_ROWS = V_HEAD + 16
N_SLOTS = 3
FFT_N2 = 128
VMEM_LIMIT = 56 * 1024 * 1024
FF_CHUNK = 512
LOG2E = 1.4426950408889634

BF16 = jnp.bfloat16
F32 = jnp.float32


def _params(sem):
    return pltpu.CompilerParams(dimension_semantics=sem, vmem_limit_bytes=VMEM_LIMIT)


def _rms(x, g):
    ms = jnp.mean(x * x, axis=-1, keepdims=True)
    return x * lax.rsqrt(ms + EPS) * g


def _rope128(x, c, s1, s2):
    return x * c + pltpu.roll(x, 96, 1) * s1 + pltpu.roll(x, 32, 1) * s2


def _norm_kernel(x_ref, g_ref, o_ref):
    o_ref[...] = _rms(x_ref[...].astype(F32), g_ref[...]).astype(o_ref.dtype)


def _norm(x, g, tm):
    t, d = x.shape
    return pl.pallas_call(
        _norm_kernel,
        grid=(t // tm,),
        in_specs=[pl.BlockSpec((tm, d), lambda i: (i, 0)),
                  pl.BlockSpec((1, d), lambda i: (0, 0))],
        out_specs=pl.BlockSpec((tm, d), lambda i: (i, 0)),
        out_shape=jax.ShapeDtypeStruct((t, d), BF16),
        compiler_params=_params(("parallel",)),
        name="rmsnorm",
    )(x, g.reshape(1, d))


def _mm_kernel(a_ref, w_ref, o_ref):
    o_ref[...] = jnp.dot(a_ref[...], w_ref[...],
                         preferred_element_type=F32).astype(o_ref.dtype)


def _mm(a, w, tm, tn, name):
    m, k = a.shape
    n = w.shape[1]
    tm = min(tm, m)
    tn = min(tn, n)
    return pl.pallas_call(
        _mm_kernel,
        grid=(m // tm, n // tn),
        in_specs=[pl.BlockSpec((tm, k), lambda i, j: (i, 0)),
                  pl.BlockSpec((k, tn), lambda i, j: (0, j))],
        out_specs=pl.BlockSpec((tm, tn), lambda i, j: (i, j)),
        out_shape=jax.ShapeDtypeStruct((m, n), BF16),
        compiler_params=_params(("parallel", "arbitrary")),
        name=name,
    )(a, w)


def _attn_prep_kernel(h_ref, wl_ref, qg_ref, kvg_ref, wq_ref, wk_ref, wvt_ref,
                      c_ref, s1_ref, s2_ref, q_ref, k_ref, vt_ref, *, scale):
    c, s1, s2 = c_ref[...], s1_ref[...], s2_ref[...]
    z = jnp.dot(h_ref[...], wl_ref[...], preferred_element_type=F32)
    cq = _rms(z[:, :Q_LORA], qg_ref[...]).astype(BF16)
    lat = _rms(z[:, Q_LORA:Q_LORA + KV_LORA], kvg_ref[...]).astype(BF16)
    kpe = _rope128(z[:, Q_LORA + KV_LORA:], c, s1, s2).astype(BF16)
    zq = jnp.dot(cq, wq_ref[...], preferred_element_type=F32) * scale
    kn = jnp.dot(lat, wk_ref[...], preferred_element_type=F32).astype(BF16)
    vt = lax.dot_general(wvt_ref[...], lat, (((1,), (1,)), ((), ())),
                         preferred_element_type=F32).astype(BF16)
    pad_rows = V_ROWS - V_HEAD
    row = lax.broadcasted_iota(jnp.int32, (pad_rows, vt.shape[1]), 0)
    ones_rows = jnp.where(row == 0, 1.0, 0.0).astype(BF16)
    for h in range(MLA_HEADS):
        lo = h * QK_PAD
        q_ref[:, lo:lo + QK_NOPE] = zq[:, lo:lo + QK_NOPE].astype(BF16)
        pe = _rope128(zq[:, lo + QK_NOPE:lo + QK_PAD], c, s1, s2)
        q_ref[:, lo + QK_NOPE:lo + QK_PAD] = pe.astype(BF16)
        k_ref[:, lo:lo + QK_NOPE] = kn[:, h * QK_NOPE:(h + 1) * QK_NOPE]
        k_ref[:, lo + QK_NOPE:lo + QK_PAD] = kpe
        vt_ref[h * V_ROWS:h * V_ROWS + V_HEAD, :] = vt[h * V_HEAD:(h + 1) * V_HEAD, :]
        vt_ref[h * V_ROWS + V_HEAD:(h + 1) * V_ROWS, :] = ones_rows


def _attn_prep(h, w, tabs, b, seq, tm, scale):
    d = h.shape[1]
    tm = min(tm, seq)
    nblk = seq // tm
    nq = MLA_HEADS * QK_PAD
    nv = MLA_HEADS * V_ROWS
    row = lambda bi, i: (bi * nblk + i, 0)
    fix = lambda bi, i: (0, 0)
    tab = pl.BlockSpec((tm, LANES), lambda bi, i: (i, 0))
    def wspec(arr):
        return pl.BlockSpec(arr.shape, fix, pipeline_mode=pl.Buffered(1))
    return pl.pallas_call(
        functools.partial(_attn_prep_kernel, scale=scale),
        grid=(b, nblk),
        in_specs=[pl.BlockSpec((tm, d), row), wspec(w["w_lat"]),
                  pl.BlockSpec((1, Q_LORA), fix), pl.BlockSpec((1, KV_LORA), fix),
                  wspec(w["wq"]), wspec(w["wk"]), wspec(w["wvt"]), tab, tab, tab],
        out_specs=[pl.BlockSpec((tm, nq), row), pl.BlockSpec((tm, nq), row),
                   pl.BlockSpec((None, nv, tm), lambda bi, i: (bi, 0, i))],
        out_shape=[jax.ShapeDtypeStruct((b * seq, nq), BF16),
                   jax.ShapeDtypeStruct((b * seq, nq), BF16),
                   jax.ShapeDtypeStruct((b, nv, seq), BF16)],
        compiler_params=_params(("parallel", "arbitrary")),
        name="attn_prep",
    )(h, w["w_lat"], w["q_norm"].reshape(1, -1), w["kv_norm"].reshape(1, -1),
      w["wq"], w["wk"], w["wvt"], *tabs)


def _mla_kernel(q_ref, k_ref, vt_ref, o_ref, s_ref, p_ref, qt_ref, *, tq, tk):
    n_q = q_ref.shape[0] // tq
    n_k = k_ref.shape[0] // tk

    def q_tile(qi, carry):
        q0 = pl.multiple_of(qi * tq, tq)
        qt_ref[...] = q_ref[pl.ds(q0, tq), :].T

        def scores(c, slot):
            k = k_ref[pl.ds(pl.multiple_of(c * tk, tk), tk), :]
            s = jnp.dot(k, qt_ref[...], preferred_element_type=F32)
            s_ref[slot] = s
            return jnp.max(s, axis=0, keepdims=True)

        def softmax(slot, cmax, m):
            m_new = jnp.maximum(m, cmax)
            p_ref[slot] = jnp.exp2((s_ref[slot] - m_new).astype(BF16))
            return m_new, jnp.exp2(m - m_new)

        def pv(c, slot):
            vt = vt_ref[:, pl.ds(pl.multiple_of(c * tk, tk), tk)]
            return jnp.dot(vt, p_ref[slot], preferred_element_type=F32)

        def stage(c, u, m, acc, cmax):
            part = pv(c - 1, (u - 1) % N_SLOTS)
            cmax_next = scores(c + 1, (u + 1) % N_SLOTS)
            m, alpha = softmax(u % N_SLOTS, cmax, m)
            return m, (acc + part) * alpha, cmax_next

        cmax = scores(0, 0)
        m, _ = softmax(0, cmax, jnp.full((1, tq), -1e30, F32))
        cmax = scores(1, 1)
        acc = jnp.zeros((V_ROWS, tq), F32)

        n_mid = n_k - 2
        unroll = 4 * N_SLOTS
        peel = n_mid % unroll
        st = (m, acc, cmax)
        for u in range(1, peel + 1):
            st = stage(u, u, *st)

        def group(j, st):
            for u in range(1, unroll + 1):
                st = stage(peel + unroll * j + u, peel + u, *st)
            return st

        trips = n_mid // unroll
        if trips == 1:
            st = group(0, st)
        elif trips > 1:
            st = lax.fori_loop(0, trips, group, st)
        m, acc, cmax = st
        part = pv(n_k - 2, (n_k - 2) % N_SLOTS)
        m, alpha = softmax((n_k - 1) % N_SLOTS, cmax, m)
        acc = (acc + part) * alpha + pv(n_k - 1, (n_k - 1) % N_SLOTS)
        out = acc[:V_HEAD] * (1.0 / acc[V_HEAD:V_HEAD + 1])
        o_ref[pl.ds(q0, tq), :] = out.T.astype(o_ref.dtype)
        return carry

    lax.fori_loop(0, n_q, q_tile, 0)


def _mla(q, k, vt, tq_outer, tq, tk):
    b, s, _ = q.shape
    tq_outer = min(tq_outer, s)
    tq = min(tq, tq_outer)
    tk = min(tk, s // 2)
    assert (s // tk) % 2 == 0
    return pl.pallas_call(
        functools.partial(_mla_kernel, tq=tq, tk=tk),
        grid=(b, MLA_HEADS, s // tq_outer),
        in_specs=[pl.BlockSpec((None, tq_outer, QK_PAD), lambda bi, h, qi: (bi, qi, h)),
                  pl.BlockSpec((None, s, QK_PAD), lambda bi, h, qi: (bi, 0, h)),
                  pl.BlockSpec((None, V_ROWS, s), lambda bi, h, qi: (bi, h, 0))],
        out_specs=pl.BlockSpec((None, tq_outer, V_HEAD), lambda bi, h, qi: (bi, qi, h)),
        out_shape=jax.ShapeDtypeStruct((b, s, MLA_HEADS * V_HEAD), BF16),
        scratch_shapes=[pltpu.VMEM((N_SLOTS, tk, tq), F32),
                        pltpu.VMEM((N_SLOTS, tk, tq), BF16),
                        pltpu.VMEM((QK_PAD, tq), BF16)],
        compiler_params=_params(("parallel", "parallel", "arbitrary")),
        name="mla_attention",
    )(q, k, vt)


def _mem_kernel(q_ref, kv_ref, o_ref, *, scale):
    for h in range(MEM_HEADS):
        lo = h * MEM_HEAD_DIM
        q = q_ref[:, lo:lo + MEM_HEAD_DIM]
        k = kv_ref[:, lo:lo + MEM_HEAD_DIM]
        v = kv_ref[:, MEM_DIM + lo:MEM_DIM + lo + MEM_HEAD_DIM]
        s = lax.dot_general(q, k, (((1,), (1,)), ((), ())),
                            preferred_element_type=F32) * scale
        p = jnp.exp2(s - jnp.max(s, axis=-1, keepdims=True))
        l = jnp.sum(p, axis=-1, keepdims=True)
        o = jnp.dot(p.astype(BF16), v, preferred_element_type=F32)
        o_ref[:, lo:lo + MEM_HEAD_DIM] = (o * (1.0 / l)).astype(o_ref.dtype)


def _mem_attn(z1, kvm, q_blk, tq):
    b, s, _ = z1.shape
    n_mem = kvm.shape[1]
    tq = min(tq, s)
    scale = MEM_HEAD_DIM ** -0.5 * LOG2E
    return pl.pallas_call(
        functools.partial(_mem_kernel, scale=scale),
        grid=(b, s // tq),
        in_specs=[pl.BlockSpec((None, tq, MEM_DIM), lambda bi, i: (bi, i, q_blk)),
                  pl.BlockSpec((None, n_mem, 2 * MEM_DIM), lambda bi, i: (bi, 0, 0))],
        out_specs=pl.BlockSpec((None, tq, MEM_DIM), lambda bi, i: (bi, i, 0)),
        out_shape=jax.ShapeDtypeStruct((b, s, MEM_DIM), BF16),
        compiler_params=_params(("parallel", "arbitrary")),
        name="mem_attention",
    )(z1, kvm)


def _fft_a_kernel(u_ref, f_ref, tc_ref, ts_ref, o_ref, *, n1):
    f = f_ref[...]
    for j in range(u_ref.shape[0]):
        r = jnp.dot(f, u_ref[j], preferred_element_type=F32)
        br, bi = r[:n1], r[n1:]
        tc, ts = tc_ref[j], ts_ref[j]
        o_ref[j, 0] = (br * tc + bi * ts).astype(o_ref.dtype)
        o_ref[j, 1] = (bi * tc - br * ts).astype(o_ref.dtype)


def _fft_b_kernel(b_ref, m_ref, cs_ref, o_ref):
    m2 = m_ref[...]
    cs = cs_ref[...]
    n2 = o_ref.shape[1]
    for j in range(b_ref.shape[0]):
        x = b_ref[j].reshape(2 * n2, FOURIER_DIM)
        z = jnp.dot(m2, x, preferred_element_type=F32).astype(BF16)
        for g in range(N_GROUPS):
            lo = g * GROUP_DIM
            zz = jnp.concatenate([z[:n2, lo:lo + GROUP_DIM], z[n2:, lo:lo + GROUP_DIM]],
                                 axis=1)
            y = jnp.dot(zz, cs, preferred_element_type=F32)
            o_ref[j, :, lo:lo + GROUP_DIM] = y.astype(o_ref.dtype)


def _dft_tables(seq):
    n2 = FFT_N2
    n1 = seq // n2
    a1 = 2.0 * np.pi * np.outer(np.arange(n1), np.arange(n1)) / n1
    f1 = np.concatenate([np.cos(a1), -np.sin(a1)], axis=0) / math.sqrt(n1)
    th = 2.0 * np.pi * np.outer(np.arange(n2), np.arange(n1)) / seq
    a2 = 2.0 * np.pi * np.outer(np.arange(n2), np.arange(n2)) / n2
    c2, s2 = np.cos(a2), np.sin(a2)
    m2 = np.block([[c2, s2], [-s2, c2]]) / math.sqrt(n2)
    ac = 2.0 * np.pi * np.outer(np.arange(GROUP_DIM), np.arange(GROUP_DIM)) / GROUP_DIM
    cs = np.concatenate([np.cos(ac), np.sin(ac)], axis=0) / math.sqrt(GROUP_DIM)
    return (jnp.asarray(f1, BF16), jnp.asarray(np.cos(th)[:, :, None], F32),
            jnp.asarray(np.sin(th)[:, :, None], F32), jnp.asarray(m2, BF16),
            jnp.asarray(cs, BF16))


def _fourier(u, t2, tk1):
    b, s, c = u.shape
    n2 = FFT_N2
    n1 = s // n2
    t2 = min(t2, n2)
    tk1 = min(tk1, n1)
    f1, tc, ts, m2, cs = _dft_tables(s)
    ut = u.reshape(b, n1, n2, c).transpose(0, 2, 1, 3)
    bt = pl.pallas_call(
        functools.partial(_fft_a_kernel, n1=n1),
        grid=(b, n2 // t2),
        in_specs=[pl.BlockSpec((None, t2, n1, c), lambda bi, i: (bi, i, 0, 0)),
                  pl.BlockSpec((2 * n1, n1), lambda bi, i: (0, 0)),
                  pl.BlockSpec((t2, n1, 1), lambda bi, i: (i, 0, 0)),
                  pl.BlockSpec((t2, n1, 1), lambda bi, i: (i, 0, 0))],
        out_specs=pl.BlockSpec((None, t2, 2, n1, c), lambda bi, i: (bi, i, 0, 0, 0)),
        out_shape=jax.ShapeDtypeStruct((b, n2, 2, n1, c), BF16),
        compiler_params=_params(("parallel", "arbitrary")),
        name="fft_stage_a",
    )(ut, f1, tc, ts)
    bk = bt.transpose(0, 3, 2, 1, 4)
    y = pl.pallas_call(
        _fft_b_kernel,
        grid=(b, n1 // tk1),
        in_specs=[pl.BlockSpec((None, tk1, 2, n2, c), lambda bi, i: (bi, i, 0, 0, 0)),
                  pl.BlockSpec((2 * n2, 2 * n2), lambda bi, i: (0, 0)),
                  pl.BlockSpec((2 * GROUP_DIM, GROUP_DIM), lambda bi, i: (0, 0))],
        out_specs=pl.BlockSpec((None, tk1, n2, c), lambda bi, i: (bi, i, 0, 0)),
        out_shape=jax.ShapeDtypeStruct((b, n1, n2, c), BF16),
        compiler_params=_params(("parallel", "arbitrary")),
        name="fft_stage_b",
    )(bk, m2, cs)
    return y.transpose(0, 2, 1, 3).reshape(b, s, c)


def _merge_kernel(yf_ref, oa_ref, om_ref, g0_ref, g1_ref, g2_ref,
                  wf_ref, wa_ref, wm_ref, o_ref):
    def gate(g_ref):
        return jax.nn.sigmoid(g_ref[...].astype(F32))
    yf = jnp.dot(yf_ref[...], wf_ref[...], preferred_element_type=F32)
    ya = jnp.dot(oa_ref[...], wa_ref[...], preferred_element_type=F32)
    ym = jnp.dot(om_ref[...], wm_ref[...], preferred_element_type=F32)
    o_ref[...] = (gate(g0_ref) * yf + gate(g1_ref) * ya + gate(g2_ref) * ym
                  ).astype(o_ref.dtype)


def _merge(yf, oa, om, z1, wf, wa, wm, tm, tn):
    t = yf.shape[0]
    tm = min(tm, t)
    nb = D_MODEL // tn
    def gspec(k):
        return pl.BlockSpec((tm, tn), lambda i, j: (i, k * nb + j))
    def aspec(w):
        return pl.BlockSpec((tm, w), lambda i, j: (i, 0))
    def wspec(w):
        return pl.BlockSpec((w, tn), lambda i, j: (0, j))
    return pl.pallas_call(
        _merge_kernel,
        grid=(t // tm, nb),
        in_specs=[aspec(FOURIER_DIM), aspec(MLA_HEADS * V_HEAD), aspec(MEM_DIM),
                  gspec(0), gspec(1), gspec(2),
                  wspec(FOURIER_DIM), wspec(MLA_HEADS * V_HEAD), wspec(MEM_DIM)],
        out_specs=pl.BlockSpec((tm, tn), lambda i, j: (i, j)),
        out_shape=jax.ShapeDtypeStruct((t, D_MODEL), BF16),
        compiler_params=_params(("parallel", "arbitrary")),
        name="branch_merge",
    )(yf, oa, om, z1, z1, z1, wf, wa, wm)


def _proj_norm_kernel(a_ref, w_ref, x_ref, gpost_ref, gpre_ref, x1_ref, *h_ref, ts):
    for r0 in range(0, a_ref.shape[0], ts):
        rows = slice(r0, r0 + ts)
        y = jnp.dot(a_ref[rows, :], w_ref[...], preferred_element_type=F32)
        x1 = x_ref[rows, :] + _rms(y, gpost_ref[...])
        x1_ref[rows, :] = x1
        if h_ref:
            h_ref[0][rows, :] = _rms(x1, gpre_ref[...]).astype(h_ref[0].dtype)


def _proj_norm(a, w, x, gpost, gpre, tm, ts, emit_h, name):
    t, d = x.shape
    k = a.shape[1]
    tm = min(tm, t)
    row = lambda i: (i, 0)
    fix = lambda i: (0, 0)
    out_specs = [pl.BlockSpec((tm, d), row)]
    out_shape = [jax.ShapeDtypeStruct((t, d), F32)]
    if emit_h:
        out_specs.append(pl.BlockSpec((tm, d), row))
        out_shape.append(jax.ShapeDtypeStruct((t, d), BF16))
    outs = pl.pallas_call(
        functools.partial(_proj_norm_kernel, ts=min(ts, tm)),
        grid=(t // tm,),
        in_specs=[pl.BlockSpec((tm, k), row),
                  pl.BlockSpec((k, d), fix, pipeline_mode=pl.Buffered(1)),
                  pl.BlockSpec((tm, d), row), pl.BlockSpec((1, d), fix),
                  pl.BlockSpec((1, d), fix)],
        out_specs=out_specs,
        out_shape=out_shape,
        compiler_params=_params(("parallel",)),
        name=name,
    )(a, w, x, gpost.reshape(1, d), gpre.reshape(1, d))
    return outs if emit_h else (outs[0], None)


def _gelu_tanh(x):
    return 0.5 * x * (1.0 + jnp.tanh(0.7978845608028654 * (x + 0.044715 * x * x * x)))


def _ffn_act_kernel(a_ref, ap_ref, an_ref, wg_ref, wu_ref, cw_ref, cb_ref, o_ref, ax_ref, *,
                    tiles_per_seq, halo):
    tm = a_ref.shape[0]
    tc = o_ref.shape[1]

    pos = pl.program_id(0) % tiles_per_seq

    @pl.when(pl.program_id(1) == 0)
    def _():
        zero = jnp.zeros_like(ap_ref[...])
        ax_ref[0:halo] = jnp.where(pos == 0, zero, ap_ref[...])
        ax_ref[halo:halo + tm] = a_ref[...]
        ax_ref[halo + tm:] = jnp.where(pos == tiles_per_seq - 1, zero, an_ref[...])

    g = jnp.dot(ax_ref[...], wg_ref[...], preferred_element_type=F32)
    u = jnp.dot(ax_ref[halo:halo + tm], wu_ref[...], preferred_element_type=F32)
    g_prev = pltpu.roll(g, 1, 0)[halo:halo + tm]
    g_next = pltpu.roll(g, tm + 2 * halo - 1, 0)[halo:halo + tm]
    cw = cw_ref[...]
    conv = g_prev * cw[0:1] + g[halo:halo + tm] * cw[1:2] + g_next * cw[2:3] + cb_ref[...]
    o_ref[...] = (_gelu_tanh(conv) * u).astype(o_ref.dtype)


def _ffn_act(h, w_gate, w_up, cw, cb, seq, tm, tc):
    t, d = h.shape
    n = cw.shape[1]
    tm = min(tm, seq)
    halo = 16
    tiles_per_seq = seq // tm
    hb = tm // halo
    last_hb = t // halo - 1
    return pl.pallas_call(
        functools.partial(_ffn_act_kernel, tiles_per_seq=tiles_per_seq, halo=halo),
        grid=(t // tm, n // tc),
        in_specs=[pl.BlockSpec((tm, d), lambda i, j: (i, 0)),
                  pl.BlockSpec((halo, d), lambda i, j: (jnp.maximum(i * hb - 1, 0), 0)),
                  pl.BlockSpec((halo, d), lambda i, j: (jnp.minimum((i + 1) * hb, last_hb), 0)),
                  pl.BlockSpec((d, tc), lambda i, j: (0, j)),
                  pl.BlockSpec((d, tc), lambda i, j: (0, j)),
                  pl.BlockSpec((3, tc), lambda i, j: (0, j)),
                  pl.BlockSpec((1, tc), lambda i, j: (0, j))],
        out_specs=pl.BlockSpec((tm, tc), lambda i, j: (i, j)),
        out_shape=jax.ShapeDtypeStruct((t, n), BF16),
        scratch_shapes=[pltpu.VMEM((tm + 2 * halo, d), BF16)],
        compiler_params=_params(("parallel", "arbitrary")),
        name="ffn_gate_up_act",
    )(h, h, h, w_gate, w_up, cw, cb.reshape(1, n))


def _pack_weights(p, l):
    s0 = FOURIER_DIM
    s1 = s0 + Q_LORA
    s2 = s1 + KV_LORA
    s3 = s2 + QK_ROPE
    s4 = s3 + MEM_DIM
    w_in = p["w_in"][l]
    d = w_in.shape[0]
    bf = lambda a: a.astype(BF16)
    w_gq = bf(jnp.concatenate([w_in[:, s4:], w_in[:, s3:s4]], axis=1))
    w_f = bf(w_in[:, :s0])
    w_lat = bf(jnp.concatenate([w_in[:, s0:s3], jnp.zeros((d, KPE_PAD - QK_ROPE), F32)],
                               axis=1))
    wq = p["w_uq"][l].reshape(Q_LORA, MLA_HEADS, QK_NOPE + QK_ROPE)
    wq = jnp.pad(wq, ((0, 0), (0, 0), (0, QK_PAD - QK_NOPE - QK_ROPE)))
    wq = bf(wq.reshape(Q_LORA, MLA_HEADS * QK_PAD))
    wkv = p["w_ukv"][l].reshape(KV_LORA, MLA_HEADS, QK_NOPE + V_HEAD)
    wk = bf(wkv[:, :, :QK_NOPE].reshape(KV_LORA, MLA_HEADS * QK_NOPE))
    wvt = bf(wkv[:, :, QK_NOPE:].reshape(KV_LORA, MLA_HEADS * V_HEAD).T)
    return dict(
        w_gq=w_gq, w_f=w_f, w_lat=w_lat, wq=wq, wk=wk, wvt=wvt,
        w_gate=bf(p["w_ffn_gate"][l]), w_up=bf(p["w_ffn_up"][l]),
        w_mem_kv=bf(p["w_mem_kv"][l]), w_fo=bf(p["w_fourier_out"][l]),
        w_ao=bf(p["w_attn_out"][l]), w_mo=bf(p["w_mem_out"][l]), w_o=bf(p["w_o"][l]),
        w_down=bf(p["w_ffn_down"][l]), conv_w=p["ffn_conv_w"][l], conv_b=p["ffn_conv_b"][l],
        q_norm=p["q_norm"][l], kv_norm=p["kv_norm"][l], mem_norm=p["mem_norm"][l],
        pre_mix=p["pre_mix_norm"][l], post_mix=p["post_mix_norm"][l],
        pre_ffn=p["pre_ffn_norm"][l], post_ffn=p["post_ffn_norm"][l])


def _rope_tables(seq):
    inv_freq = 1.0 / (ROPE_THETA ** (jnp.arange(0, QK_ROPE, 2, dtype=F32) / QK_ROPE))
    ang = jnp.arange(seq, dtype=F32)[:, None] * inv_freq[None, :]
    cos, sin = jnp.cos(ang), jnp.sin(ang)
    half = QK_ROPE // 2
    zh = jnp.zeros((seq, half), F32)
    zr = jnp.zeros((seq, LANES - QK_ROPE), F32)
    c = jnp.concatenate([cos, cos, zr], axis=1)
    s1 = jnp.concatenate([-sin, zh, zr], axis=1)
    s2 = jnp.concatenate([zh, sin, zr], axis=1)
    return c, s1, s2


def _trunk(x, mem, layers):
    b, s, d = x.shape
    t = b * s
    n_mem = mem.shape[1]
    tm = min(512, s)
    tabs = _rope_tables(s)
    q_scale = (QK_NOPE + QK_ROPE) ** -0.5 * LOG2E
    gate_w = 3 * D_MODEL
    xf = x.reshape(t, d)
    memf = mem.reshape(b * n_mem, d)
    h = _norm(xf, layers[0]["pre_mix"], tm)
    for l, w in enumerate(layers):
        z1 = _mm(h, w["w_gq"], 1024, 1024, "in_proj_gates")
        f_in = _mm(h, w["w_f"], 1024, 1024, "in_proj_fourier")
        q, kc, vt = _attn_prep(h, w, tabs, b, s, 256, q_scale)
        oa = _mla(q.reshape(b, s, -1), kc.reshape(b, s, -1), vt, 2048, 1024,
                  512 if s > 2048 else 256)
        mem_n = _norm(memf, w["mem_norm"], n_mem)
        kvm = _mm(mem_n, w["w_mem_kv"], 1024, 1024, "mem_kv")
        om = _mem_attn(z1.reshape(b, s, -1), kvm.reshape(b, n_mem, -1),
                       gate_w // MEM_DIM, 1024)
        yf = _fourier(f_in.reshape(b, s, -1), 16, 16)
        merged = _merge(yf.reshape(t, -1), oa.reshape(t, -1), om.reshape(t, -1), z1,
                        w["w_fo"], w["w_ao"], w["w_mo"], 1024, 512)
        x1, h2 = _proj_norm(merged, w["w_o"], xf, w["post_mix"], w["pre_ffn"], 512, 128,
                            True, "out_proj_norm")
        act = _ffn_act(h2, w["w_gate"], w["w_up"], w["conv_w"], w["conv_b"], s, 1024,
                       FF_CHUNK)
        last = l == len(layers) - 1
        nxt = layers[0 if last else l + 1]["pre_mix"]
        xf, h = _proj_norm(act, w["w_down"], x1, w["post_ffn"], nxt, 256, 128, not last,
                           "ffn_down_norm")
    return xf.reshape(b, s, d)


def kernel(x_prompt, x_sample, mem_prompt, mem_sample, pre_mix_norm, w_in, q_norm, w_uq,
           kv_norm, w_ukv, mem_norm, w_mem_kv, w_fourier_out, w_attn_out, w_mem_out, w_o,
           post_mix_norm, pre_ffn_norm, w_ffn_gate, w_ffn_up, ffn_conv_w, ffn_conv_b,
           w_ffn_down, post_ffn_norm):
    p = dict(pre_mix_norm=pre_mix_norm, w_in=w_in, q_norm=q_norm, w_uq=w_uq,
             kv_norm=kv_norm, w_ukv=w_ukv, mem_norm=mem_norm, w_mem_kv=w_mem_kv,
             w_fourier_out=w_fourier_out, w_attn_out=w_attn_out, w_mem_out=w_mem_out,
             w_o=w_o, post_mix_norm=post_mix_norm, pre_ffn_norm=pre_ffn_norm,
             w_ffn_gate=w_ffn_gate, w_ffn_up=w_ffn_up, ffn_conv_w=ffn_conv_w,
             ffn_conv_b=ffn_conv_b, w_ffn_down=w_ffn_down, post_ffn_norm=post_ffn_norm)
    layers = [_pack_weights(p, l) for l in range(w_in.shape[0])]
    return (_trunk(x_prompt, mem_prompt, layers), _trunk(x_sample, mem_sample, layers))
```

```python
import functools
import math

import numpy as np
import jax
import jax.numpy as jnp
from jax import lax
from jax.experimental import pallas as pl
from jax.experimental.pallas import tpu as pltpu

D_MODEL = 2048
N_GROUPS = 4
GROUP_DIM = 256
FOURIER_DIM = N_GROUPS * GROUP_DIM
MLA_HEADS = 16
Q_LORA = 512
KV_LORA = 512
QK_NOPE = 128
QK_ROPE = 64
V_HEAD = 128
ROPE_THETA = 10000.0
MEM_HEADS = 4
MEM_HEAD_DIM = 256
MEM_DIM = MEM_HEADS * MEM_HEAD_DIM
D_FF = 5632
EPS = 1e-6

LANES = 128
QK_PAD = 256
KPE_PAD = LANES
V_ROWS = V_HEAD + 16
N_SLOTS = 3
FFT_N2 = 128
VMEM_LIMIT = 56 * 1024 * 1024
FF_CHUNK = 512
LOG2E = 1.4426950408889634

BF16 = jnp.bfloat16
F32 = jnp.float32


def _params(sem):
    return pltpu.CompilerParams(dimension_semantics=sem, vmem_limit_bytes=VMEM_LIMIT)


def _rms(x, g):
    ms = jnp.mean(x * x, axis=-1, keepdims=True)
    return x * lax.rsqrt(ms + EPS) * g


def _rope128(x, c, s1, s2):
    return x * c + pltpu.roll(x, 96, 1) * s1 + pltpu.roll(x, 32, 1) * s2


def _norm_kernel(x_ref, g_ref, o_ref):
    o_ref[...] = _rms(x_ref[...].astype(F32), g_ref[...]).astype(o_ref.dtype)


def _norm(x, g, tm):
    t, d = x.shape
    return pl.pallas_call(
        _norm_kernel,
        grid=(t // tm,),
        in_specs=[pl.BlockSpec((tm, d), lambda i: (i, 0)),
                  pl.BlockSpec((1, d), lambda i: (0, 0))],
        out_specs=pl.BlockSpec((tm, d), lambda i: (i, 0)),
        out_shape=jax.ShapeDtypeStruct((t, d), BF16),
        compiler_params=_params(("parallel",)),
        name="rmsnorm",
    )(x, g.reshape(1, d))


def _mm_kernel(a_ref, w_ref, o_ref):
    o_ref[...] = jnp.dot(a_ref[...], w_ref[...],
                         preferred_element_type=F32).astype(o_ref.dtype)


def _mm(a, w, tm, tn, name):
    m, k = a.shape
    n = w.shape[1]
    tm = min(tm, m)
    tn = min(tn, n)
    return pl.pallas_call(
        _mm_kernel,
        grid=(m // tm, n // tn),
        in_specs=[pl.BlockSpec((tm, k), lambda i, j: (i, 0)),
                  pl.BlockSpec((k, tn), lambda i, j: (0, j))],
        out_specs=pl.BlockSpec((tm, tn), lambda i, j: (i, j)),
        out_shape=jax.ShapeDtypeStruct((m, n), BF16),
        compiler_params=_params(("parallel", "arbitrary")),
        name=name,
    )(a, w)


def _attn_prep_kernel(h_ref, wl_ref, qg_ref, kvg_ref, wq_ref, wk_ref, wvt_ref,
                      c_ref, s1_ref, s2_ref, q_ref, k_ref, vt_ref, *, scale):
    c, s1, s2 = c_ref[...], s1_ref[...], s2_ref[...]
    z = jnp.dot(h_ref[...], wl_ref[...], preferred_element_type=F32)
    cq = _rms(z[:, :Q_LORA], qg_ref[...]).astype(BF16)
    lat = _rms(z[:, Q_LORA:Q_LORA + KV_LORA], kvg_ref[...]).astype(BF16)
    kpe = _rope128(z[:, Q_LORA + KV_LORA:], c, s1, s2).astype(BF16)
    zq = jnp.dot(cq, wq_ref[...], preferred_element_type=F32) * scale
    kn = jnp.dot(lat, wk_ref[...], preferred_element_type=F32).astype(BF16)
    vt = lax.dot_general(wvt_ref[...], lat, (((1,), (1,)), ((), ())),
                         preferred_element_type=F32).astype(BF16)
    pad_rows = V_ROWS - V_HEAD
    row = lax.broadcasted_iota(jnp.int32, (pad_rows, vt.shape[1]), 0)
    ones_rows = jnp.where(row == 0, 1.0, 0.0).astype(BF16)
    for h in range(MLA_HEADS):
        lo = h * QK_PAD
        q_ref[:, lo:lo + QK_NOPE] = zq[:, lo:lo + QK_NOPE].astype(BF16)
        pe = _rope128(zq[:, lo + QK_NOPE:lo + QK_PAD], c, s1, s2)
        q_ref[:, lo + QK_NOPE:lo + QK_PAD] = pe.astype(BF16)
        k_ref[:, lo:lo + QK_NOPE] = kn[:, h * QK_NOPE:(h + 1) * QK_NOPE]
        k_ref[:, lo + QK_NOPE:lo + QK_PAD] = kpe
        vt_ref[h * V_ROWS:h * V_ROWS + V_HEAD, :] = vt[h * V_HEAD:(h + 1) * V_HEAD, :]
        vt_ref[h * V_ROWS + V_HEAD:(h + 1) * V_ROWS, :] = ones_rows


def _attn_prep(h, w, tabs, b, seq, tm, scale):
    d = h.shape[1]
    tm = min(tm, seq)
    nblk = seq // tm
    nq = MLA_HEADS * QK_PAD
    nv = MLA_HEADS * V_ROWS
    row = lambda bi, i: (bi * nblk + i, 0)
    fix = lambda bi, i: (0, 0)
    tab = pl.BlockSpec((tm, LANES), lambda bi, i: (i, 0))
    def wspec(arr):
        return pl.BlockSpec(arr.shape, fix, pipeline_mode=pl.Buffered(1))
    return pl.pallas_call(
        functools.partial(_attn_prep_kernel, scale=scale),
        grid=(b, nblk),
        in_specs=[pl.BlockSpec((tm, d), row), wspec(w["w_lat"]),
                  pl.BlockSpec((1, Q_LORA), fix), pl.BlockSpec((1, KV_LORA), fix),
                  wspec(w["wq"]), wspec(w["wk"]), wspec(w["wvt"]), tab, tab, tab],
        out_specs=[pl.BlockSpec((tm, nq), row), pl.BlockSpec((tm, nq), row),
                   pl.BlockSpec((None, nv, tm), lambda bi, i: (bi, 0, i))],
        out_shape=[jax.ShapeDtypeStruct((b * seq, nq), BF16),
                   jax.ShapeDtypeStruct((b * seq, nq), BF16),
                   jax.ShapeDtypeStruct((b, nv, seq), BF16)],
        compiler_params=_params(("parallel", "arbitrary")),
        name="attn_prep",
    )(h, w["w_lat"], w["q_norm"].reshape(1, -1), w["kv_norm"].reshape(1, -1),
      w["wq"], w["wk"], w["wvt"], *tabs)


def _mla_kernel(q_ref, k_ref, vt_ref, o_ref, s_ref, p_ref, qt_ref, *, tq, tk):
    n_q = q_ref.shape[0] // tq
    n_k = k_ref.shape[0] // tk

    def q_tile(qi, carry):
        q0 = pl.multiple_of(qi * tq, tq)
        qt_ref[...] = q_ref[pl.ds(q0, tq), :].T

        def scores(c, slot):
            k = k_ref[pl.ds(pl.multiple_of(c * tk, tk), tk), :]
            s = jnp.dot(k, qt_ref[...], preferred_element_type=F32)
            s_ref[slot] = s
            return jnp.max(s, axis=0, keepdims=True)

        def softmax(slot, cmax, m):
            m_new = jnp.maximum(m, cmax)
            p_ref[slot] = jnp.exp2((s_ref[slot] - m_new).astype(BF16))
            return m_new, jnp.exp2(m - m_new)

        def pv(c, slot):
            vt = vt_ref[:, pl.ds(pl.multiple_of(c * tk, tk), tk)]
            return jnp.dot(vt, p_ref[slot], preferred_element_type=F32)

        def stage(c, u, m, acc, cmax):
            cmax_next = scores(c + 1, (u + 1) % N_SLOTS)
            m, alpha = softmax(u % N_SLOTS, cmax, m)
            part = pv(c - 1, (u - 1) % N_SLOTS)
            return m, (acc + part) * alpha, cmax_next

        cmax = scores(0, 0)
        m, _ = softmax(0, cmax, jnp.full((1, tq), -1e30, F32))
        cmax = scores(1, 1)
        acc = jnp.zeros((V_ROWS, tq), F32)

        n_mid = n_k - 2
        unroll = 4 * N_SLOTS
        peel = n_mid % unroll
        st = (m, acc, cmax)
        for u in range(1, peel + 1):
            st = stage(u, u, *st)

        def group(j, st):
            for u in range(1, unroll + 1):
                st = stage(peel + unroll * j + u, peel + u, *st)
            return st

        trips = n_mid // unroll
        if trips == 1:
            st = group(0, st)
        elif trips > 1:
            st = lax.fori_loop(0, trips, group, st)
        m, acc, cmax = st
        part = pv(n_k - 2, (n_k - 2) % N_SLOTS)
        m, alpha = softmax((n_k - 1) % N_SLOTS, cmax, m)
        acc = (acc + part) * alpha + pv(n_k - 1, (n_k - 1) % N_SLOTS)
        out = acc[:V_HEAD] * (1.0 / acc[V_HEAD:V_HEAD + 1])
        o_ref[pl.ds(q0, tq), :] = out.T.astype(o_ref.dtype)
        return carry

    lax.fori_loop(0, n_q, q_tile, 0)


def _mla(q, k, vt, tq_outer, tq, tk):
    b, s, _ = q.shape
    tq_outer = min(tq_outer, s)
    tq = min(tq, tq_outer)
    tk = min(tk, s // 2)
    assert (s // tk) % 2 == 0
    return pl.pallas_call(
        functools.partial(_mla_kernel, tq=tq, tk=tk),
        grid=(b, MLA_HEADS, s // tq_outer),
        in_specs=[pl.BlockSpec((None, tq_outer, QK_PAD), lambda bi, h, qi: (bi, qi, h)),
                  pl.BlockSpec((None, s, QK_PAD), lambda bi, h, qi: (bi, 0, h)),
                  pl.BlockSpec((None, V_ROWS, s), lambda bi, h, qi: (bi, h, 0))],
        out_specs=pl.BlockSpec((None, tq_outer, V_HEAD), lambda bi, h, qi: (bi, qi, h)),
        out_shape=jax.ShapeDtypeStruct((b, s, MLA_HEADS * V_HEAD), BF16),
        scratch_shapes=[pltpu.VMEM((N_SLOTS, tk, tq), F32),
                        pltpu.VMEM((N_SLOTS, tk, tq), BF16),
                        pltpu.VMEM((QK_PAD, tq), BF16)],
        compiler_params=_params(("parallel", "parallel", "arbitrary")),
        name="mla_attention",
    )(q, k, vt)


def _mem_kernel(q_ref, kv_ref, o_ref, *, scale):
    for h in range(MEM_HEADS):
        lo = h * MEM_HEAD_DIM
        q = q_ref[:, lo:lo + MEM_HEAD_DIM]
        k = kv_ref[:, lo:lo + MEM_HEAD_DIM]
        v = kv_ref[:, MEM_DIM + lo:MEM_DIM + lo + MEM_HEAD_DIM]
        s = lax.dot_general(q, k, (((1,), (1,)), ((), ())),
                            preferred_element_type=F32) * scale
        p = jnp.exp2(s - jnp.max(s, axis=-1, keepdims=True))
        l = jnp.sum(p, axis=-1, keepdims=True)
        o = jnp.dot(p.astype(BF16), v, preferred_element_type=F32)
        o_ref[:, lo:lo + MEM_HEAD_DIM] = (o * (1.0 / l)).astype(o_ref.dtype)


def _mem_attn(z1, kvm, q_blk, tq):
    b, s, _ = z1.shape
    n_mem = kvm.shape[1]
    tq = min(tq, s)
    scale = MEM_HEAD_DIM ** -0.5 * LOG2E
    return pl.pallas_call(
        functools.partial(_mem_kernel, scale=scale),
        grid=(b, s // tq),
        in_specs=[pl.BlockSpec((None, tq, MEM_DIM), lambda bi, i: (bi, i, q_blk)),
                  pl.BlockSpec((None, n_mem, 2 * MEM_DIM), lambda bi, i: (bi, 0, 0))],
        out_specs=pl.BlockSpec((None, tq, MEM_DIM), lambda bi, i: (bi, i, 0)),
        out_shape=jax.ShapeDtypeStruct((b, s, MEM_DIM), BF16),
        compiler_params=_params(("parallel", "arbitrary")),
        name="mem_attention",
    )(z1, kvm)


def _fft_a_kernel(u_ref, f_ref, tc_ref, ts_ref, o_ref, *, n1):
    f = f_ref[...]
    for j in range(u_ref.shape[0]):
        r = jnp.dot(f, u_ref[j], preferred_element_type=F32)
        br, bi = r[:n1], r[n1:]
        tc, ts = tc_ref[j], ts_ref[j]
        o_ref[j, 0] = (br * tc + bi * ts).astype(o_ref.dtype)
        o_ref[j, 1] = (bi * tc - br * ts).astype(o_ref.dtype)


def _fft_b_kernel(b_ref, m_ref, cs_ref, o_ref):
    m2 = m_ref[...]
    cs = cs_ref[...]
    n2 = o_ref.shape[1]
    for j in range(b_ref.shape[0]):
        x = b_ref[j].reshape(2 * n2, FOURIER_DIM)
        z = jnp.dot(m2, x, preferred_element_type=F32).astype(BF16)
        for g in range(N_GROUPS):
            lo = g * GROUP_DIM
            zz = jnp.concatenate([z[:n2, lo:lo + GROUP_DIM], z[n2:, lo:lo + GROUP_DIM]],
                                 axis=1)
            y = jnp.dot(zz, cs, preferred_element_type=F32)
            o_ref[j, :, lo:lo + GROUP_DIM] = y.astype(o_ref.dtype)


def _dft_tables(seq):
    n2 = FFT_N2
    n1 = seq // n2
    a1 = 2.0 * np.pi * np.outer(np.arange(n1), np.arange(n1)) / n1
    f1 = np.concatenate([np.cos(a1), -np.sin(a1)], axis=0) / math.sqrt(n1)
    th = 2.0 * np.pi * np.outer(np.arange(n2), np.arange(n1)) / seq
    a2 = 2.0 * np.pi * np.outer(np.arange(n2), np.arange(n2)) / n2
    c2, s2 = np.cos(a2), np.sin(a2)
    m2 = np.block([[c2, s2], [-s2, c2]]) / math.sqrt(n2)
    ac = 2.0 * np.pi * np.outer(np.arange(GROUP_DIM), np.arange(GROUP_DIM)) / GROUP_DIM
    cs = np.concatenate([np.cos(ac), np.sin(ac)], axis=0) / math.sqrt(GROUP_DIM)
    return (jnp.asarray(f1, BF16), jnp.asarray(np.cos(th)[:, :, None], F32),
            jnp.asarray(np.sin(th)[:, :, None], F32), jnp.asarray(m2, BF16),
            jnp.asarray(cs, BF16))


def _fourier(u, t2, tk1):
    b, s, c = u.shape
    n2 = FFT_N2
    n1 = s // n2
    t2 = min(t2, n2)
    tk1 = min(tk1, n1)
    f1, tc, ts, m2, cs = _dft_tables(s)
    ut = u.reshape(b, n1, n2, c).transpose(0, 2, 1, 3)
    bt = pl.pallas_call(
        functools.partial(_fft_a_kernel, n1=n1),
        grid=(b, n2 // t2),
        in_specs=[pl.BlockSpec((None, t2, n1, c), lambda bi, i: (bi, i, 0, 0)),
                  pl.BlockSpec((2 * n1, n1), lambda bi, i: (0, 0)),
                  pl.BlockSpec((t2, n1, 1), lambda bi, i: (i, 0, 0)),
                  pl.BlockSpec((t2, n1, 1), lambda bi, i: (i, 0, 0))],
        out_specs=pl.BlockSpec((None, t2, 2, n1, c), lambda bi, i: (bi, i, 0, 0, 0)),
        out_shape=jax.ShapeDtypeStruct((b, n2, 2, n1, c), BF16),
        compiler_params=_params(("parallel", "arbitrary")),
        name="fft_stage_a",
    )(ut, f1, tc, ts)
    bk = bt.transpose(0, 3, 2, 1, 4)
    y = pl.pallas_call(
        _fft_b_kernel,
        grid=(b, n1 // tk1),
        in_specs=[pl.BlockSpec((None, tk1, 2, n2, c), lambda bi, i: (bi, i, 0, 0, 0)),
                  pl.BlockSpec((2 * n2, 2 * n2), lambda bi, i: (0, 0)),
                  pl.BlockSpec((2 * GROUP_DIM, GROUP_DIM), lambda bi, i: (0, 0))],
        out_specs=pl.BlockSpec((None, tk1, n2, c), lambda bi, i: (bi, i, 0, 0)),
        out_shape=jax.ShapeDtypeStruct((b, n1, n2, c), BF16),
        compiler_params=_params(("parallel", "arbitrary")),
        name="fft_stage_b",
    )(bk, m2, cs)
    return y.transpose(0, 2, 1, 3).reshape(b, s, c)


def _merge_kernel(yf_ref, oa_ref, om_ref, g0_ref, g1_ref, g2_ref,
                  wf_ref, wa_ref, wm_ref, o_ref):
    def gate(g_ref):
        return jax.nn.sigmoid(g_ref[...].astype(F32))
    yf = jnp.dot(yf_ref[...], wf_ref[...], preferred_element_type=F32)
    ya = jnp.dot(oa_ref[...], wa_ref[...], preferred_element_type=F32)
    ym = jnp.dot(om_ref[...], wm_ref[...], preferred_element_type=F32)
    o_ref[...] = (gate(g0_ref) * yf + gate(g1_ref) * ya + gate(g2_ref) * ym
                  ).astype(o_ref.dtype)


def _merge(yf, oa, om, z1, wf, wa, wm, tm, tn):
    t = yf.shape[0]
    tm = min(tm, t)
    nb = D_MODEL // tn
    def gspec(k):
        return pl.BlockSpec((tm, tn), lambda i, j: (i, k * nb + j))
    def aspec(w):
        return pl.BlockSpec((tm, w), lambda i, j: (i, 0))
    def wspec(w):
        return pl.BlockSpec((w, tn), lambda i, j: (0, j))
    return pl.pallas_call(
        _merge_kernel,
        grid=(t // tm, nb),
        in_specs=[aspec(FOURIER_DIM), aspec(MLA_HEADS * V_HEAD), aspec(MEM_DIM),
                  gspec(0), gspec(1), gspec(2),
                  wspec(FOURIER_DIM), wspec(MLA_HEADS * V_HEAD), wspec(MEM_DIM)],
        out_specs=pl.BlockSpec((tm, tn), lambda i, j: (i, j)),
        out_shape=jax.ShapeDtypeStruct((t, D_MODEL), BF16),
        compiler_params=_params(("parallel", "arbitrary")),
        name="branch_merge",
    )(yf, oa, om, z1, z1, z1, wf, wa, wm)


def _proj_norm_kernel(a_ref, w_ref, x_ref, gpost_ref, gpre_ref, x1_ref, *h_ref, ts):
    for r0 in range(0, a_ref.shape[0], ts):
        rows = slice(r0, r0 + ts)
        y = jnp.dot(a_ref[rows, :], w_ref[...], preferred_element_type=F32)
        x1 = x_ref[rows, :] + _rms(y, gpost_ref[...])
        x1_ref[rows, :] = x1
        if h_ref:
            h_ref[0][rows, :] = _rms(x1, gpre_ref[...]).astype(h_ref[0].dtype)


def _proj_norm(a, w, x, gpost, gpre, tm, ts, emit_h, name):
    t, d = x.shape
    k = a.shape[1]
    tm = min(tm, t)
    row = lambda i: (i, 0)
    fix = lambda i: (0, 0)
    out_specs = [pl.BlockSpec((tm, d), row)]
    out_shape = [jax.ShapeDtypeStruct((t, d), F32)]
    if emit_h:
        out_specs.append(pl.BlockSpec((tm, d), row))
        out_shape.append(jax.ShapeDtypeStruct((t, d), BF16))
    outs = pl.pallas_call(
        functools.partial(_proj_norm_kernel, ts=min(ts, tm)),
        grid=(t // tm,),
        in_specs=[pl.BlockSpec((tm, k), row),
                  pl.BlockSpec((k, d), fix, pipeline_mode=pl.Buffered(1)),
                  pl.BlockSpec((tm, d), row), pl.BlockSpec((1, d), fix),
                  pl.BlockSpec((1, d), fix)],
        out_specs=out_specs,
        out_shape=out_shape,
        compiler_params=_params(("parallel",)),
        name=name,
    )(a, w, x, gpost.reshape(1, d), gpre.reshape(1, d))
    return outs if emit_h else (outs[0], None)


def _gelu_tanh(x):
    return 0.5 * x * (1.0 + jnp.tanh(0.7978845608028654 * (x + 0.044715 * x * x * x)))


def _ffn_act_kernel(a_ref, ap_ref, an_ref, wg_ref, wu_ref, cw_ref, cb_ref, o_ref, ax_ref, *,
                    tiles_per_seq, halo):
    tm = a_ref.shape[0]
    tc = o_ref.shape[1]

    pos = pl.program_id(0) % tiles_per_seq

    @pl.when(pl.program_id(1) == 0)
    def _():
        zero = jnp.zeros_like(ap_ref[...])
        ax_ref[0:halo] = jnp.where(pos == 0, zero, ap_ref[...])
        ax_ref[halo:halo + tm] = a_ref[...]
        ax_ref[halo + tm:] = jnp.where(pos == tiles_per_seq - 1, zero, an_ref[...])

    g = jnp.dot(ax_ref[...], wg_ref[...], preferred_element_type=F32)
    u = jnp.dot(ax_ref[halo:halo + tm], wu_ref[...], preferred_element_type=F32)
    g_prev = pltpu.roll(g, 1, 0)[halo:halo + tm]
    g_next = pltpu.roll(g, tm + 2 * halo - 1, 0)[halo:halo + tm]
    cw = cw_ref[...]
    conv = g_prev * cw[0:1] + g[halo:halo + tm] * cw[1:2] + g_next * cw[2:3] + cb_ref[...]
    o_ref[...] = (_gelu_tanh(conv) * u).astype(o_ref.dtype)


def _ffn_act(h, w_gate, w_up, cw, cb, seq, tm, tc):
    t, d = h.shape
    n = cw.shape[1]
    tm = min(tm, seq)
    halo = 16
    tiles_per_seq = seq // tm
    hb = tm // halo
    last_hb = t // halo - 1
    return pl.pallas_call(
        functools.partial(_ffn_act_kernel, tiles_per_seq=tiles_per_seq, halo=halo),
        grid=(t // tm, n // tc),
        in_specs=[pl.BlockSpec((tm, d), lambda i, j: (i, 0)),
                  pl.BlockSpec((halo, d), lambda i, j: (jnp.maximum(i * hb - 1, 0), 0)),
                  pl.BlockSpec((halo, d), lambda i, j: (jnp.minimum((i + 1) * hb, last_hb), 0)),
                  pl.BlockSpec((d, tc), lambda i, j: (0, j)),
                  pl.BlockSpec((d, tc), lambda i, j: (0, j)),
                  pl.BlockSpec((3, tc), lambda i, j: (0, j)),
                  pl.BlockSpec((1, tc), lambda i, j: (0, j))],
        out_specs=pl.BlockSpec((tm, tc), lambda i, j: (i, j)),
        out_shape=jax.ShapeDtypeStruct((t, n), BF16),
        scratch_shapes=[pltpu.VMEM((tm + 2 * halo, d), BF16)],
        compiler_params=_params(("parallel", "arbitrary")),
        name="ffn_gate_up_act",
    )(h, h, h, w_gate, w_up, cw, cb.reshape(1, n))


def _pack_weights(p, l):
    s0 = FOURIER_DIM
    s1 = s0 + Q_LORA
    s2 = s1 + KV_LORA
    s3 = s2 + QK_ROPE
    s4 = s3 + MEM_DIM
    w_in = p["w_in"][l]
    d = w_in.shape[0]
    bf = lambda a: a.astype(BF16)
    w_gq = bf(jnp.concatenate([w_in[:, s4:], w_in[:, s3:s4]], axis=1))
    w_f = bf(w_in[:, :s0])
    w_lat = bf(jnp.concatenate([w_in[:, s0:s3], jnp.zeros((d, KPE_PAD - QK_ROPE), F32)],
                               axis=1))
    wq = p["w_uq"][l].reshape(Q_LORA, MLA_HEADS, QK_NOPE + QK_ROPE)
    wq = jnp.pad(wq, ((0, 0), (0, 0), (0, QK_PAD - QK_NOPE - QK_ROPE)))
    wq = bf(wq.reshape(Q_LORA, MLA_HEADS * QK_PAD))
    wkv = p["w_ukv"][l].reshape(KV_LORA, MLA_HEADS, QK_NOPE + V_HEAD)
    wk = bf(wkv[:, :, :QK_NOPE].reshape(KV_LORA, MLA_HEADS * QK_NOPE))
    wvt = bf(wkv[:, :, QK_NOPE:].reshape(KV_LORA, MLA_HEADS * V_HEAD).T)
    return dict(
        w_gq=w_gq, w_f=w_f, w_lat=w_lat, wq=wq, wk=wk, wvt=wvt,
        w_gate=bf(p["w_ffn_gate"][l]), w_up=bf(p["w_ffn_up"][l]),
        w_mem_kv=bf(p["w_mem_kv"][l]), w_fo=bf(p["w_fourier_out"][l]),
        w_ao=bf(p["w_attn_out"][l]), w_mo=bf(p["w_mem_out"][l]), w_o=bf(p["w_o"][l]),
        w_down=bf(p["w_ffn_down"][l]), conv_w=p["ffn_conv_w"][l], conv_b=p["ffn_conv_b"][l],
        q_norm=p["q_norm"][l], kv_norm=p["kv_norm"][l], mem_norm=p["mem_norm"][l],
        pre_mix=p["pre_mix_norm"][l], post_mix=p["post_mix_norm"][l],
        pre_ffn=p["pre_ffn_norm"][l], post_ffn=p["post_ffn_norm"][l])


def _rope_tables(seq):
    inv_freq = 1.0 / (ROPE_THETA ** (jnp.arange(0, QK_ROPE, 2, dtype=F32) / QK_ROPE))
    ang = jnp.arange(seq, dtype=F32)[:, None] * inv_freq[None, :]
    cos, sin = jnp.cos(ang), jnp.sin(ang)
    half = QK_ROPE // 2
    zh = jnp.zeros((seq, half), F32)
    zr = jnp.zeros((seq, LANES - QK_ROPE), F32)
    c = jnp.concatenate([cos, cos, zr], axis=1)
    s1 = jnp.concatenate([-sin, zh, zr], axis=1)
    s2 = jnp.concatenate([zh, sin, zr], axis=1)
    return c, s1, s2


def _trunk(x, mem, layers):
    b, s, d = x.shape
    t = b * s
    n_mem = mem.shape[1]
    tm = min(512, s)
    tabs = _rope_tables(s)
    q_scale = (QK_NOPE + QK_ROPE) ** -0.5 * LOG2E
    gate_w = 3 * D_MODEL
    xf = x.reshape(t, d)
    memf = mem.reshape(b * n_mem, d)
    h = _norm(xf, layers[0]["pre_mix"], tm)
    for l, w in enumerate(layers):
        z1 = _mm(h, w["w_gq"], 1024, 1024, "in_proj_gates")
        f_in = _mm(h, w["w_f"], 1024, 1024, "in_proj_fourier")
        q, kc, vt = _attn_prep(h, w, tabs, b, s, 256, q_scale)
        oa = _mla(q.reshape(b, s, -1), kc.reshape(b, s, -1), vt, 2048, 1024, 256)
        mem_n = _norm(memf, w["mem_norm"], n_mem)
        kvm = _mm(mem_n, w["w_mem_kv"], 1024, 1024, "mem_kv")
        om = _mem_attn(z1.reshape(b, s, -1), kvm.reshape(b, n_mem, -1),
                       gate_w // MEM_DIM, 1024)
        yf = _fourier(f_in.reshape(b, s, -1), 16, 16)
        merged = _merge(yf.reshape(t, -1), oa.reshape(t, -1), om.reshape(t, -1), z1,
                        w["w_fo"], w["w_ao"], w["w_mo"], 1024, 512)
        x1, h2 = _proj_norm(merged, w["w_o"], xf, w["post_mix"], w["pre_ffn"], 512, 128,
                            True, "out_proj_norm")
        act = _ffn_act(h2, w["w_gate"], w["w_up"], w["conv_w"], w["conv_b"], s, 1024,
                       FF_CHUNK)
        last = l == len(layers) - 1
        nxt = layers[0 if last else l + 1]["pre_mix"]
        xf, h = _proj_norm(act, w["w_down"], x1, w["post_ffn"], nxt, 256, 128, not last,
                           "ffn_down_norm")
    return xf.reshape(b, s, d)


def kernel(x_prompt, x_sample, mem_prompt, mem_sample, pre_mix_norm, w_in, q_norm, w_uq,
           kv_norm, w_ukv, mem_norm, w_mem_kv, w_fourier_out, w_attn_out, w_mem_out, w_o,
           post_mix_norm, pre_ffn_norm, w_ffn_gate, w_ffn_up, ffn_conv_w, ffn_conv_b,
           w_ffn_down, post_ffn_norm):
    p = dict(pre_mix_norm=pre_mix_norm, w_in=w_in, q_norm=q_norm, w_uq=w_uq,
             kv_norm=kv_norm, w_ukv=w_ukv, mem_norm=mem_norm, w_mem_kv=w_mem_kv,
             w_fourier_out=w_fourier_out, w_attn_out=w_attn_out, w_mem_out=w_mem_out,
             w_o=w_o, post_mix_norm=post_mix_norm, pre_ffn_norm=pre_ffn_norm,
             w_ffn_gate=w_ffn_gate, w_ffn_up=w_ffn_up, ffn_conv_w=ffn_conv_w,
             ffn_conv_b=ffn_conv_b, w_ffn_down=w_ffn_down, post_ffn_norm=post_ffn_norm)
    layers = [_pack_weights(p, l) for l in range(w_in.shape[0])]
    return (_trunk(x_prompt, mem_prompt, layers), _trunk(x_sample, mem_sample, layers))
```

```python
import functools
import math

import numpy as np
import jax
import jax.numpy as jnp
from jax import lax
from jax.experimental import pallas as pl
from jax.experimental.pallas import tpu as pltpu

D_MODEL = 2048
N_GROUPS = 4
GROUP_DIM = 256
FOURIER_DIM = N_GROUPS * GROUP_DIM
MLA_HEADS = 16
Q_LORA = 512
KV_LORA = 512
QK_NOPE = 128
QK_ROPE = 64
V_HEAD = 128
ROPE_THETA = 10000.0
MEM_HEADS = 4
MEM_HEAD_DIM = 256
MEM_DIM = MEM_HEADS * MEM_HEAD_DIM
D_FF = 5632
EPS = 1e-6

LANES = 128
QK_PAD = 256
KPE_PAD = LANES
V_ROWS = V_HEAD + 16
N_SLOTS = 3
FFT_N2 = 128
VMEM_LIMIT = 56 * 1024 * 1024
FF_CHUNK = 512
LOG2E = 1.4426950408889634

BF16 = jnp.bfloat16
F32 = jnp.float32


def _params(sem):
    return pltpu.CompilerParams(dimension_semantics=sem, vmem_limit_bytes=VMEM_LIMIT)


def _rms(x, g):
    ms = jnp.mean(x * x, axis=-1, keepdims=True)
    return x * lax.rsqrt(ms + EPS) * g


def _rope128(x, c, s1, s2):
    return x * c + pltpu.roll(x, 96, 1) * s1 + pltpu.roll(x, 32, 1) * s2


def _norm_kernel(x_ref, g_ref, o_ref):
    o_ref[...] = _rms(x_ref[...].astype(F32), g_ref[...]).astype(o_ref.dtype)


def _norm(x, g, tm):
    t, d = x.shape
    return pl.pallas_call(
        _norm_kernel,
        grid=(t // tm,),
        in_specs=[pl.BlockSpec((tm, d), lambda i: (i, 0)),
                  pl.BlockSpec((1, d), lambda i: (0, 0))],
        out_specs=pl.BlockSpec((tm, d), lambda i: (i, 0)),
        out_shape=jax.ShapeDtypeStruct((t, d), BF16),
        compiler_params=_params(("parallel",)),
        name="rmsnorm",
    )(x, g.reshape(1, d))


def _mm_kernel(a_ref, w_ref, o_ref):
    o_ref[...] = jnp.dot(a_ref[...], w_ref[...],
                         preferred_element_type=F32).astype(o_ref.dtype)


def _mm(a, w, tm, tn, name):
    m, k = a.shape
    n = w.shape[1]
    tm = min(tm, m)
    tn = min(tn, n)
    return pl.pallas_call(
        _mm_kernel,
        grid=(m // tm, n // tn),
        in_specs=[pl.BlockSpec((tm, k), lambda i, j: (i, 0)),
                  pl.BlockSpec((k, tn), lambda i, j: (0, j))],
        out_specs=pl.BlockSpec((tm, tn), lambda i, j: (i, j)),
        out_shape=jax.ShapeDtypeStruct((m, n), BF16),
        compiler_params=_params(("parallel", "arbitrary")),
        name=name,
    )(a, w)


def _norm_mm_kernel(x_ref, g_ref, w_ref, h_ref, o_ref):
    @pl.when(pl.program_id(1) == 0)
    def _():
        h_ref[...] = _rms(x_ref[...], g_ref[...]).astype(h_ref.dtype)

    o_ref[...] = jnp.dot(h_ref[...], w_ref[...],
                         preferred_element_type=F32).astype(o_ref.dtype)


def _norm_mm(x, g, w, tm, tn, name):
    m, k = x.shape
    n = w.shape[1]
    tm = min(tm, m)
    return pl.pallas_call(
        _norm_mm_kernel,
        grid=(m // tm, n // tn),
        in_specs=[pl.BlockSpec((tm, k), lambda i, j: (i, 0)),
                  pl.BlockSpec((1, k), lambda i, j: (0, 0)),
                  pl.BlockSpec((k, tn), lambda i, j: (0, j))],
        out_specs=[pl.BlockSpec((tm, k), lambda i, j: (i, 0)),
                   pl.BlockSpec((tm, tn), lambda i, j: (i, j))],
        out_shape=[jax.ShapeDtypeStruct((m, k), BF16), jax.ShapeDtypeStruct((m, n), BF16)],
        compiler_params=_params(("parallel", "arbitrary")),
        name=name,
    )(x, g.reshape(1, k), w)


def _attn_prep_kernel(h_ref, wl_ref, qg_ref, kvg_ref, wq_ref, wk_ref, wvt_ref,
                      c_ref, s1_ref, s2_ref, q_ref, k_ref, vt_ref, *, scale):
    c, s1, s2 = c_ref[...], s1_ref[...], s2_ref[...]
    z = jnp.dot(h_ref[...], wl_ref[...], preferred_element_type=F32)
    cq = _rms(z[:, :Q_LORA], qg_ref[...]).astype(BF16)
    lat = _rms(z[:, Q_LORA:Q_LORA + KV_LORA], kvg_ref[...]).astype(BF16)
    kpe = _rope128(z[:, Q_LORA + KV_LORA:], c, s1, s2).astype(BF16)
    zq = jnp.dot(cq, wq_ref[...], preferred_element_type=F32) * scale
    kn = jnp.dot(lat, wk_ref[...], preferred_element_type=F32).astype(BF16)
    vt = lax.dot_general(wvt_ref[...], lat, (((1,), (1,)), ((), ())),
                         preferred_element_type=F32).astype(BF16)
    pad_rows = V_ROWS - V_HEAD
    row = lax.broadcasted_iota(jnp.int32, (pad_rows, vt.shape[1]), 0)
    ones_rows = jnp.where(row == 0, 1.0, 0.0).astype(BF16)
    for h in range(MLA_HEADS):
        lo = h * QK_PAD
        q_ref[:, lo:lo + QK_NOPE] = zq[:, lo:lo + QK_NOPE].astype(BF16)
        pe = _rope128(zq[:, lo + QK_NOPE:lo + QK_PAD], c, s1, s2)
        q_ref[:, lo + QK_NOPE:lo + QK_PAD] = pe.astype(BF16)
        k_ref[:, lo:lo + QK_NOPE] = kn[:, h * QK_NOPE:(h + 1) * QK_NOPE]
        k_ref[:, lo + QK_NOPE:lo + QK_PAD] = kpe
        vt_ref[h * V_ROWS:h * V_ROWS + V_HEAD, :] = vt[h * V_HEAD:(h + 1) * V_HEAD, :]
        vt_ref[h * V_ROWS + V_HEAD:(h + 1) * V_ROWS, :] = ones_rows


def _attn_prep(h, w, tabs, b, seq, tm, scale):
    d = h.shape[1]
    tm = min(tm, seq)
    nblk = seq // tm
    nq = MLA_HEADS * QK_PAD
    nv = MLA_HEADS * V_ROWS
    row = lambda bi, i: (bi * nblk + i, 0)
    fix = lambda bi, i: (0, 0)
    tab = pl.BlockSpec((tm, LANES), lambda bi, i: (i, 0))
    def wspec(arr):
        return pl.BlockSpec(arr.shape, fix, pipeline_mode=pl.Buffered(1))
    return pl.pallas_call(
        functools.partial(_attn_prep_kernel, scale=scale),
        grid=(b, nblk),
        in_specs=[pl.BlockSpec((tm, d), row), wspec(w["w_lat"]),
                  pl.BlockSpec((1, Q_LORA), fix), pl.BlockSpec((1, KV_LORA), fix),
                  wspec(w["wq"]), wspec(w["wk"]), wspec(w["wvt"]), tab, tab, tab],
        out_specs=[pl.BlockSpec((tm, nq), row), pl.BlockSpec((tm, nq), row),
                   pl.BlockSpec((None, nv, tm), lambda bi, i: (bi, 0, i))],
        out_shape=[jax.ShapeDtypeStruct((b * seq, nq), BF16),
                   jax.ShapeDtypeStruct((b * seq, nq), BF16),
                   jax.ShapeDtypeStruct((b, nv, seq), BF16)],
        compiler_params=_params(("parallel", "arbitrary")),
        name="attn_prep",
    )(h, w["w_lat"], w["q_norm"].reshape(1, -1), w["kv_norm"].reshape(1, -1),
      w["wq"], w["wk"], w["wvt"], *tabs)


def _mla_kernel(q_ref, k_ref, vt_ref, o_ref, s_ref, p_ref, qt_ref, *, tq, tk):
    n_q = q_ref.shape[0] // tq
    n_k = k_ref.shape[0] // tk

    def q_tile(qi, carry):
        q0 = pl.multiple_of(qi * tq, tq)
        qt_ref[...] = q_ref[pl.ds(q0, tq), :].T

        def scores(c, slot):
            k = k_ref[pl.ds(pl.multiple_of(c * tk, tk), tk), :]
            s = jnp.dot(k, qt_ref[...], preferred_element_type=F32)
            s_ref[slot] = s
            return jnp.max(s, axis=0, keepdims=True)

        def softmax(slot, cmax, m):
            m_new = jnp.maximum(m, cmax)
            p_ref[slot] = jnp.exp2((s_ref[slot] - m_new).astype(BF16))
            return m_new, jnp.exp2(m - m_new)

        def pv(c, slot):
            vt = vt_ref[:, pl.ds(pl.multiple_of(c * tk, tk), tk)]
            return jnp.dot(vt, p_ref[slot], preferred_element_type=F32)

        def stage(c, u, m, acc, cmax):
            part = pv(c - 1, (u - 1) % N_SLOTS)
            cmax_next = scores(c + 1, (u + 1) % N_SLOTS)
            m, alpha = softmax(u % N_SLOTS, cmax, m)
            return m, (acc + part) * alpha, cmax_next

        cmax = scores(0, 0)
        m, _ = softmax(0, cmax, jnp.full((1, tq), -1e30, F32))
        cmax = scores(1, 1)
        acc = jnp.zeros((V_ROWS, tq), F32)

        n_mid = n_k - 2
        unroll = 4 * N_SLOTS
        peel = n_mid % unroll
        st = (m, acc, cmax)
        for u in range(1, peel + 1):
            st = stage(u, u, *st)

        def group(j, st):
            for u in range(1, unroll + 1):
                st = stage(peel + unroll * j + u, peel + u, *st)
            return st

        trips = n_mid // unroll
        if trips == 1:
            st = group(0, st)
        elif trips > 1:
            st = lax.fori_loop(0, trips, group, st)
        m, acc, cmax = st
        part = pv(n_k - 2, (n_k - 2) % N_SLOTS)
        m, alpha = softmax((n_k - 1) % N_SLOTS, cmax, m)
        acc = (acc + part) * alpha + pv(n_k - 1, (n_k - 1) % N_SLOTS)
        out = acc[:V_HEAD] * (1.0 / acc[V_HEAD:V_HEAD + 1])
        o_ref[pl.ds(q0, tq), :] = out.T.astype(o_ref.dtype)
        return carry

    lax.fori_loop(0, n_q, q_tile, 0)


def _mla(q, k, vt, tq_outer, tq, tk):
    b, s, _ = q.shape
    tq_outer = min(tq_outer, s)
    tq = min(tq, tq_outer)
    tk = min(tk, s // 2)
    assert (s // tk) % 2 == 0
    return pl.pallas_call(
        functools.partial(_mla_kernel, tq=tq, tk=tk),
        grid=(b, MLA_HEADS, s // tq_outer),
        in_specs=[pl.BlockSpec((None, tq_outer, QK_PAD), lambda bi, h, qi: (bi, qi, h)),
                  pl.BlockSpec((None, s, QK_PAD), lambda bi, h, qi: (bi, 0, h)),
                  pl.BlockSpec((None, V_ROWS, s), lambda bi, h, qi: (bi, h, 0))],
        out_specs=pl.BlockSpec((None, tq_outer, V_HEAD), lambda bi, h, qi: (bi, qi, h)),
        out_shape=jax.ShapeDtypeStruct((b, s, MLA_HEADS * V_HEAD), BF16),
        scratch_shapes=[pltpu.VMEM((N_SLOTS, tk, tq), F32),
                        pltpu.VMEM((N_SLOTS, tk, tq), BF16),
                        pltpu.VMEM((QK_PAD, tq), BF16)],
        compiler_params=_params(("parallel", "parallel", "arbitrary")),
        name="mla_attention",
    )(q, k, vt)


def _mem_kernel(q_ref, kv_ref, o_ref, *, scale):
    for h in range(MEM_HEADS):
        lo = h * MEM_HEAD_DIM
        q = q_ref[:, lo:lo + MEM_HEAD_DIM]
        k = kv_ref[:, lo:lo + MEM_HEAD_DIM]
        v = kv_ref[:, MEM_DIM + lo:MEM_DIM + lo + MEM_HEAD_DIM]
        s = lax.dot_general(q, k, (((1,), (1,)), ((), ())),
                            preferred_element_type=F32) * scale
        p = jnp.exp2(s - jnp.max(s, axis=-1, keepdims=True))
        l = jnp.sum(p, axis=-1, keepdims=True)
        o = jnp.dot(p.astype(BF16), v, preferred_element_type=F32)
        o_ref[:, lo:lo + MEM_HEAD_DIM] = (o * (1.0 / l)).astype(o_ref.dtype)


def _mem_attn(z1, kvm, q_blk, tq):
    b, s, _ = z1.shape
    n_mem = kvm.shape[1]
    tq = min(tq, s)
    scale = MEM_HEAD_DIM ** -0.5 * LOG2E
    return pl.pallas_call(
        functools.partial(_mem_kernel, scale=scale),
        grid=(b, s // tq),
        in_specs=[pl.BlockSpec((None, tq, MEM_DIM), lambda bi, i: (bi, i, q_blk)),
                  pl.BlockSpec((None, n_mem, 2 * MEM_DIM), lambda bi, i: (bi, 0, 0))],
        out_specs=pl.BlockSpec((None, tq, MEM_DIM), lambda bi, i: (bi, i, 0)),
        out_shape=jax.ShapeDtypeStruct((b, s, MEM_DIM), BF16),
        compiler_params=_params(("parallel", "arbitrary")),
        name="mem_attention",
    )(z1, kvm)


def _fft_a_kernel(u_ref, f_ref, tc_ref, ts_ref, o_ref, *, n1):
    f = f_ref[...]
    for j in range(u_ref.shape[0]):
        r = jnp.dot(f, u_ref[j], preferred_element_type=F32)
        br, bi = r[:n1], r[n1:]
        tc, ts = tc_ref[j], ts_ref[j]
        o_ref[j, 0] = (br * tc + bi * ts).astype(o_ref.dtype)
        o_ref[j, 1] = (bi * tc - br * ts).astype(o_ref.dtype)


def _fft_b_kernel(b_ref, m_ref, cs_ref, o_ref):
    m2 = m_ref[...]
    cs = cs_ref[...]
    n2 = o_ref.shape[1]
    for j in range(b_ref.shape[0]):
        x = b_ref[j].reshape(2 * n2, FOURIER_DIM)
        z = jnp.dot(m2, x, preferred_element_type=F32).astype(BF16)
        for g in range(N_GROUPS):
            lo = g * GROUP_DIM
            zz = jnp.concatenate([z[:n2, lo:lo + GROUP_DIM], z[n2:, lo:lo + GROUP_DIM]],
                                 axis=1)
            y = jnp.dot(zz, cs, preferred_element_type=F32)
            o_ref[j, :, lo:lo + GROUP_DIM] = y.astype(o_ref.dtype)


def _dft_tables(seq):
    n2 = FFT_N2
    n1 = seq // n2
    a1 = 2.0 * np.pi * np.outer(np.arange(n1), np.arange(n1)) / n1
    f1 = np.concatenate([np.cos(a1), -np.sin(a1)], axis=0) / math.sqrt(n1)
    th = 2.0 * np.pi * np.outer(np.arange(n2), np.arange(n1)) / seq
    a2 = 2.0 * np.pi * np.outer(np.arange(n2), np.arange(n2)) / n2
    c2, s2 = np.cos(a2), np.sin(a2)
    m2 = np.block([[c2, s2], [-s2, c2]]) / math.sqrt(n2)
    ac = 2.0 * np.pi * np.outer(np.arange(GROUP_DIM), np.arange(GROUP_DIM)) / GROUP_DIM
    cs = np.concatenate([np.cos(ac), np.sin(ac)], axis=0) / math.sqrt(GROUP_DIM)
    return (jnp.asarray(f1, BF16), jnp.asarray(np.cos(th)[:, :, None], F32),
            jnp.asarray(np.sin(th)[:, :, None], F32), jnp.asarray(m2, BF16),
            jnp.asarray(cs, BF16))


def _fourier(u, t2, tk1):
    b, s, c = u.shape
    n2 = FFT_N2
    n1 = s // n2
    t2 = min(t2, n2)
    tk1 = min(tk1, n1)
    f1, tc, ts, m2, cs = _dft_tables(s)
    ut = u.reshape(b, n1, n2, c).transpose(0, 2, 1, 3)
    bt = pl.pallas_call(
        functools.partial(_fft_a_kernel, n1=n1),
        grid=(b, n2 // t2),
        in_specs=[pl.BlockSpec((None, t2, n1, c), lambda bi, i: (bi, i, 0, 0)),
                  pl.BlockSpec((2 * n1, n1), lambda bi, i: (0, 0)),
                  pl.BlockSpec((t2, n1, 1), lambda bi, i: (i, 0, 0)),
                  pl.BlockSpec((t2, n1, 1), lambda bi, i: (i, 0, 0))],
        out_specs=pl.BlockSpec((None, t2, 2, n1, c), lambda bi, i: (bi, i, 0, 0, 0)),
        out_shape=jax.ShapeDtypeStruct((b, n2, 2, n1, c), BF16),
        compiler_params=_params(("parallel", "arbitrary")),
        name="fft_stage_a",
    )(ut, f1, tc, ts)
    bk = bt.transpose(0, 3, 2, 1, 4)
    y = pl.pallas_call(
        _fft_b_kernel,
        grid=(b, n1 // tk1),
        in_specs=[pl.BlockSpec((None, tk1, 2, n2, c), lambda bi, i: (bi, i, 0, 0, 0)),
                  pl.BlockSpec((2 * n2, 2 * n2), lambda bi, i: (0, 0)),
                  pl.BlockSpec((2 * GROUP_DIM, GROUP_DIM), lambda bi, i: (0, 0))],
        out_specs=pl.BlockSpec((None, tk1, n2, c), lambda bi, i: (bi, i, 0, 0)),
        out_shape=jax.ShapeDtypeStruct((b, n1, n2, c), BF16),
        compiler_params=_params(("parallel", "arbitrary")),
        name="fft_stage_b",
    )(bk, m2, cs)
    return y.transpose(0, 2, 1, 3).reshape(b, s, c)


def _merge_kernel(yf_ref, oa_ref, om_ref, g0_ref, g1_ref, g2_ref,
                  wf_ref, wa_ref, wm_ref, o_ref):
    def gate(g_ref):
        return jax.nn.sigmoid(g_ref[...].astype(F32))
    yf = jnp.dot(yf_ref[...], wf_ref[...], preferred_element_type=F32)
    ya = jnp.dot(oa_ref[...], wa_ref[...], preferred_element_type=F32)
    ym = jnp.dot(om_ref[...], wm_ref[...], preferred_element_type=F32)
    o_ref[...] = (gate(g0_ref) * yf + gate(g1_ref) * ya + gate(g2_ref) * ym
                  ).astype(o_ref.dtype)


def _merge(yf, oa, om, z1, wf, wa, wm, tm, tn):
    t = yf.shape[0]
    tm = min(tm, t)
    nb = D_MODEL // tn
    def gspec(k):
        return pl.BlockSpec((tm, tn), lambda i, j: (i, k * nb + j))
    def aspec(w):
        return pl.BlockSpec((tm, w), lambda i, j: (i, 0))
    def wspec(w):
        return pl.BlockSpec((w, tn), lambda i, j: (0, j))
    return pl.pallas_call(
        _merge_kernel,
        grid=(t // tm, nb),
        in_specs=[aspec(FOURIER_DIM), aspec(MLA_HEADS * V_HEAD), aspec(MEM_DIM),
                  gspec(0), gspec(1), gspec(2),
                  wspec(FOURIER_DIM), wspec(MLA_HEADS * V_HEAD), wspec(MEM_DIM)],
        out_specs=pl.BlockSpec((tm, tn), lambda i, j: (i, j)),
        out_shape=jax.ShapeDtypeStruct((t, D_MODEL), BF16),
        compiler_params=_params(("parallel", "arbitrary")),
        name="branch_merge",
    )(yf, oa, om, z1, z1, z1, wf, wa, wm)


def _proj_norm_kernel(a_ref, w_ref, x_ref, gpost_ref, gpre_ref, x1_ref, *h_ref, ts):
    for r0 in range(0, a_ref.shape[0], ts):
        rows = slice(r0, r0 + ts)
        y = jnp.dot(a_ref[rows, :], w_ref[...], preferred_element_type=F32)
        x1 = x_ref[rows, :] + _rms(y, gpost_ref[...])
        x1_ref[rows, :] = x1
        if h_ref:
            h_ref[0][rows, :] = _rms(x1, gpre_ref[...]).astype(h_ref[0].dtype)


def _proj_norm(a, w, x, gpost, gpre, tm, ts, emit_h, name):
    t, d = x.shape
    k = a.shape[1]
    tm = min(tm, t)
    row = lambda i: (i, 0)
    fix = lambda i: (0, 0)
    out_specs = [pl.BlockSpec((tm, d), row)]
    out_shape = [jax.ShapeDtypeStruct((t, d), F32)]
    if emit_h:
        out_specs.append(pl.BlockSpec((tm, d), row))
        out_shape.append(jax.ShapeDtypeStruct((t, d), BF16))
    outs = pl.pallas_call(
        functools.partial(_proj_norm_kernel, ts=min(ts, tm)),
        grid=(t // tm,),
        in_specs=[pl.BlockSpec((tm, k), row),
                  pl.BlockSpec((k, d), fix, pipeline_mode=pl.Buffered(1)),
                  pl.BlockSpec((tm, d), row), pl.BlockSpec((1, d), fix),
                  pl.BlockSpec((1, d), fix)],
        out_specs=out_specs,
        out_shape=out_shape,
        compiler_params=_params(("parallel",)),
        name=name,
    )(a, w, x, gpost.reshape(1, d), gpre.reshape(1, d))
    return outs if emit_h else (outs[0], None)


def _gelu_tanh(x):
    return 0.5 * x * (1.0 + jnp.tanh(0.7978845608028654 * (x + 0.044715 * x * x * x)))


def _ffn_act_kernel(a_ref, ap_ref, an_ref, wg_ref, wu_ref, cw_ref, cb_ref, o_ref, ax_ref, *,
                    tiles_per_seq, halo):
    tm = a_ref.shape[0]
    tc = o_ref.shape[1]

    pos = pl.program_id(0) % tiles_per_seq

    @pl.when(pl.program_id(1) == 0)
    def _():
        zero = jnp.zeros_like(ap_ref[...])
        ax_ref[0:halo] = jnp.where(pos == 0, zero, ap_ref[...])
        ax_ref[halo:halo + tm] = a_ref[...]
        ax_ref[halo + tm:] = jnp.where(pos == tiles_per_seq - 1, zero, an_ref[...])

    g = jnp.dot(ax_ref[...], wg_ref[...], preferred_element_type=F32)
    u = jnp.dot(ax_ref[halo:halo + tm], wu_ref[...], preferred_element_type=F32)
    g_prev = pltpu.roll(g, 1, 0)[halo:halo + tm]
    g_next = pltpu.roll(g, tm + 2 * halo - 1, 0)[halo:halo + tm]
    cw = cw_ref[...]
    conv = g_prev * cw[0:1] + g[halo:halo + tm] * cw[1:2] + g_next * cw[2:3] + cb_ref[...]
    o_ref[...] = (_gelu_tanh(conv) * u).astype(o_ref.dtype)


def _ffn_act(h, w_gate, w_up, cw, cb, seq, tm, tc):
    t, d = h.shape
    n = cw.shape[1]
    tm = min(tm, seq)
    halo = 16
    tiles_per_seq = seq // tm
    hb = tm // halo
    last_hb = t // halo - 1
    return pl.pallas_call(
        functools.partial(_ffn_act_kernel, tiles_per_seq=tiles_per_seq, halo=halo),
        grid=(t // tm, n // tc),
        in_specs=[pl.BlockSpec((tm, d), lambda i, j: (i, 0)),
                  pl.BlockSpec((halo, d), lambda i, j: (jnp.maximum(i * hb - 1, 0), 0)),
                  pl.BlockSpec((halo, d), lambda i, j: (jnp.minimum((i + 1) * hb, last_hb), 0)),
                  pl.BlockSpec((d, tc), lambda i, j: (0, j)),
                  pl.BlockSpec((d, tc), lambda i, j: (0, j)),
                  pl.BlockSpec((3, tc), lambda i, j: (0, j)),
                  pl.BlockSpec((1, tc), lambda i, j: (0, j))],
        out_specs=pl.BlockSpec((tm, tc), lambda i, j: (i, j)),
        out_shape=jax.ShapeDtypeStruct((t, n), BF16),
        scratch_shapes=[pltpu.VMEM((tm + 2 * halo, d), BF16)],
        compiler_params=_params(("parallel", "arbitrary")),
        name="ffn_gate_up_act",
    )(h, h, h, w_gate, w_up, cw, cb.reshape(1, n))


def _pack_weights(p, l):
    s0 = FOURIER_DIM
    s1 = s0 + Q_LORA
    s2 = s1 + KV_LORA
    s3 = s2 + QK_ROPE
    s4 = s3 + MEM_DIM
    w_in = p["w_in"][l]
    d = w_in.shape[0]
    bf = lambda a: a.astype(BF16)
    w_gq = bf(jnp.concatenate([w_in[:, s4:], w_in[:, s3:s4]], axis=1))
    w_f = bf(w_in[:, :s0])
    w_lat = bf(jnp.concatenate([w_in[:, s0:s3], jnp.zeros((d, KPE_PAD - QK_ROPE), F32)],
                               axis=1))
    wq = p["w_uq"][l].reshape(Q_LORA, MLA_HEADS, QK_NOPE + QK_ROPE)
    wq = jnp.pad(wq, ((0, 0), (0, 0), (0, QK_PAD - QK_NOPE - QK_ROPE)))
    wq = bf(wq.reshape(Q_LORA, MLA_HEADS * QK_PAD))
    wkv = p["w_ukv"][l].reshape(KV_LORA, MLA_HEADS, QK_NOPE + V_HEAD)
    wk = bf(wkv[:, :, :QK_NOPE].reshape(KV_LORA, MLA_HEADS * QK_NOPE))
    wvt = bf(wkv[:, :, QK_NOPE:].reshape(KV_LORA, MLA_HEADS * V_HEAD).T)
    return dict(
        w_gq=w_gq, w_f=w_f, w_lat=w_lat, wq=wq, wk=wk, wvt=wvt,
        w_gate=bf(p["w_ffn_gate"][l]), w_up=bf(p["w_ffn_up"][l]),
        w_mem_kv=bf(p["w_mem_kv"][l]), w_fo=bf(p["w_fourier_out"][l]),
        w_ao=bf(p["w_attn_out"][l]), w_mo=bf(p["w_mem_out"][l]), w_o=bf(p["w_o"][l]),
        w_down=bf(p["w_ffn_down"][l]), conv_w=p["ffn_conv_w"][l], conv_b=p["ffn_conv_b"][l],
        q_norm=p["q_norm"][l], kv_norm=p["kv_norm"][l], mem_norm=p["mem_norm"][l],
        pre_mix=p["pre_mix_norm"][l], post_mix=p["post_mix_norm"][l],
        pre_ffn=p["pre_ffn_norm"][l], post_ffn=p["post_ffn_norm"][l])


def _rope_tables(seq):
    inv_freq = 1.0 / (ROPE_THETA ** (jnp.arange(0, QK_ROPE, 2, dtype=F32) / QK_ROPE))
    ang = jnp.arange(seq, dtype=F32)[:, None] * inv_freq[None, :]
    cos, sin = jnp.cos(ang), jnp.sin(ang)
    half = QK_ROPE // 2
    zh = jnp.zeros((seq, half), F32)
    zr = jnp.zeros((seq, LANES - QK_ROPE), F32)
    c = jnp.concatenate([cos, cos, zr], axis=1)
    s1 = jnp.concatenate([-sin, zh, zr], axis=1)
    s2 = jnp.concatenate([zh, sin, zr], axis=1)
    return c, s1, s2


def _trunk(x, mem, layers):
    b, s, d = x.shape
    t = b * s
    n_mem = mem.shape[1]
    tabs = _rope_tables(s)
    q_scale = (QK_NOPE + QK_ROPE) ** -0.5 * LOG2E
    gate_w = 3 * D_MODEL
    xf = x.reshape(t, d)
    memf = mem.reshape(b * n_mem, d)
    h = None
    for l, w in enumerate(layers):
        if l == 0:
            h, z1 = _norm_mm(xf, w["pre_mix"], w["w_gq"], 1024, 1024, "in_proj_gates")
        else:
            z1 = _mm(h, w["w_gq"], 1024, 1024, "in_proj_gates")
        f_in = _mm(h, w["w_f"], 1024, 1024, "in_proj_fourier")
        q, kc, vt = _attn_prep(h, w, tabs, b, s, 256, q_scale)
        oa = _mla(q.reshape(b, s, -1), kc.reshape(b, s, -1), vt, 2048, 1024, 256)
        mem_n = _norm(memf, w["mem_norm"], n_mem)
        kvm = _mm(mem_n, w["w_mem_kv"], 1024, 1024, "mem_kv")
        om = _mem_attn(z1.reshape(b, s, -1), kvm.reshape(b, n_mem, -1),
                       gate_w // MEM_DIM, 1024)
        yf = _fourier(f_in.reshape(b, s, -1), 16, 16)
        merged = _merge(yf.reshape(t, -1), oa.reshape(t, -1), om.reshape(t, -1), z1,
                        w["w_fo"], w["w_ao"], w["w_mo"], 1024, 512)
        x1, h2 = _proj_norm(merged, w["w_o"], xf, w["post_mix"], w["pre_ffn"], 512, 128,
                            True, "out_proj_norm")
        act = _ffn_act(h2, w["w_gate"], w["w_up"], w["conv_w"], w["conv_b"], s, 1024,
                       FF_CHUNK)
        last = l == len(layers) - 1
        nxt = layers[0 if last else l + 1]["pre_mix"]
        xf, h = _proj_norm(act, w["w_down"], x1, w["post_ffn"], nxt, 256, 128, not last,
                           "ffn_down_norm")
    return xf.reshape(b, s, d)


def kernel(x_prompt, x_sample, mem_prompt, mem_sample, pre_mix_norm, w_in, q_norm, w_uq,
           kv_norm, w_ukv, mem_norm, w_mem_kv, w_fourier_out, w_attn_out, w_mem_out, w_o,
           post_mix_norm, pre_ffn_norm, w_ffn_gate, w_ffn_up, ffn_conv_w, ffn_conv_b,
           w_ffn_down, post_ffn_norm):
    p = dict(pre_mix_norm=pre_mix_norm, w_in=w_in, q_norm=q_norm, w_uq=w_uq,
             kv_norm=kv_norm, w_ukv=w_ukv, mem_norm=mem_norm, w_mem_kv=w_mem_kv,
             w_fourier_out=w_fourier_out, w_attn_out=w_attn_out, w_mem_out=w_mem_out,
             w_o=w_o, post_mix_norm=post_mix_norm, pre_ffn_norm=pre_ffn_norm,
             w_ffn_gate=w_ffn_gate, w_ffn_up=w_ffn_up, ffn_conv_w=ffn_conv_w,
             ffn_conv_b=ffn_conv_b, w_ffn_down=w_ffn_down, post_ffn_norm=post_ffn_norm)
    layers = [_pack_weights(p, l) for l in range(w_in.shape[0])]
    return (_trunk(x_prompt, mem_prompt, layers), _trunk(x_sample, mem_sample, layers))
```

```python
import functools
import math

import numpy as np
import jax
import jax.numpy as jnp
from jax import lax
from jax.experimental import pallas as pl
from jax.experimental.pallas import tpu as pltpu

D_MODEL = 2048
N_GROUPS = 4
GROUP_DIM = 256
FOURIER_DIM = N_GROUPS * GROUP_DIM
MLA_HEADS = 16
Q_LORA = 512
KV_LORA = 512
QK_NOPE = 128
QK_ROPE = 64
V_HEAD = 128
ROPE_THETA = 10000.0
MEM_HEADS = 4
MEM_HEAD_DIM = 256
MEM_DIM = MEM_HEADS * MEM_HEAD_DIM
EPS = 1e-6

LANES = 128
BF16_ROWS = 16
QK_PAD = 256
KPE_PAD = LANES
V_ROWS = V_HEAD + BF16_ROWS
N_SLOTS = 3
FFT_N2 = 128
VMEM_LIMIT = 56 * 1024 * 1024
LOG2E = 1.4426950408889634

MM_ROWS, MM_COLS = 1024, 1024
PREP_ROWS = 256
ATT_Q_TILE = 2048
ATT_K_CHUNK = 256
MEM_Q_TILE = 1024
FFT_STEP = 16
MERGE_ROWS, MERGE_COLS = 1024, 512
OUT_PROJ_ROWS, DOWN_ROWS = 512, 256
NORM_SUB_ROWS = 128
FFN_ROWS, FF_CHUNK = 1024, 512

BF16 = jnp.bfloat16
F32 = jnp.float32


def _params(sem):
    return pltpu.CompilerParams(dimension_semantics=sem, vmem_limit_bytes=VMEM_LIMIT)


def _rms(x, g):
    ms = jnp.mean(x * x, axis=-1, keepdims=True)
    return x * lax.rsqrt(ms + EPS) * g


def _rope128(x, c, s1, s2):
    return x * c + pltpu.roll(x, 96, 1) * s1 + pltpu.roll(x, 32, 1) * s2


def _norm_kernel(x_ref, g_ref, o_ref):
    o_ref[...] = _rms(x_ref[...].astype(F32), g_ref[...]).astype(o_ref.dtype)


def _norm(x, g, tm):
    t, d = x.shape
    return pl.pallas_call(
        _norm_kernel,
        grid=(t // tm,),
        in_specs=[pl.BlockSpec((tm, d), lambda i: (i, 0)),
                  pl.BlockSpec((1, d), lambda i: (0, 0))],
        out_specs=pl.BlockSpec((tm, d), lambda i: (i, 0)),
        out_shape=jax.ShapeDtypeStruct((t, d), BF16),
        compiler_params=_params(("parallel",)),
        name="rmsnorm",
    )(x, g.reshape(1, d))


def _mm_kernel(a_ref, w_ref, o_ref):
    o_ref[...] = jnp.dot(a_ref[...], w_ref[...],
                         preferred_element_type=F32).astype(o_ref.dtype)


def _mm(a, w, tm, tn, name):
    m, k = a.shape
    n = w.shape[1]
    tm = min(tm, m)
    tn = min(tn, n)
    return pl.pallas_call(
        _mm_kernel,
        grid=(m // tm, n // tn),
        in_specs=[pl.BlockSpec((tm, k), lambda i, j: (i, 0)),
                  pl.BlockSpec((k, tn), lambda i, j: (0, j))],
        out_specs=pl.BlockSpec((tm, tn), lambda i, j: (i, j)),
        out_shape=jax.ShapeDtypeStruct((m, n), BF16),
        compiler_params=_params(("parallel", "arbitrary")),
        name=name,
    )(a, w)


def _norm_mm_kernel(x_ref, g_ref, w_ref, h_ref, o_ref):
    @pl.when(pl.program_id(1) == 0)
    def _():
        h_ref[...] = _rms(x_ref[...], g_ref[...]).astype(h_ref.dtype)

    o_ref[...] = jnp.dot(h_ref[...], w_ref[...],
                         preferred_element_type=F32).astype(o_ref.dtype)


def _norm_mm(x, g, w, tm, tn, name):
    m, k = x.shape
    n = w.shape[1]
    tm = min(tm, m)
    return pl.pallas_call(
        _norm_mm_kernel,
        grid=(m // tm, n // tn),
        in_specs=[pl.BlockSpec((tm, k), lambda i, j: (i, 0)),
                  pl.BlockSpec((1, k), lambda i, j: (0, 0)),
                  pl.BlockSpec((k, tn), lambda i, j: (0, j))],
        out_specs=[pl.BlockSpec((tm, k), lambda i, j: (i, 0)),
                   pl.BlockSpec((tm, tn), lambda i, j: (i, j))],
        out_shape=[jax.ShapeDtypeStruct((m, k), BF16), jax.ShapeDtypeStruct((m, n), BF16)],
        compiler_params=_params(("parallel", "arbitrary")),
        name=name,
    )(x, g.reshape(1, k), w)


def _attn_prep_kernel(h_ref, wl_ref, qg_ref, kvg_ref, wq_ref, wk_ref, wvt_ref,
                      c_ref, s1_ref, s2_ref, q_ref, k_ref, vt_ref, *, scale):
    c, s1, s2 = c_ref[...], s1_ref[...], s2_ref[...]
    z = jnp.dot(h_ref[...], wl_ref[...], preferred_element_type=F32)
    cq = _rms(z[:, :Q_LORA], qg_ref[...]).astype(BF16)
    lat = _rms(z[:, Q_LORA:Q_LORA + KV_LORA], kvg_ref[...]).astype(BF16)
    kpe = _rope128(z[:, Q_LORA + KV_LORA:], c, s1, s2).astype(BF16)
    zq = jnp.dot(cq, wq_ref[...], preferred_element_type=F32) * scale
    kn = jnp.dot(lat, wk_ref[...], preferred_element_type=F32).astype(BF16)
    vt = lax.dot_general(wvt_ref[...], lat, (((1,), (1,)), ((), ())),
                         preferred_element_type=F32).astype(BF16)
    pad_rows = V_ROWS - V_HEAD
    row = lax.broadcasted_iota(jnp.int32, (pad_rows, vt.shape[1]), 0)
    ones_rows = jnp.where(row == 0, 1.0, 0.0).astype(BF16)
    for h in range(MLA_HEADS):
        lo = h * QK_PAD
        q_ref[:, lo:lo + QK_NOPE] = zq[:, lo:lo + QK_NOPE].astype(BF16)
        pe = _rope128(zq[:, lo + QK_NOPE:lo + QK_PAD], c, s1, s2)
        q_ref[:, lo + QK_NOPE:lo + QK_PAD] = pe.astype(BF16)
        k_ref[:, lo:lo + QK_NOPE] = kn[:, h * QK_NOPE:(h + 1) * QK_NOPE]
        k_ref[:, lo + QK_NOPE:lo + QK_PAD] = kpe
        vt_ref[h * V_ROWS:h * V_ROWS + V_HEAD, :] = vt[h * V_HEAD:(h + 1) * V_HEAD, :]
        vt_ref[h * V_ROWS + V_HEAD:(h + 1) * V_ROWS, :] = ones_rows


def _attn_prep(h, w, tabs, b, seq, tm, scale):
    d = h.shape[1]
    tm = min(tm, seq)
    nblk = seq // tm
    nq = MLA_HEADS * QK_PAD
    nv = MLA_HEADS * V_ROWS
    row = lambda bi, i: (bi * nblk + i, 0)
    fix = lambda bi, i: (0, 0)
    tab = pl.BlockSpec((tm, LANES), lambda bi, i: (i, 0))
    def wspec(arr):
        return pl.BlockSpec(arr.shape, fix, pipeline_mode=pl.Buffered(1))
    return pl.pallas_call(
        functools.partial(_attn_prep_kernel, scale=scale),
        grid=(b, nblk),
        in_specs=[pl.BlockSpec((tm, d), row), wspec(w["w_lat"]),
                  pl.BlockSpec((1, Q_LORA), fix), pl.BlockSpec((1, KV_LORA), fix),
                  wspec(w["wq"]), wspec(w["wk"]), wspec(w["wvt"]), tab, tab, tab],
        out_specs=[pl.BlockSpec((tm, nq), row), pl.BlockSpec((tm, nq), row),
                   pl.BlockSpec((None, nv, tm), lambda bi, i: (bi, 0, i))],
        out_shape=[jax.ShapeDtypeStruct((b * seq, nq), BF16),
                   jax.ShapeDtypeStruct((b * seq, nq), BF16),
                   jax.ShapeDtypeStruct((b, nv, seq), BF16)],
        compiler_params=_params(("parallel", "arbitrary")),
        name="attn_prep",
    )(h, w["w_lat"], w["q_norm"].reshape(1, -1), w["kv_norm"].reshape(1, -1),
      w["wq"], w["wk"], w["wvt"], *tabs)


def _mla_kernel(q_ref, k_ref, vt_ref, o_ref, s_ref, p_ref, qt_ref, *, tq, tk):
    n_q = q_ref.shape[0] // tq
    n_k = k_ref.shape[0] // tk

    def q_tile(qi, carry):
        q0 = pl.multiple_of(qi * tq, tq)
        qt_ref[...] = q_ref[pl.ds(q0, tq), :].T

        def scores(c, slot):
            k = k_ref[pl.ds(pl.multiple_of(c * tk, tk), tk), :]
            s = jnp.dot(k, qt_ref[...], preferred_element_type=F32)
            s_ref[slot] = s
            return jnp.max(s, axis=0, keepdims=True)

        def softmax(slot, cmax, m):
            m_new = jnp.maximum(m, cmax)
            p_ref[slot] = jnp.exp2((s_ref[slot] - m_new).astype(BF16))
            return m_new, jnp.exp2(m - m_new)

        def pv(c, slot):
            vt = vt_ref[:, pl.ds(pl.multiple_of(c * tk, tk), tk)]
            return jnp.dot(vt, p_ref[slot], preferred_element_type=F32)

        def stage(c, u, m, acc, cmax):
            part = pv(c - 1, (u - 1) % N_SLOTS)
            cmax_next = scores(c + 1, (u + 1) % N_SLOTS)
            m, alpha = softmax(u % N_SLOTS, cmax, m)
            return m, (acc + part) * alpha, cmax_next

        cmax = scores(0, 0)
        m, _ = softmax(0, cmax, jnp.full((1, tq), -1e30, F32))
        cmax = scores(1, 1)
        acc = jnp.zeros((V_ROWS, tq), F32)

        n_mid = n_k - 2
        unroll = 4 * N_SLOTS
        peel = n_mid % unroll
        st = (m, acc, cmax)
        for u in range(1, peel + 1):
            st = stage(u, u, *st)

        def group(j, st):
            for u in range(1, unroll + 1):
                st = stage(peel + unroll * j + u, peel + u, *st)
            return st

        trips = n_mid // unroll
        if trips == 1:
            st = group(0, st)
        elif trips > 1:
            st = lax.fori_loop(0, trips, group, st)
        m, acc, cmax = st
        part = pv(n_k - 2, (n_k - 2) % N_SLOTS)
        m, alpha = softmax((n_k - 1) % N_SLOTS, cmax, m)
        acc = (acc + part) * alpha + pv(n_k - 1, (n_k - 1) % N_SLOTS)
        out = acc[:V_HEAD] * (1.0 / acc[V_HEAD:V_HEAD + 1])
        o_ref[pl.ds(q0, tq), :] = out.T.astype(o_ref.dtype)
        return carry

    lax.fori_loop(0, n_q, q_tile, 0)


def _mla(q, k, vt, tq_outer, tq, tk):
    b, s, _ = q.shape
    tq_outer = min(tq_outer, s)
    tq = min(tq, tq_outer)
    tk = min(tk, s // 2)
    assert (s // tk) % 2 == 0
    return pl.pallas_call(
        functools.partial(_mla_kernel, tq=tq, tk=tk),
        grid=(b, MLA_HEADS, s // tq_outer),
        in_specs=[pl.BlockSpec((None, tq_outer, QK_PAD), lambda bi, h, qi: (bi, qi, h)),
                  pl.BlockSpec((None, s, QK_PAD), lambda bi, h, qi: (bi, 0, h)),
                  pl.BlockSpec((None, V_ROWS, s), lambda bi, h, qi: (bi, h, 0))],
        out_specs=pl.BlockSpec((None, tq_outer, V_HEAD), lambda bi, h, qi: (bi, qi, h)),
        out_shape=jax.ShapeDtypeStruct((b, s, MLA_HEADS * V_HEAD), BF16),
        scratch_shapes=[pltpu.VMEM((N_SLOTS, tk, tq), F32),
                        pltpu.VMEM((N_SLOTS, tk, tq), BF16),
                        pltpu.VMEM((QK_PAD, tq), BF16)],
        compiler_params=_params(("parallel", "parallel", "arbitrary")),
        name="mla_attention",
    )(q, k, vt)


def _mem_kernel(q_ref, kv_ref, o_ref, *, scale):
    for h in range(MEM_HEADS):
        lo = h * MEM_HEAD_DIM
        q = q_ref[:, lo:lo + MEM_HEAD_DIM]
        k = kv_ref[:, lo:lo + MEM_HEAD_DIM]
        v = kv_ref[:, MEM_DIM + lo:MEM_DIM + lo + MEM_HEAD_DIM]
        s = lax.dot_general(q, k, (((1,), (1,)), ((), ())),
                            preferred_element_type=F32) * scale
        p = jnp.exp2(s - jnp.max(s, axis=-1, keepdims=True))
        l = jnp.sum(p, axis=-1, keepdims=True)
        o = jnp.dot(p.astype(BF16), v, preferred_element_type=F32)
        o_ref[:, lo:lo + MEM_HEAD_DIM] = (o * (1.0 / l)).astype(o_ref.dtype)


def _mem_attn(z1, kvm, q_blk, tq):
    b, s, _ = z1.shape
    n_mem = kvm.shape[1]
    tq = min(tq, s)
    scale = MEM_HEAD_DIM ** -0.5 * LOG2E
    return pl.pallas_call(
        functools.partial(_mem_kernel, scale=scale),
        grid=(b, s // tq),
        in_specs=[pl.BlockSpec((None, tq, MEM_DIM), lambda bi, i: (bi, i, q_blk)),
                  pl.BlockSpec((None, n_mem, 2 * MEM_DIM), lambda bi, i: (bi, 0, 0))],
        out_specs=pl.BlockSpec((None, tq, MEM_DIM), lambda bi, i: (bi, i, 0)),
        out_shape=jax.ShapeDtypeStruct((b, s, MEM_DIM), BF16),
        compiler_params=_params(("parallel", "arbitrary")),
        name="mem_attention",
    )(z1, kvm)


def _fft_a_kernel(u_ref, f_ref, tc_ref, ts_ref, o_ref, *, n1):
    f = f_ref[...]
    for j in range(u_ref.shape[0]):
        r = jnp.dot(f, u_ref[j], preferred_element_type=F32)
        br, bi = r[:n1], r[n1:]
        tc, ts = tc_ref[j], ts_ref[j]
        o_ref[j, 0] = (br * tc + bi * ts).astype(o_ref.dtype)
        o_ref[j, 1] = (bi * tc - br * ts).astype(o_ref.dtype)


def _fft_b_kernel(b_ref, m_ref, cs_ref, o_ref):
    m2 = m_ref[...]
    cs = cs_ref[...]
    n2 = o_ref.shape[1]
    for j in range(b_ref.shape[0]):
        x = b_ref[j].reshape(2 * n2, FOURIER_DIM)
        z = jnp.dot(m2, x, preferred_element_type=F32).astype(BF16)
        for g in range(N_GROUPS):
            lo = g * GROUP_DIM
            zz = jnp.concatenate([z[:n2, lo:lo + GROUP_DIM], z[n2:, lo:lo + GROUP_DIM]],
                                 axis=1)
            y = jnp.dot(zz, cs, preferred_element_type=F32)
            o_ref[j, :, lo:lo + GROUP_DIM] = y.astype(o_ref.dtype)


def _dft_tables(seq):
    n2 = FFT_N2
    n1 = seq // n2
    a1 = 2.0 * np.pi * np.outer(np.arange(n1), np.arange(n1)) / n1
    f1 = np.concatenate([np.cos(a1), -np.sin(a1)], axis=0) / math.sqrt(n1)
    th = 2.0 * np.pi * np.outer(np.arange(n2), np.arange(n1)) / seq
    a2 = 2.0 * np.pi * np.outer(np.arange(n2), np.arange(n2)) / n2
    c2, s2 = np.cos(a2), np.sin(a2)
    m2 = np.block([[c2, s2], [-s2, c2]]) / math.sqrt(n2)
    ac = 2.0 * np.pi * np.outer(np.arange(GROUP_DIM), np.arange(GROUP_DIM)) / GROUP_DIM
    cs = np.concatenate([np.cos(ac), np.sin(ac)], axis=0) / math.sqrt(GROUP_DIM)
    return (jnp.asarray(f1, BF16), jnp.asarray(np.cos(th)[:, :, None], F32),
            jnp.asarray(np.sin(th)[:, :, None], F32), jnp.asarray(m2, BF16),
            jnp.asarray(cs, BF16))


def _fourier(u, t2, tk1):
    b, s, c = u.shape
    n2 = FFT_N2
    n1 = s // n2
    t2 = min(t2, n2)
    tk1 = min(tk1, n1)
    f1, tc, ts, m2, cs = _dft_tables(s)
    ut = u.reshape(b, n1, n2, c).transpose(0, 2, 1, 3)
    bt = pl.pallas_call(
        functools.partial(_fft_a_kernel, n1=n1),
        grid=(b, n2 // t2),
        in_specs=[pl.BlockSpec((None, t2, n1, c), lambda bi, i: (bi, i, 0, 0)),
                  pl.BlockSpec((2 * n1, n1), lambda bi, i: (0, 0)),
                  pl.BlockSpec((t2, n1, 1), lambda bi, i: (i, 0, 0)),
                  pl.BlockSpec((t2, n1, 1), lambda bi, i: (i, 0, 0))],
        out_specs=pl.BlockSpec((None, t2, 2, n1, c), lambda bi, i: (bi, i, 0, 0, 0)),
        out_shape=jax.ShapeDtypeStruct((b, n2, 2, n1, c), BF16),
        compiler_params=_params(("parallel", "arbitrary")),
        name="fft_stage_a",
    )(ut, f1, tc, ts)
    bk = bt.transpose(0, 3, 2, 1, 4)
    y = pl.pallas_call(
        _fft_b_kernel,
        grid=(b, n1 // tk1),
        in_specs=[pl.BlockSpec((None, tk1, 2, n2, c), lambda bi, i: (bi, i, 0, 0, 0)),
                  pl.BlockSpec((2 * n2, 2 * n2), lambda bi, i: (0, 0)),
                  pl.BlockSpec((2 * GROUP_DIM, GROUP_DIM), lambda bi, i: (0, 0))],
        out_specs=pl.BlockSpec((None, tk1, n2, c), lambda bi, i: (bi, i, 0, 0)),
        out_shape=jax.ShapeDtypeStruct((b, n1, n2, c), BF16),
        compiler_params=_params(("parallel", "arbitrary")),
        name="fft_stage_b",
    )(bk, m2, cs)
    return y.transpose(0, 2, 1, 3).reshape(b, s, c)


def _merge_kernel(yf_ref, oa_ref, om_ref, g0_ref, g1_ref, g2_ref,
                  wf_ref, wa_ref, wm_ref, o_ref):
    def gate(g_ref):
        return jax.nn.sigmoid(g_ref[...].astype(F32))
    yf = jnp.dot(yf_ref[...], wf_ref[...], preferred_element_type=F32)
    ya = jnp.dot(oa_ref[...], wa_ref[...], preferred_element_type=F32)
    ym = jnp.dot(om_ref[...], wm_ref[...], preferred_element_type=F32)
    o_ref[...] = (gate(g0_ref) * yf + gate(g1_ref) * ya + gate(g2_ref) * ym
                  ).astype(o_ref.dtype)


def _merge(yf, oa, om, z1, wf, wa, wm, tm, tn):
    t = yf.shape[0]
    tm = min(tm, t)
    nb = D_MODEL // tn
    def gspec(k):
        return pl.BlockSpec((tm, tn), lambda i, j: (i, k * nb + j))
    def aspec(w):
        return pl.BlockSpec((tm, w), lambda i, j: (i, 0))
    def wspec(w):
        return pl.BlockSpec((w, tn), lambda i, j: (0, j))
    return pl.pallas_call(
        _merge_kernel,
        grid=(t // tm, nb),
        in_specs=[aspec(FOURIER_DIM), aspec(MLA_HEADS * V_HEAD), aspec(MEM_DIM),
                  gspec(0), gspec(1), gspec(2),
                  wspec(FOURIER_DIM), wspec(MLA_HEADS * V_HEAD), wspec(MEM_DIM)],
        out_specs=pl.BlockSpec((tm, tn), lambda i, j: (i, j)),
        out_shape=jax.ShapeDtypeStruct((t, D_MODEL), BF16),
        compiler_params=_params(("parallel", "arbitrary")),
        name="branch_merge",
    )(yf, oa, om, z1, z1, z1, wf, wa, wm)


def _proj_norm_kernel(a_ref, w_ref, x_ref, gpost_ref, gpre_ref, x1_ref, *h_ref, ts):
    for r0 in range(0, a_ref.shape[0], ts):
        rows = slice(r0, r0 + ts)
        y = jnp.dot(a_ref[rows, :], w_ref[...], preferred_element_type=F32)
        x1 = x_ref[rows, :] + _rms(y, gpost_ref[...])
        x1_ref[rows, :] = x1
        if h_ref:
            h_ref[0][rows, :] = _rms(x1, gpre_ref[...]).astype(h_ref[0].dtype)


def _proj_norm(a, w, x, gpost, gpre, tm, ts, emit_h, name):
    t, d = x.shape
    k = a.shape[1]
    tm = min(tm, t)
    row = lambda i: (i, 0)
    fix = lambda i: (0, 0)
    out_specs = [pl.BlockSpec((tm, d), row)]
    out_shape = [jax.ShapeDtypeStruct((t, d), F32)]
    if emit_h:
        out_specs.append(pl.BlockSpec((tm, d), row))
        out_shape.append(jax.ShapeDtypeStruct((t, d), BF16))
    outs = pl.pallas_call(
        functools.partial(_proj_norm_kernel, ts=min(ts, tm)),
        grid=(t // tm,),
        in_specs=[pl.BlockSpec((tm, k), row),
                  pl.BlockSpec((k, d), fix, pipeline_mode=pl.Buffered(1)),
                  pl.BlockSpec((tm, d), row), pl.BlockSpec((1, d), fix),
                  pl.BlockSpec((1, d), fix)],
        out_specs=out_specs,
        out_shape=out_shape,
        compiler_params=_params(("parallel",)),
        name=name,
    )(a, w, x, gpost.reshape(1, d), gpre.reshape(1, d))
    return outs if emit_h else (outs[0], None)


def _gelu_tanh(x):
    return 0.5 * x * (1.0 + jnp.tanh(0.7978845608028654 * (x + 0.044715 * x * x * x)))


def _ffn_act_kernel(a_ref, ap_ref, an_ref, wg_ref, wu_ref, cw_ref, cb_ref, o_ref, ax_ref, *,
                    tiles_per_seq, halo):
    tm = a_ref.shape[0]
    tc = o_ref.shape[1]

    pos = pl.program_id(0) % tiles_per_seq

    @pl.when(pl.program_id(1) == 0)
    def _():
        zero = jnp.zeros_like(ap_ref[...])
        ax_ref[0:halo] = jnp.where(pos == 0, zero, ap_ref[...])
        ax_ref[halo:halo + tm] = a_ref[...]
        ax_ref[halo + tm:] = jnp.where(pos == tiles_per_seq - 1, zero, an_ref[...])

    g = jnp.dot(ax_ref[...], wg_ref[...], preferred_element_type=F32)
    u = jnp.dot(ax_ref[halo:halo + tm], wu_ref[...], preferred_element_type=F32)
    g_prev = pltpu.roll(g, 1, 0)[halo:halo + tm]
    g_next = pltpu.roll(g, tm + 2 * halo - 1, 0)[halo:halo + tm]
    cw = cw_ref[...]
    conv = g_prev * cw[0:1] + g[halo:halo + tm] * cw[1:2] + g_next * cw[2:3] + cb_ref[...]
    o_ref[...] = (_gelu_tanh(conv) * u).astype(o_ref.dtype)


def _ffn_act(h, w_gate, w_up, cw, cb, seq, tm, tc):
    t, d = h.shape
    n = cw.shape[1]
    tm = min(tm, seq)
    halo = BF16_ROWS
    tiles_per_seq = seq // tm
    hb = tm // halo
    last_hb = t // halo - 1
    return pl.pallas_call(
        functools.partial(_ffn_act_kernel, tiles_per_seq=tiles_per_seq, halo=halo),
        grid=(t // tm, n // tc),
        in_specs=[pl.BlockSpec((tm, d), lambda i, j: (i, 0)),
                  pl.BlockSpec((halo, d), lambda i, j: (jnp.maximum(i * hb - 1, 0), 0)),
                  pl.BlockSpec((halo, d), lambda i, j: (jnp.minimum((i + 1) * hb, last_hb), 0)),
                  pl.BlockSpec((d, tc), lambda i, j: (0, j)),
                  pl.BlockSpec((d, tc), lambda i, j: (0, j)),
                  pl.BlockSpec((3, tc), lambda i, j: (0, j)),
                  pl.BlockSpec((1, tc), lambda i, j: (0, j))],
        out_specs=pl.BlockSpec((tm, tc), lambda i, j: (i, j)),
        out_shape=jax.ShapeDtypeStruct((t, n), BF16),
        scratch_shapes=[pltpu.VMEM((tm + 2 * halo, d), BF16)],
        compiler_params=_params(("parallel", "arbitrary")),
        name="ffn_gate_up_act",
    )(h, h, h, w_gate, w_up, cw, cb.reshape(1, n))


def _pack_weights(p, l):
    s0 = FOURIER_DIM
    s1 = s0 + Q_LORA
    s2 = s1 + KV_LORA
    s3 = s2 + QK_ROPE
    s4 = s3 + MEM_DIM
    w_in = p["w_in"][l]
    d = w_in.shape[0]
    bf = lambda a: a.astype(BF16)
    w_gq = bf(jnp.concatenate([w_in[:, s4:], w_in[:, s3:s4]], axis=1))
    w_f = bf(w_in[:, :s0])
    w_lat = bf(jnp.concatenate([w_in[:, s0:s3], jnp.zeros((d, KPE_PAD - QK_ROPE), F32)],
                               axis=1))
    wq = p["w_uq"][l].reshape(Q_LORA, MLA_HEADS, QK_NOPE + QK_ROPE)
    wq = jnp.pad(wq, ((0, 0), (0, 0), (0, QK_PAD - QK_NOPE - QK_ROPE)))
    wq = bf(wq.reshape(Q_LORA, MLA_HEADS * QK_PAD))
    wkv = p["w_ukv"][l].reshape(KV_LORA, MLA_HEADS, QK_NOPE + V_HEAD)
    wk = bf(wkv[:, :, :QK_NOPE].reshape(KV_LORA, MLA_HEADS * QK_NOPE))
    wvt = bf(wkv[:, :, QK_NOPE:].reshape(KV_LORA, MLA_HEADS * V_HEAD).T)
    return dict(
        w_gq=w_gq, w_f=w_f, w_lat=w_lat, wq=wq, wk=wk, wvt=wvt,
        w_gate=bf(p["w_ffn_gate"][l]), w_up=bf(p["w_ffn_up"][l]),
        w_mem_kv=bf(p["w_mem_kv"][l]), w_fo=bf(p["w_fourier_out"][l]),
        w_ao=bf(p["w_attn_out"][l]), w_mo=bf(p["w_mem_out"][l]), w_o=bf(p["w_o"][l]),
        w_down=bf(p["w_ffn_down"][l]), conv_w=p["ffn_conv_w"][l], conv_b=p["ffn_conv_b"][l],
        q_norm=p["q_norm"][l], kv_norm=p["kv_norm"][l], mem_norm=p["mem_norm"][l],
        pre_mix=p["pre_mix_norm"][l], post_mix=p["post_mix_norm"][l],
        pre_ffn=p["pre_ffn_norm"][l], post_ffn=p["post_ffn_norm"][l])


def _rope_tables(seq):
    inv_freq = 1.0 / (ROPE_THETA ** (jnp.arange(0, QK_ROPE, 2, dtype=F32) / QK_ROPE))
    ang = jnp.arange(seq, dtype=F32)[:, None] * inv_freq[None, :]
    cos, sin = jnp.cos(ang), jnp.sin(ang)
    half = QK_ROPE // 2
    zh = jnp.zeros((seq, half), F32)
    zr = jnp.zeros((seq, LANES - QK_ROPE), F32)
    c = jnp.concatenate([cos, cos, zr], axis=1)
    s1 = jnp.concatenate([-sin, zh, zr], axis=1)
    s2 = jnp.concatenate([zh, sin, zr], axis=1)
    return c, s1, s2


def _trunk(x, mem, layers):
    b, s, d = x.shape
    t = b * s
    n_mem = mem.shape[1]
    tabs = _rope_tables(s)
    q_scale = (QK_NOPE + QK_ROPE) ** -0.5 * LOG2E
    gate_w = 3 * D_MODEL
    xf = x.reshape(t, d)
    memf = mem.reshape(b * n_mem, d)
    h = None
    for l, w in enumerate(layers):
        if l == 0:
            h, z1 = _norm_mm(xf, w["pre_mix"], w["w_gq"], MM_ROWS, MM_COLS, "in_proj_gates")
        else:
            z1 = _mm(h, w["w_gq"], MM_ROWS, MM_COLS, "in_proj_gates")
        f_in = _mm(h, w["w_f"], MM_ROWS, MM_COLS, "in_proj_fourier")
        q, kc, vt = _attn_prep(h, w, tabs, b, s, PREP_ROWS, q_scale)
        oa = _mla(q.reshape(b, s, -1), kc.reshape(b, s, -1), vt, ATT_Q_TILE, ATT_Q_TILE,
                  ATT_K_CHUNK)
        mem_n = _norm(memf, w["mem_norm"], n_mem)
        kvm = _mm(mem_n, w["w_mem_kv"], MM_ROWS, MM_COLS, "mem_kv")
        om = _mem_attn(z1.reshape(b, s, -1), kvm.reshape(b, n_mem, -1),
                       gate_w // MEM_DIM, MEM_Q_TILE)
        yf = _fourier(f_in.reshape(b, s, -1), FFT_STEP, FFT_STEP)
        merged = _merge(yf.reshape(t, -1), oa.reshape(t, -1), om.reshape(t, -1), z1,
                        w["w_fo"], w["w_ao"], w["w_mo"], MERGE_ROWS, MERGE_COLS)
        x1, h2 = _proj_norm(merged, w["w_o"], xf, w["post_mix"], w["pre_ffn"],
                            OUT_PROJ_ROWS, NORM_SUB_ROWS, True, "out_proj_norm")
        act = _ffn_act(h2, w["w_gate"], w["w_up"], w["conv_w"], w["conv_b"], s, FFN_ROWS,
                       FF_CHUNK)
        last = l == len(layers) - 1
        nxt = layers[0 if last else l + 1]["pre_mix"]
        xf, h = _proj_norm(act, w["w_down"], x1, w["post_ffn"], nxt, DOWN_ROWS,
                           NORM_SUB_ROWS, not last, "ffn_down_norm")
    return xf.reshape(b, s, d)


def kernel(x_prompt, x_sample, mem_prompt, mem_sample, pre_mix_norm, w_in, q_norm, w_uq,
           kv_norm, w_ukv, mem_norm, w_mem_kv, w_fourier_out, w_attn_out, w_mem_out, w_o,
           post_mix_norm, pre_ffn_norm, w_ffn_gate, w_ffn_up, ffn_conv_w, ffn_conv_b,
           w_ffn_down, post_ffn_norm):
    p = dict(pre_mix_norm=pre_mix_norm, w_in=w_in, q_norm=q_norm, w_uq=w_uq,
             kv_norm=kv_norm, w_ukv=w_ukv, mem_norm=mem_norm, w_mem_kv=w_mem_kv,
             w_fourier_out=w_fourier_out, w_attn_out=w_attn_out, w_mem_out=w_mem_out,
             w_o=w_o, post_mix_norm=post_mix_norm, pre_ffn_norm=pre_ffn_norm,
             w_ffn_gate=w_ffn_gate, w_ffn_up=w_ffn_up, ffn_conv_w=ffn_conv_w,
             ffn_conv_b=ffn_conv_b, w_ffn_down=w_ffn_down, post_ffn_norm=post_ffn_norm)
    layers = [_pack_weights(p, l) for l in range(w_in.shape[0])]
    return (_trunk(x_prompt, mem_prompt, layers), _trunk(x_sample, mem_sample, layers))
```

```python
import functools
import math

import numpy as np
import jax
import jax.numpy as jnp
from jax import lax
from jax.experimental import pallas as pl
from jax.experimental.pallas import tpu as pltpu

D_MODEL = 2048
N_GROUPS = 4
GROUP_DIM = 256
FOURIER_DIM = N_GROUPS * GROUP_DIM
MLA_HEADS = 16
Q_LORA = 512
KV_LORA = 512
QK_NOPE = 128
QK_ROPE = 64
V_HEAD = 128
ROPE_THETA = 10000.0
MEM_HEADS = 4
MEM_HEAD_DIM = 256
MEM_DIM = MEM_HEADS * MEM_HEAD_DIM
EPS = 1e-6

LANES = 128
BF16_ROWS = 16
QK_PAD = 256
KPE_PAD = LANES
V_ROWS = V_HEAD + BF16_ROWS
N_SLOTS = 3
FFT_N2 = 128
VMEM_LIMIT = 56 * 1024 * 1024
LOG2E = 1.4426950408889634

MM_ROWS, MM_COLS = 1024, 1024
PREP_ROWS = 256
ATT_Q_BLOCK = 2048
ATT_Q_TILE_LONG = 1024
ATT_K_CHUNK = 256
MEM_Q_TILE = 1024
FFT_STEP = 16
MERGE_ROWS, MERGE_COLS = 1024, 512
OUT_PROJ_ROWS, DOWN_ROWS = 512, 256
NORM_SUB_ROWS = 128
FFN_ROWS, FF_CHUNK = 1024, 512

BF16 = jnp.bfloat16
F32 = jnp.float32


def _params(sem):
    return pltpu.CompilerParams(dimension_semantics=sem, vmem_limit_bytes=VMEM_LIMIT)


def _rms(x, g):
    ms = jnp.mean(x * x, axis=-1, keepdims=True)
    return x * lax.rsqrt(ms + EPS) * g


def _rope128(x, c, s1, s2):
    return x * c + pltpu.roll(x, 96, 1) * s1 + pltpu.roll(x, 32, 1) * s2


def _norm_kernel(x_ref, g_ref, o_ref):
    o_ref[...] = _rms(x_ref[...].astype(F32), g_ref[...]).astype(o_ref.dtype)


def _norm(x, g, tm):
    t, d = x.shape
    return pl.pallas_call(
        _norm_kernel,
        grid=(t // tm,),
        in_specs=[pl.BlockSpec((tm, d), lambda i: (i, 0)),
                  pl.BlockSpec((1, d), lambda i: (0, 0))],
        out_specs=pl.BlockSpec((tm, d), lambda i: (i, 0)),
        out_shape=jax.ShapeDtypeStruct((t, d), BF16),
        compiler_params=_params(("parallel",)),
        name="rmsnorm",
    )(x, g.reshape(1, d))


def _mm_kernel(a_ref, w_ref, o_ref):
    o_ref[...] = jnp.dot(a_ref[...], w_ref[...],
                         preferred_element_type=F32).astype(o_ref.dtype)


def _mm(a, w, tm, tn, name):
    m, k = a.shape
    n = w.shape[1]
    tm = min(tm, m)
    tn = min(tn, n)
    return pl.pallas_call(
        _mm_kernel,
        grid=(m // tm, n // tn),
        in_specs=[pl.BlockSpec((tm, k), lambda i, j: (i, 0)),
                  pl.BlockSpec((k, tn), lambda i, j: (0, j))],
        out_specs=pl.BlockSpec((tm, tn), lambda i, j: (i, j)),
        out_shape=jax.ShapeDtypeStruct((m, n), BF16),
        compiler_params=_params(("parallel", "arbitrary")),
        name=name,
    )(a, w)


def _norm_mm_kernel(x_ref, g_ref, w_ref, h_ref, o_ref):
    @pl.when(pl.program_id(1) == 0)
    def _():
        h_ref[...] = _rms(x_ref[...], g_ref[...]).astype(h_ref.dtype)

    o_ref[...] = jnp.dot(h_ref[...], w_ref[...],
                         preferred_element_type=F32).astype(o_ref.dtype)


def _norm_mm(x, g, w, tm, tn, name):
    m, k = x.shape
    n = w.shape[1]
    tm = min(tm, m)
    return pl.pallas_call(
        _norm_mm_kernel,
        grid=(m // tm, n // tn),
        in_specs=[pl.BlockSpec((tm, k), lambda i, j: (i, 0)),
                  pl.BlockSpec((1, k), lambda i, j: (0, 0)),
                  pl.BlockSpec((k, tn), lambda i, j: (0, j))],
        out_specs=[pl.BlockSpec((tm, k), lambda i, j: (i, 0)),
                   pl.BlockSpec((tm, tn), lambda i, j: (i, j))],
        out_shape=[jax.ShapeDtypeStruct((m, k), BF16), jax.ShapeDtypeStruct((m, n), BF16)],
        compiler_params=_params(("parallel", "arbitrary")),
        name=name,
    )(x, g.reshape(1, k), w)


def _attn_prep_kernel(h_ref, wl_ref, qg_ref, kvg_ref, wq_ref, wk_ref, wvt_ref,
                      c_ref, s1_ref, s2_ref, q_ref, k_ref, vt_ref, *, scale):
    c, s1, s2 = c_ref[...], s1_ref[...], s2_ref[...]
    z = jnp.dot(h_ref[...], wl_ref[...], preferred_element_type=F32)
    cq = _rms(z[:, :Q_LORA], qg_ref[...]).astype(BF16)
    lat = _rms(z[:, Q_LORA:Q_LORA + KV_LORA], kvg_ref[...]).astype(BF16)
    kpe = _rope128(z[:, Q_LORA + KV_LORA:], c, s1, s2).astype(BF16)
    zq = jnp.dot(cq, wq_ref[...], preferred_element_type=F32) * scale
    kn = jnp.dot(lat, wk_ref[...], preferred_element_type=F32).astype(BF16)
    vt = lax.dot_general(wvt_ref[...], lat, (((1,), (1,)), ((), ())),
                         preferred_element_type=F32).astype(BF16)
    pad_rows = V_ROWS - V_HEAD
    row = lax.broadcasted_iota(jnp.int32, (pad_rows, vt.shape[1]), 0)
    ones_rows = jnp.where(row == 0, 1.0, 0.0).astype(BF16)
    for h in range(MLA_HEADS):
        lo = h * QK_PAD
        q_ref[:, lo:lo + QK_NOPE] = zq[:, lo:lo + QK_NOPE].astype(BF16)
        pe = _rope128(zq[:, lo + QK_NOPE:lo + QK_PAD], c, s1, s2)
        q_ref[:, lo + QK_NOPE:lo + QK_PAD] = pe.astype(BF16)
        k_ref[:, lo:lo + QK_NOPE] = kn[:, h * QK_NOPE:(h + 1) * QK_NOPE]
        k_ref[:, lo + QK_NOPE:lo + QK_PAD] = kpe
        vt_ref[h * V_ROWS:h * V_ROWS + V_HEAD, :] = vt[h * V_HEAD:(h + 1) * V_HEAD, :]
        vt_ref[h * V_ROWS + V_HEAD:(h + 1) * V_ROWS, :] = ones_rows


def _attn_prep(h, w, tabs, b, seq, tm, scale):
    d = h.shape[1]
    tm = min(tm, seq)
    nblk = seq // tm
    nq = MLA_HEADS * QK_PAD
    nv = MLA_HEADS * V_ROWS
    row = lambda bi, i: (bi * nblk + i, 0)
    fix = lambda bi, i: (0, 0)
    tab = pl.BlockSpec((tm, LANES), lambda bi, i: (i, 0))
    def wspec(arr):
        return pl.BlockSpec(arr.shape, fix, pipeline_mode=pl.Buffered(1))
    return pl.pallas_call(
        functools.partial(_attn_prep_kernel, scale=scale),
        grid=(b, nblk),
        in_specs=[pl.BlockSpec((tm, d), row), wspec(w["w_lat"]),
                  pl.BlockSpec((1, Q_LORA), fix), pl.BlockSpec((1, KV_LORA), fix),
                  wspec(w["wq"]), wspec(w["wk"]), wspec(w["wvt"]), tab, tab, tab],
        out_specs=[pl.BlockSpec((tm, nq), row), pl.BlockSpec((tm, nq), row),
                   pl.BlockSpec((None, nv, tm), lambda bi, i: (bi, 0, i))],
        out_shape=[jax.ShapeDtypeStruct((b * seq, nq), BF16),
                   jax.ShapeDtypeStruct((b * seq, nq), BF16),
                   jax.ShapeDtypeStruct((b, nv, seq), BF16)],
        compiler_params=_params(("parallel", "arbitrary")),
        name="attn_prep",
    )(h, w["w_lat"], w["q_norm"].reshape(1, -1), w["kv_norm"].reshape(1, -1),
      w["wq"], w["wk"], w["wvt"], *tabs)


def _mla_kernel(q_ref, k_ref, vt_ref, o_ref, s_ref, p_ref, qt_ref, *, tq, tk):
    n_q = q_ref.shape[0] // tq
    n_k = k_ref.shape[0] // tk

    def q_tile(qi, carry):
        q0 = pl.multiple_of(qi * tq, tq)
        qt_ref[...] = q_ref[pl.ds(q0, tq), :].T

        def scores(c, slot):
            k = k_ref[pl.ds(pl.multiple_of(c * tk, tk), tk), :]
            s = jnp.dot(k, qt_ref[...], preferred_element_type=F32)
            s_ref[slot] = s
            return jnp.max(s, axis=0, keepdims=True)

        def softmax(slot, cmax, m):
            m_new = jnp.maximum(m, cmax)
            p_ref[slot] = jnp.exp2((s_ref[slot] - m_new).astype(BF16))
            return m_new, jnp.exp2(m - m_new)

        def pv(c, slot):
            vt = vt_ref[:, pl.ds(pl.multiple_of(c * tk, tk), tk)]
            return jnp.dot(vt, p_ref[slot], preferred_element_type=F32)

        def stage(c, u, m, acc, cmax):
            part = pv(c - 1, (u - 1) % N_SLOTS)
            cmax_next = scores(c + 1, (u + 1) % N_SLOTS)
            m, alpha = softmax(u % N_SLOTS, cmax, m)
            return m, (acc + part) * alpha, cmax_next

        cmax = scores(0, 0)
        m, _ = softmax(0, cmax, jnp.full((1, tq), -1e30, F32))
        cmax = scores(1, 1)
        acc = jnp.zeros((V_ROWS, tq), F32)

        n_mid = n_k - 2
        unroll = 4 * N_SLOTS
        peel = n_mid % unroll
        st = (m, acc, cmax)
        for u in range(1, peel + 1):
            st = stage(u, u, *st)

        def group(j, st):
            for u in range(1, unroll + 1):
                st = stage(peel + unroll * j + u, peel + u, *st)
            return st

        trips = n_mid // unroll
        if trips == 1:
            st = group(0, st)
        elif trips > 1:
            st = lax.fori_loop(0, trips, group, st)
        m, acc, cmax = st
        part = pv(n_k - 2, (n_k - 2) % N_SLOTS)
        m, alpha = softmax((n_k - 1) % N_SLOTS, cmax, m)
        acc = (acc + part) * alpha + pv(n_k - 1, (n_k - 1) % N_SLOTS)
        out = acc[:V_HEAD] * (1.0 / acc[V_HEAD:V_HEAD + 1])
        o_ref[pl.ds(q0, tq), :] = out.T.astype(o_ref.dtype)
        return carry

    lax.fori_loop(0, n_q, q_tile, 0)


def _mla(q, k, vt, tq_outer, tq, tk):
    b, s, _ = q.shape
    tq_outer = min(tq_outer, s)
    tq = min(tq, tq_outer)
    tk = min(tk, s // 2)
    assert (s // tk) % 2 == 0
    return pl.pallas_call(
        functools.partial(_mla_kernel, tq=tq, tk=tk),
        grid=(b, MLA_HEADS, s // tq_outer),
        in_specs=[pl.BlockSpec((None, tq_outer, QK_PAD), lambda bi, h, qi: (bi, qi, h)),
                  pl.BlockSpec((None, s, QK_PAD), lambda bi, h, qi: (bi, 0, h)),
                  pl.BlockSpec((None, V_ROWS, s), lambda bi, h, qi: (bi, h, 0))],
        out_specs=pl.BlockSpec((None, tq_outer, V_HEAD), lambda bi, h, qi: (bi, qi, h)),
        out_shape=jax.ShapeDtypeStruct((b, s, MLA_HEADS * V_HEAD), BF16),
        scratch_shapes=[pltpu.VMEM((N_SLOTS, tk, tq), F32),
                        pltpu.VMEM((N_SLOTS, tk, tq), BF16),
                        pltpu.VMEM((QK_PAD, tq), BF16)],
        compiler_params=_params(("parallel", "parallel", "arbitrary")),
        name="mla_attention",
    )(q, k, vt)


def _mem_kernel(q_ref, kv_ref, o_ref, *, scale):
    for h in range(MEM_HEADS):
        lo = h * MEM_HEAD_DIM
        q = q_ref[:, lo:lo + MEM_HEAD_DIM]
        k = kv_ref[:, lo:lo + MEM_HEAD_DIM]
        v = kv_ref[:, MEM_DIM + lo:MEM_DIM + lo + MEM_HEAD_DIM]
        s = lax.dot_general(q, k, (((1,), (1,)), ((), ())),
                            preferred_element_type=F32) * scale
        p = jnp.exp2(s - jnp.max(s, axis=-1, keepdims=True))
        l = jnp.sum(p, axis=-1, keepdims=True)
        o = jnp.dot(p.astype(BF16), v, preferred_element_type=F32)
        o_ref[:, lo:lo + MEM_HEAD_DIM] = (o * (1.0 / l)).astype(o_ref.dtype)


def _mem_attn(z1, kvm, q_blk, tq):
    b, s, _ = z1.shape
    n_mem = kvm.shape[1]
    tq = min(tq, s)
    scale = MEM_HEAD_DIM ** -0.5 * LOG2E
    return pl.pallas_call(
        functools.partial(_mem_kernel, scale=scale),
        grid=(b, s // tq),
        in_specs=[pl.BlockSpec((None, tq, MEM_DIM), lambda bi, i: (bi, i, q_blk)),
                  pl.BlockSpec((None, n_mem, 2 * MEM_DIM), lambda bi, i: (bi, 0, 0))],
        out_specs=pl.BlockSpec((None, tq, MEM_DIM), lambda bi, i: (bi, i, 0)),
        out_shape=jax.ShapeDtypeStruct((b, s, MEM_DIM), BF16),
        compiler_params=_params(("parallel", "arbitrary")),
        name="mem_attention",
    )(z1, kvm)


def _fft_a_kernel(u_ref, f_ref, tc_ref, ts_ref, o_ref, *, n1):
    f = f_ref[...]
    for j in range(u_ref.shape[0]):
        r = jnp.dot(f, u_ref[j], preferred_element_type=F32)
        br, bi = r[:n1], r[n1:]
        tc, ts = tc_ref[j], ts_ref[j]
        o_ref[j, 0] = (br * tc + bi * ts).astype(o_ref.dtype)
        o_ref[j, 1] = (bi * tc - br * ts).astype(o_ref.dtype)


def _fft_b_kernel(b_ref, m_ref, cs_ref, o_ref):
    m2 = m_ref[...]
    cs = cs_ref[...]
    n2 = o_ref.shape[1]
    for j in range(b_ref.shape[0]):
        x = b_ref[j].reshape(2 * n2, FOURIER_DIM)
        z = jnp.dot(m2, x, preferred_element_type=F32).astype(BF16)
        for g in range(N_GROUPS):
            lo = g * GROUP_DIM
            zz = jnp.concatenate([z[:n2, lo:lo + GROUP_DIM], z[n2:, lo:lo + GROUP_DIM]],
                                 axis=1)
            y = jnp.dot(zz, cs, preferred_element_type=F32)
            o_ref[j, :, lo:lo + GROUP_DIM] = y.astype(o_ref.dtype)


def _dft_tables(seq):
    n2 = FFT_N2
    n1 = seq // n2
    a1 = 2.0 * np.pi * np.outer(np.arange(n1), np.arange(n1)) / n1
    f1 = np.concatenate([np.cos(a1), -np.sin(a1)], axis=0) / math.sqrt(n1)
    th = 2.0 * np.pi * np.outer(np.arange(n2), np.arange(n1)) / seq
    a2 = 2.0 * np.pi * np.outer(np.arange(n2), np.arange(n2)) / n2
    c2, s2 = np.cos(a2), np.sin(a2)
    m2 = np.block([[c2, s2], [-s2, c2]]) / math.sqrt(n2)
    ac = 2.0 * np.pi * np.outer(np.arange(GROUP_DIM), np.arange(GROUP_DIM)) / GROUP_DIM
    cs = np.concatenate([np.cos(ac), np.sin(ac)], axis=0) / math.sqrt(GROUP_DIM)
    return (jnp.asarray(f1, BF16), jnp.asarray(np.cos(th)[:, :, None], F32),
            jnp.asarray(np.sin(th)[:, :, None], F32), jnp.asarray(m2, BF16),
            jnp.asarray(cs, BF16))


def _fourier(u, t2, tk1):
    b, s, c = u.shape
    n2 = FFT_N2
    n1 = s // n2
    t2 = min(t2, n2)
    tk1 = min(tk1, n1)
    f1, tc, ts, m2, cs = _dft_tables(s)
    ut = u.reshape(b, n1, n2, c).transpose(0, 2, 1, 3)
    bt = pl.pallas_call(
        functools.partial(_fft_a_kernel, n1=n1),
        grid=(b, n2 // t2),
        in_specs=[pl.BlockSpec((None, t2, n1, c), lambda bi, i: (bi, i, 0, 0)),
                  pl.BlockSpec((2 * n1, n1), lambda bi, i: (0, 0)),
                  pl.BlockSpec((t2, n1, 1), lambda bi, i: (i, 0, 0)),
                  pl.BlockSpec((t2, n1, 1), lambda bi, i: (i, 0, 0))],
        out_specs=pl.BlockSpec((None, t2, 2, n1, c), lambda bi, i: (bi, i, 0, 0, 0)),
        out_shape=jax.ShapeDtypeStruct((b, n2, 2, n1, c), BF16),
        compiler_params=_params(("parallel", "arbitrary")),
        name="fft_stage_a",
    )(ut, f1, tc, ts)
    bk = bt.transpose(0, 3, 2, 1, 4)
    y = pl.pallas_call(
        _fft_b_kernel,
        grid=(b, n1 // tk1),
        in_specs=[pl.BlockSpec((None, tk1, 2, n2, c), lambda bi, i: (bi, i, 0, 0, 0)),
                  pl.BlockSpec((2 * n2, 2 * n2), lambda bi, i: (0, 0)),
                  pl.BlockSpec((2 * GROUP_DIM, GROUP_DIM), lambda bi, i: (0, 0))],
        out_specs=pl.BlockSpec((None, tk1, n2, c), lambda bi, i: (bi, i, 0, 0)),
        out_shape=jax.ShapeDtypeStruct((b, n1, n2, c), BF16),
        compiler_params=_params(("parallel", "arbitrary")),
        name="fft_stage_b",
    )(bk, m2, cs)
    return y.transpose(0, 2, 1, 3).reshape(b, s, c)


def _merge_kernel(yf_ref, oa_ref, om_ref, g0_ref, g1_ref, g2_ref,
                  wf_ref, wa_ref, wm_ref, o_ref):
    def gate(g_ref):
        return jax.nn.sigmoid(g_ref[...].astype(F32))
    yf = jnp.dot(yf_ref[...], wf_ref[...], preferred_element_type=F32)
    ya = jnp.dot(oa_ref[...], wa_ref[...], preferred_element_type=F32)
    ym = jnp.dot(om_ref[...], wm_ref[...], preferred_element_type=F32)
    o_ref[...] = (gate(g0_ref) * yf + gate(g1_ref) * ya + gate(g2_ref) * ym
                  ).astype(o_ref.dtype)


def _merge(yf, oa, om, z1, wf, wa, wm, tm, tn):
    t = yf.shape[0]
    tm = min(tm, t)
    nb = D_MODEL // tn
    def gspec(k):
        return pl.BlockSpec((tm, tn), lambda i, j: (i, k * nb + j))
    def aspec(w):
        return pl.BlockSpec((tm, w), lambda i, j: (i, 0))
    def wspec(w):
        return pl.BlockSpec((w, tn), lambda i, j: (0, j))
    return pl.pallas_call(
        _merge_kernel,
        grid=(t // tm, nb),
        in_specs=[aspec(FOURIER_DIM), aspec(MLA_HEADS * V_HEAD), aspec(MEM_DIM),
                  gspec(0), gspec(1), gspec(2),
                  wspec(FOURIER_DIM), wspec(MLA_HEADS * V_HEAD), wspec(MEM_DIM)],
        out_specs=pl.BlockSpec((tm, tn), lambda i, j: (i, j)),
        out_shape=jax.ShapeDtypeStruct((t, D_MODEL), BF16),
        compiler_params=_params(("parallel", "arbitrary")),
        name="branch_merge",
    )(yf, oa, om, z1, z1, z1, wf, wa, wm)


def _proj_norm_kernel(a_ref, w_ref, x_ref, gpost_ref, gpre_ref, x1_ref, *h_ref, ts):
    for r0 in range(0, a_ref.shape[0], ts):
        rows = slice(r0, r0 + ts)
        y = jnp.dot(a_ref[rows, :], w_ref[...], preferred_element_type=F32)
        x1 = x_ref[rows, :] + _rms(y, gpost_ref[...])
        x1_ref[rows, :] = x1
        if h_ref:
            h_ref[0][rows, :] = _rms(x1, gpre_ref[...]).astype(h_ref[0].dtype)


def _proj_norm(a, w, x, gpost, gpre, tm, ts, emit_h, name):
    t, d = x.shape
    k = a.shape[1]
    tm = min(tm, t)
    row = lambda i: (i, 0)
    fix = lambda i: (0, 0)
    out_specs = [pl.BlockSpec((tm, d), row)]
    out_shape = [jax.ShapeDtypeStruct((t, d), F32)]
    if emit_h:
        out_specs.append(pl.BlockSpec((tm, d), row))
        out_shape.append(jax.ShapeDtypeStruct((t, d), BF16))
    outs = pl.pallas_call(
        functools.partial(_proj_norm_kernel, ts=min(ts, tm)),
        grid=(t // tm,),
        in_specs=[pl.BlockSpec((tm, k), row),
                  pl.BlockSpec((k, d), fix, pipeline_mode=pl.Buffered(1)),
                  pl.BlockSpec((tm, d), row), pl.BlockSpec((1, d), fix),
                  pl.BlockSpec((1, d), fix)],
        out_specs=out_specs,
        out_shape=out_shape,
        compiler_params=_params(("parallel",)),
        name=name,
    )(a, w, x, gpost.reshape(1, d), gpre.reshape(1, d))
    return outs if emit_h else (outs[0], None)


def _gelu_tanh(x):
    return 0.5 * x * (1.0 + jnp.tanh(0.7978845608028654 * (x + 0.044715 * x * x * x)))


def _ffn_act_kernel(a_ref, ap_ref, an_ref, wg_ref, wu_ref, cw_ref, cb_ref, o_ref, ax_ref, *,
                    tiles_per_seq, halo):
    tm = a_ref.shape[0]
    tc = o_ref.shape[1]

    pos = pl.program_id(0) % tiles_per_seq

    @pl.when(pl.program_id(1) == 0)
    def _():
        zero = jnp.zeros_like(ap_ref[...])
        ax_ref[0:halo] = jnp.where(pos == 0, zero, ap_ref[...])
        ax_ref[halo:halo + tm] = a_ref[...]
        ax_ref[halo + tm:] = jnp.where(pos == tiles_per_seq - 1, zero, an_ref[...])

    g = jnp.dot(ax_ref[...], wg_ref[...], preferred_element_type=F32)
    u = jnp.dot(ax_ref[halo:halo + tm], wu_ref[...], preferred_element_type=F32)
    g_prev = pltpu.roll(g, 1, 0)[halo:halo + tm]
    g_next = pltpu.roll(g, tm + 2 * halo - 1, 0)[halo:halo + tm]
    cw = cw_ref[...]
    conv = g_prev * cw[0:1] + g[halo:halo + tm] * cw[1:2] + g_next * cw[2:3] + cb_ref[...]
    o_ref[...] = (_gelu_tanh(conv) * u).astype(o_ref.dtype)


def _ffn_act(h, w_gate, w_up, cw, cb, seq, tm, tc):
    t, d = h.shape
    n = cw.shape[1]
    tm = min(tm, seq)
    halo = BF16_ROWS
    tiles_per_seq = seq // tm
    hb = tm // halo
    last_hb = t // halo - 1
    return pl.pallas_call(
        functools.partial(_ffn_act_kernel, tiles_per_seq=tiles_per_seq, halo=halo),
        grid=(t // tm, n // tc),
        in_specs=[pl.BlockSpec((tm, d), lambda i, j: (i, 0)),
                  pl.BlockSpec((halo, d), lambda i, j: (jnp.maximum(i * hb - 1, 0), 0)),
                  pl.BlockSpec((halo, d), lambda i, j: (jnp.minimum((i + 1) * hb, last_hb), 0)),
                  pl.BlockSpec((d, tc), lambda i, j: (0, j)),
                  pl.BlockSpec((d, tc), lambda i, j: (0, j)),
                  pl.BlockSpec((3, tc), lambda i, j: (0, j)),
                  pl.BlockSpec((1, tc), lambda i, j: (0, j))],
        out_specs=pl.BlockSpec((tm, tc), lambda i, j: (i, j)),
        out_shape=jax.ShapeDtypeStruct((t, n), BF16),
        scratch_shapes=[pltpu.VMEM((tm + 2 * halo, d), BF16)],
        compiler_params=_params(("parallel", "arbitrary")),
        name="ffn_gate_up_act",
    )(h, h, h, w_gate, w_up, cw, cb.reshape(1, n))


def _pack_weights(p, l):
    s0 = FOURIER_DIM
    s1 = s0 + Q_LORA
    s2 = s1 + KV_LORA
    s3 = s2 + QK_ROPE
    s4 = s3 + MEM_DIM
    w_in = p["w_in"][l]
    d = w_in.shape[0]
    bf = lambda a: a.astype(BF16)
    w_gq = bf(jnp.concatenate([w_in[:, s4:], w_in[:, s3:s4]], axis=1))
    w_f = bf(w_in[:, :s0])
    w_lat = bf(jnp.concatenate([w_in[:, s0:s3], jnp.zeros((d, KPE_PAD - QK_ROPE), F32)],
                               axis=1))
    wq = p["w_uq"][l].reshape(Q_LORA, MLA_HEADS, QK_NOPE + QK_ROPE)
    wq = jnp.pad(wq, ((0, 0), (0, 0), (0, QK_PAD - QK_NOPE - QK_ROPE)))
    wq = bf(wq.reshape(Q_LORA, MLA_HEADS * QK_PAD))
    wkv = p["w_ukv"][l].reshape(KV_LORA, MLA_HEADS, QK_NOPE + V_HEAD)
    wk = bf(wkv[:, :, :QK_NOPE].reshape(KV_LORA, MLA_HEADS * QK_NOPE))
    wvt = bf(wkv[:, :, QK_NOPE:].reshape(KV_LORA, MLA_HEADS * V_HEAD).T)
    return dict(
        w_gq=w_gq, w_f=w_f, w_lat=w_lat, wq=wq, wk=wk, wvt=wvt,
        w_gate=bf(p["w_ffn_gate"][l]), w_up=bf(p["w_ffn_up"][l]),
        w_mem_kv=bf(p["w_mem_kv"][l]), w_fo=bf(p["w_fourier_out"][l]),
        w_ao=bf(p["w_attn_out"][l]), w_mo=bf(p["w_mem_out"][l]), w_o=bf(p["w_o"][l]),
        w_down=bf(p["w_ffn_down"][l]), conv_w=p["ffn_conv_w"][l], conv_b=p["ffn_conv_b"][l],
        q_norm=p["q_norm"][l], kv_norm=p["kv_norm"][l], mem_norm=p["mem_norm"][l],
        pre_mix=p["pre_mix_norm"][l], post_mix=p["post_mix_norm"][l],
        pre_ffn=p["pre_ffn_norm"][l], post_ffn=p["post_ffn_norm"][l])


def _rope_tables(seq):
    inv_freq = 1.0 / (ROPE_THETA ** (jnp.arange(0, QK_ROPE, 2, dtype=F32) / QK_ROPE))
    ang = jnp.arange(seq, dtype=F32)[:, None] * inv_freq[None, :]
    cos, sin = jnp.cos(ang), jnp.sin(ang)
    half = QK_ROPE // 2
    zh = jnp.zeros((seq, half), F32)
    zr = jnp.zeros((seq, LANES - QK_ROPE), F32)
    c = jnp.concatenate([cos, cos, zr], axis=1)
    s1 = jnp.concatenate([-sin, zh, zr], axis=1)
    s2 = jnp.concatenate([zh, sin, zr], axis=1)
    return c, s1, s2


def _trunk(x, mem, layers):
    b, s, d = x.shape
    t = b * s
    n_mem = mem.shape[1]
    tabs = _rope_tables(s)
    q_scale = (QK_NOPE + QK_ROPE) ** -0.5 * LOG2E
    gate_w = 3 * D_MODEL
    xf = x.reshape(t, d)
    memf = mem.reshape(b * n_mem, d)
    h = None
    for l, w in enumerate(layers):
        if l == 0:
            h, z1 = _norm_mm(xf, w["pre_mix"], w["w_gq"], MM_ROWS, MM_COLS, "in_proj_gates")
        else:
            z1 = _mm(h, w["w_gq"], MM_ROWS, MM_COLS, "in_proj_gates")
        f_in = _mm(h, w["w_f"], MM_ROWS, MM_COLS, "in_proj_fourier")
        q, kc, vt = _attn_prep(h, w, tabs, b, s, PREP_ROWS, q_scale)
        q_tile = ATT_Q_BLOCK if s <= ATT_Q_BLOCK else ATT_Q_TILE_LONG
        oa = _mla(q.reshape(b, s, -1), kc.reshape(b, s, -1), vt, ATT_Q_BLOCK, q_tile,
                  ATT_K_CHUNK)
        mem_n = _norm(memf, w["mem_norm"], n_mem)
        kvm = _mm(mem_n, w["w_mem_kv"], MM_ROWS, MM_COLS, "mem_kv")
        om = _mem_attn(z1.reshape(b, s, -1), kvm.reshape(b, n_mem, -1),
                       gate_w // MEM_DIM, MEM_Q_TILE)
        yf = _fourier(f_in.reshape(b, s, -1), FFT_STEP, FFT_STEP)
        merged = _merge(yf.reshape(t, -1), oa.reshape(t, -1), om.reshape(t, -1), z1,
                        w["w_fo"], w["w_ao"], w["w_mo"], MERGE_ROWS, MERGE_COLS)
        x1, h2 = _proj_norm(merged, w["w_o"], xf, w["post_mix"], w["pre_ffn"],
                            OUT_PROJ_ROWS, NORM_SUB_ROWS, True, "out_proj_norm")
        act = _ffn_act(h2, w["w_gate"], w["w_up"], w["conv_w"], w["conv_b"], s, FFN_ROWS,
                       FF_CHUNK)
        last = l == len(layers) - 1
        nxt = layers[0 if last else l + 1]["pre_mix"]
        xf, h = _proj_norm(act, w["w_down"], x1, w["post_ffn"], nxt, DOWN_ROWS,
                           NORM_SUB_ROWS, not last, "ffn_down_norm")
    return xf.reshape(b, s, d)


def kernel(x_prompt, x_sample, mem_prompt, mem_sample, pre_mix_norm, w_in, q_norm, w_uq,
           kv_norm, w_ukv, mem_norm, w_mem_kv, w_fourier_out, w_attn_out, w_mem_out, w_o,
           post_mix_norm, pre_ffn_norm, w_ffn_gate, w_ffn_up, ffn_conv_w, ffn_conv_b,
           w_ffn_down, post_ffn_norm):
    p = dict(pre_mix_norm=pre_mix_norm, w_in=w_in, q_norm=q_norm, w_uq=w_uq,
             kv_norm=kv_norm, w_ukv=w_ukv, mem_norm=mem_norm, w_mem_kv=w_mem_kv,
             w_fourier_out=w_fourier_out, w_attn_out=w_attn_out, w_mem_out=w_mem_out,
             w_o=w_o, post_mix_norm=post_mix_norm, pre_ffn_norm=pre_ffn_norm,
             w_ffn_gate=w_ffn_gate, w_ffn_up=w_ffn_up, ffn_conv_w=ffn_conv_w,
             ffn_conv_b=ffn_conv_b, w_ffn_down=w_ffn_down, post_ffn_norm=post_ffn_norm)
    layers = [_pack_weights(p, l) for l in range(w_in.shape[0])]
    return (_trunk(x_prompt, mem_prompt, layers), _trunk(x_sample, mem_sample, layers))
```

```python
import functools
import math

import numpy as np
import jax
import jax.numpy as jnp
from jax import lax
from jax.experimental import pallas as pl
from jax.experimental.pallas import tpu as pltpu

D_MODEL = 2048
N_GROUPS = 4
GROUP_DIM = 256
FOURIER_DIM = N_GROUPS * GROUP_DIM
MLA_HEADS = 16
Q_LORA = 512
KV_LORA = 512
QK_NOPE = 128
QK_ROPE = 64
V_HEAD = 128
ROPE_THETA = 10000.0
MEM_HEADS = 4
MEM_HEAD_DIM = 256
MEM_DIM = MEM_HEADS * MEM_HEAD_DIM
EPS = 1e-6

LANES = 128
BF16_ROWS = 16
QK_PAD = 256
KPE_PAD = LANES
V_ROWS = V_HEAD + BF16_ROWS
N_SLOTS = 3
FFT_N2 = 128
VMEM_LIMIT = 56 * 1024 * 1024
LOG2E = 1.4426950408889634

MM_ROWS, MM_COLS = 1024, 1024
PREP_ROWS = 256
ATT_Q_BLOCK = 2048
ATT_Q_TILE_LONG = 1024
ATT_K_CHUNK = 256
MEM_Q_TILE = 1024
FFT_STEP = 16
MERGE_ROWS, MERGE_COLS = 1024, 512
OUT_PROJ_ROWS, DOWN_ROWS = 512, 256
NORM_SUB_ROWS = 128
FFN_ROWS, FF_CHUNK = 1024, 512

BF16 = jnp.bfloat16
F32 = jnp.float32


def _params(sem):
    return pltpu.CompilerParams(dimension_semantics=sem, vmem_limit_bytes=VMEM_LIMIT)


def _rms(x, g):
    ms = jnp.mean(x * x, axis=-1, keepdims=True)
    return x * lax.rsqrt(ms + EPS) * g


def _rope128(x, c, s1, s2):
    return x * c + pltpu.roll(x, 96, 1) * s1 + pltpu.roll(x, 32, 1) * s2


def _norm_kernel(x_ref, g_ref, o_ref):
    o_ref[...] = _rms(x_ref[...].astype(F32), g_ref[...]).astype(o_ref.dtype)


def _norm(x, g, tm):
    t, d = x.shape
    return pl.pallas_call(
        _norm_kernel,
        grid=(t // tm,),
        in_specs=[pl.BlockSpec((tm, d), lambda i: (i, 0)),
                  pl.BlockSpec((1, d), lambda i: (0, 0))],
        out_specs=pl.BlockSpec((tm, d), lambda i: (i, 0)),
        out_shape=jax.ShapeDtypeStruct((t, d), BF16),
        compiler_params=_params(("parallel",)),
        name="rmsnorm",
    )(x, g.reshape(1, d))


def _mm_kernel(a_ref, w_ref, o_ref):
    o_ref[...] = jnp.dot(a_ref[...], w_ref[...],
                         preferred_element_type=F32).astype(o_ref.dtype)


def _mm(a, w, tm, tn, name):
    m, k = a.shape
    n = w.shape[1]
    tm = min(tm, m)
    tn = min(tn, n)
    return pl.pallas_call(
        _mm_kernel,
        grid=(m // tm, n // tn),
        in_specs=[pl.BlockSpec((tm, k), lambda i, j: (i, 0)),
                  pl.BlockSpec((k, tn), lambda i, j: (0, j))],
        out_specs=pl.BlockSpec((tm, tn), lambda i, j: (i, j)),
        out_shape=jax.ShapeDtypeStruct((m, n), BF16),
        compiler_params=_params(("parallel", "arbitrary")),
        name=name,
    )(a, w)


def _norm_mm_kernel(x_ref, g_ref, w_ref, h_ref, o_ref):
    @pl.when(pl.program_id(1) == 0)
    def _():
        h_ref[...] = _rms(x_ref[...], g_ref[...]).astype(h_ref.dtype)

    o_ref[...] = jnp.dot(h_ref[...], w_ref[...],
                         preferred_element_type=F32).astype(o_ref.dtype)


def _norm_mm(x, g, w, tm, tn, name):
    m, k = x.shape
    n = w.shape[1]
    tm = min(tm, m)
    return pl.pallas_call(
        _norm_mm_kernel,
        grid=(m // tm, n // tn),
        in_specs=[pl.BlockSpec((tm, k), lambda i, j: (i, 0)),
                  pl.BlockSpec((1, k), lambda i, j: (0, 0)),
                  pl.BlockSpec((k, tn), lambda i, j: (0, j))],
        out_specs=[pl.BlockSpec((tm, k), lambda i, j: (i, 0)),
                   pl.BlockSpec((tm, tn), lambda i, j: (i, j))],
        out_shape=[jax.ShapeDtypeStruct((m, k), BF16), jax.ShapeDtypeStruct((m, n), BF16)],
        compiler_params=_params(("parallel", "arbitrary")),
        name=name,
    )(x, g.reshape(1, k), w)


def _attn_prep_kernel(h_ref, wl_ref, qg_ref, kvg_ref, wq_ref, wk_ref, wvt_ref,
                      c_ref, s1_ref, s2_ref, q_ref, k_ref, vt_ref, *, scale):
    c, s1, s2 = c_ref[...], s1_ref[...], s2_ref[...]
    z = jnp.dot(h_ref[...], wl_ref[...], preferred_element_type=F32)
    cq = _rms(z[:, :Q_LORA], qg_ref[...]).astype(BF16)
    lat = _rms(z[:, Q_LORA:Q_LORA + KV_LORA], kvg_ref[...]).astype(BF16)
    kpe = _rope128(z[:, Q_LORA + KV_LORA:], c, s1, s2).astype(BF16)
    zq = jnp.dot(cq, wq_ref[...], preferred_element_type=F32) * scale
    kn = jnp.dot(lat, wk_ref[...], preferred_element_type=F32).astype(BF16)
    vt = lax.dot_general(wvt_ref[...], lat, (((1,), (1,)), ((), ())),
                         preferred_element_type=F32).astype(BF16)
    pad_rows = V_ROWS - V_HEAD
    row = lax.broadcasted_iota(jnp.int32, (pad_rows, vt.shape[1]), 0)
    ones_rows = jnp.where(row == 0, 1.0, 0.0).astype(BF16)
    for h in range(MLA_HEADS):
        lo = h * QK_PAD
        q_ref[:, lo:lo + QK_NOPE] = zq[:, lo:lo + QK_NOPE].astype(BF16)
        pe = _rope128(zq[:, lo + QK_NOPE:lo + QK_PAD], c, s1, s2)
        q_ref[:, lo + QK_NOPE:lo + QK_PAD] = pe.astype(BF16)
        k_ref[:, lo:lo + QK_NOPE] = kn[:, h * QK_NOPE:(h + 1) * QK_NOPE]
        k_ref[:, lo + QK_NOPE:lo + QK_PAD] = kpe
        vt_ref[h * V_ROWS:h * V_ROWS + V_HEAD, :] = vt[h * V_HEAD:(h + 1) * V_HEAD, :]
        vt_ref[h * V_ROWS + V_HEAD:(h + 1) * V_ROWS, :] = ones_rows


def _attn_prep(h, w, tabs, b, seq, tm, scale):
    d = h.shape[1]
    tm = min(tm, seq)
    nblk = seq // tm
    nq = MLA_HEADS * QK_PAD
    nv = MLA_HEADS * V_ROWS
    row = lambda bi, i: (bi * nblk + i, 0)
    fix = lambda bi, i: (0, 0)
    tab = pl.BlockSpec((tm, LANES), lambda bi, i: (i, 0))
    def wspec(arr):
        return pl.BlockSpec(arr.shape, fix, pipeline_mode=pl.Buffered(1))
    return pl.pallas_call(
        functools.partial(_attn_prep_kernel, scale=scale),
        grid=(b, nblk),
        in_specs=[pl.BlockSpec((tm, d), row), wspec(w["w_lat"]),
                  pl.BlockSpec((1, Q_LORA), fix), pl.BlockSpec((1, KV_LORA), fix),
                  wspec(w["wq"]), wspec(w["wk"]), wspec(w["wvt"]), tab, tab, tab],
        out_specs=[pl.BlockSpec((tm, nq), row), pl.BlockSpec((tm, nq), row),
                   pl.BlockSpec((None, nv, tm), lambda bi, i: (bi, 0, i))],
        out_shape=[jax.ShapeDtypeStruct((b * seq, nq), BF16),
                   jax.ShapeDtypeStruct((b * seq, nq), BF16),
                   jax.ShapeDtypeStruct((b, nv, seq), BF16)],
        compiler_params=_params(("parallel", "arbitrary")),
        name="attn_prep",
    )(h, w["w_lat"], w["q_norm"].reshape(1, -1), w["kv_norm"].reshape(1, -1),
      w["wq"], w["wk"], w["wvt"], *tabs)


def _mla_kernel(q_ref, k_ref, vt_ref, o_ref, s_ref, p_ref, qt_ref, *, tq, tk):
    n_q = q_ref.shape[0] // tq
    n_k = k_ref.shape[0] // tk

    def q_tile(qi, carry):
        q0 = pl.multiple_of(qi * tq, tq)
        qt_ref[...] = q_ref[pl.ds(q0, tq), :].T

        def scores(c, slot):
            k = k_ref[pl.ds(pl.multiple_of(c * tk, tk), tk), :]
            s = jnp.dot(k, qt_ref[...], preferred_element_type=F32)
            s_ref[slot] = s
            return jnp.max(s, axis=0, keepdims=True)

        def softmax(slot, cmax, m):
            m_new = jnp.maximum(m, cmax)
            p_ref[slot] = jnp.exp2(s_ref[slot] - m_new).astype(BF16)
            return m_new, jnp.exp2(m - m_new)

        def pv(c, slot):
            vt = vt_ref[:, pl.ds(pl.multiple_of(c * tk, tk), tk)]
            return jnp.dot(vt, p_ref[slot], preferred_element_type=F32)

        def stage(c, u, m, acc, cmax):
            part = pv(c - 1, (u - 1) % N_SLOTS)
            cmax_next = scores(c + 1, (u + 1) % N_SLOTS)
            m, alpha = softmax(u % N_SLOTS, cmax, m)
            return m, (acc + part) * alpha, cmax_next

        cmax = scores(0, 0)
        m, _ = softmax(0, cmax, jnp.full((1, tq), -1e30, F32))
        cmax = scores(1, 1)
        acc = jnp.zeros((V_ROWS, tq), F32)

        n_mid = n_k - 2
        unroll = 4 * N_SLOTS
        peel = n_mid % unroll
        st = (m, acc, cmax)
        for u in range(1, peel + 1):
            st = stage(u, u, *st)

        def group(j, st):
            for u in range(1, unroll + 1):
                st = stage(peel + unroll * j + u, peel + u, *st)
            return st

        trips = n_mid // unroll
        if trips == 1:
            st = group(0, st)
        elif trips > 1:
            st = lax.fori_loop(0, trips, group, st)
        m, acc, cmax = st
        part = pv(n_k - 2, (n_k - 2) % N_SLOTS)
        m, alpha = softmax((n_k - 1) % N_SLOTS, cmax, m)
        acc = (acc + part) * alpha + pv(n_k - 1, (n_k - 1) % N_SLOTS)
        out = acc[:V_HEAD] * (1.0 / acc[V_HEAD:V_HEAD + 1])
        o_ref[pl.ds(q0, tq), :] = out.T.astype(o_ref.dtype)
        return carry

    lax.fori_loop(0, n_q, q_tile, 0)


def _mla(q, k, vt, tq_outer, tq, tk):
    b, s, _ = q.shape
    tq_outer = min(tq_outer, s)
    tq = min(tq, tq_outer)
    tk = min(tk, s // 2)
    assert (s // tk) % 2 == 0
    return pl.pallas_call(
        functools.partial(_mla_kernel, tq=tq, tk=tk),
        grid=(b, MLA_HEADS, s // tq_outer),
        in_specs=[pl.BlockSpec((None, tq_outer, QK_PAD), lambda bi, h, qi: (bi, qi, h)),
                  pl.BlockSpec((None, s, QK_PAD), lambda bi, h, qi: (bi, 0, h)),
                  pl.BlockSpec((None, V_ROWS, s), lambda bi, h, qi: (bi, h, 0))],
        out_specs=pl.BlockSpec((None, tq_outer, V_HEAD), lambda bi, h, qi: (bi, qi, h)),
        out_shape=jax.ShapeDtypeStruct((b, s, MLA_HEADS * V_HEAD), BF16),
        scratch_shapes=[pltpu.VMEM((N_SLOTS, tk, tq), F32),
                        pltpu.VMEM((N_SLOTS, tk, tq), BF16),
                        pltpu.VMEM((QK_PAD, tq), BF16)],
        compiler_params=_params(("parallel", "parallel", "arbitrary")),
        name="mla_attention",
    )(q, k, vt)


def _mem_kernel(q_ref, kv_ref, o_ref, *, scale):
    for h in range(MEM_HEADS):
        lo = h * MEM_HEAD_DIM
        q = q_ref[:, lo:lo + MEM_HEAD_DIM]
        k = kv_ref[:, lo:lo + MEM_HEAD_DIM]
        v = kv_ref[:, MEM_DIM + lo:MEM_DIM + lo + MEM_HEAD_DIM]
        s = lax.dot_general(q, k, (((1,), (1,)), ((), ())),
                            preferred_element_type=F32) * scale
        p = jnp.exp2(s - jnp.max(s, axis=-1, keepdims=True))
        l = jnp.sum(p, axis=-1, keepdims=True)
        o = jnp.dot(p.astype(BF16), v, preferred_element_type=F32)
        o_ref[:, lo:lo + MEM_HEAD_DIM] = (o * (1.0 / l)).astype(o_ref.dtype)


def _mem_attn(z1, kvm, q_blk, tq):
    b, s, _ = z1.shape
    n_mem = kvm.shape[1]
    tq = min(tq, s)
    scale = MEM_HEAD_DIM ** -0.5 * LOG2E
    return pl.pallas_call(
        functools.partial(_mem_kernel, scale=scale),
        grid=(b, s // tq),
        in_specs=[pl.BlockSpec((None, tq, MEM_DIM), lambda bi, i: (bi, i, q_blk)),
                  pl.BlockSpec((None, n_mem, 2 * MEM_DIM), lambda bi, i: (bi, 0, 0))],
        out_specs=pl.BlockSpec((None, tq, MEM_DIM), lambda bi, i: (bi, i, 0)),
        out_shape=jax.ShapeDtypeStruct((b, s, MEM_DIM), BF16),
        compiler_params=_params(("parallel", "arbitrary")),
        name="mem_attention",
    )(z1, kvm)


def _fft_a_kernel(u_ref, f_ref, tc_ref, ts_ref, o_ref, *, n1):
    f = f_ref[...]
    for j in range(u_ref.shape[0]):
        r = jnp.dot(f, u_ref[j], preferred_element_type=F32)
        br, bi = r[:n1], r[n1:]
        tc, ts = tc_ref[j], ts_ref[j]
        o_ref[j, 0] = (br * tc + bi * ts).astype(o_ref.dtype)
        o_ref[j, 1] = (bi * tc - br * ts).astype(o_ref.dtype)


def _fft_b_kernel(b_ref, m_ref, cs_ref, o_ref):
    m2 = m_ref[...]
    cs = cs_ref[...]
    n2 = o_ref.shape[1]
    for j in range(b_ref.shape[0]):
        x = b_ref[j].reshape(2 * n2, FOURIER_DIM)
        z = jnp.dot(m2, x, preferred_element_type=F32).astype(BF16)
        for g in range(N_GROUPS):
            lo = g * GROUP_DIM
            zz = jnp.concatenate([z[:n2, lo:lo + GROUP_DIM], z[n2:, lo:lo + GROUP_DIM]],
                                 axis=1)
            y = jnp.dot(zz, cs, preferred_element_type=F32)
            o_ref[j, :, lo:lo + GROUP_DIM] = y.astype(o_ref.dtype)


def _dft_tables(seq):
    n2 = FFT_N2
    n1 = seq // n2
    a1 = 2.0 * np.pi * np.outer(np.arange(n1), np.arange(n1)) / n1
    f1 = np.concatenate([np.cos(a1), -np.sin(a1)], axis=0) / math.sqrt(n1)
    th = 2.0 * np.pi * np.outer(np.arange(n2), np.arange(n1)) / seq
    a2 = 2.0 * np.pi * np.outer(np.arange(n2), np.arange(n2)) / n2
    c2, s2 = np.cos(a2), np.sin(a2)
    m2 = np.block([[c2, s2], [-s2, c2]]) / math.sqrt(n2)
    ac = 2.0 * np.pi * np.outer(np.arange(GROUP_DIM), np.arange(GROUP_DIM)) / GROUP_DIM
    cs = np.concatenate([np.cos(ac), np.sin(ac)], axis=0) / math.sqrt(GROUP_DIM)
    return (jnp.asarray(f1, BF16), jnp.asarray(np.cos(th)[:, :, None], F32),
            jnp.asarray(np.sin(th)[:, :, None], F32), jnp.asarray(m2, BF16),
            jnp.asarray(cs, BF16))


def _fourier(u, t2, tk1):
    b, s, c = u.shape
    n2 = FFT_N2
    n1 = s // n2
    t2 = min(t2, n2)
    tk1 = min(tk1, n1)
    f1, tc, ts, m2, cs = _dft_tables(s)
    ut = u.reshape(b, n1, n2, c).transpose(0, 2, 1, 3)
    bt = pl.pallas_call(
        functools.partial(_fft_a_kernel, n1=n1),
        grid=(b, n2 // t2),
        in_specs=[pl.BlockSpec((None, t2, n1, c), lambda bi, i: (bi, i, 0, 0)),
                  pl.BlockSpec((2 * n1, n1), lambda bi, i: (0, 0)),
                  pl.BlockSpec((t2, n1, 1), lambda bi, i: (i, 0, 0)),
                  pl.BlockSpec((t2, n1, 1), lambda bi, i: (i, 0, 0))],
        out_specs=pl.BlockSpec((None, t2, 2, n1, c), lambda bi, i: (bi, i, 0, 0, 0)),
        out_shape=jax.ShapeDtypeStruct((b, n2, 2, n1, c), BF16),
        compiler_params=_params(("parallel", "arbitrary")),
        name="fft_stage_a",
    )(ut, f1, tc, ts)
    bk = bt.transpose(0, 3, 2, 1, 4)
    y = pl.pallas_call(
        _fft_b_kernel,
        grid=(b, n1 // tk1),
        in_specs=[pl.BlockSpec((None, tk1, 2, n2, c), lambda bi, i: (bi, i, 0, 0, 0)),
                  pl.BlockSpec((2 * n2, 2 * n2), lambda bi, i: (0, 0)),
                  pl.BlockSpec((2 * GROUP_DIM, GROUP_DIM), lambda bi, i: (0, 0))],
        out_specs=pl.BlockSpec((None, tk1, n2, c), lambda bi, i: (bi, i, 0, 0)),
        out_shape=jax.ShapeDtypeStruct((b, n1, n2, c), BF16),
        compiler_params=_params(("parallel", "arbitrary")),
        name="fft_stage_b",
    )(bk, m2, cs)
    return y.transpose(0, 2, 1, 3).reshape(b, s, c)


def _merge_kernel(yf_ref, oa_ref, om_ref, g0_ref, g1_ref, g2_ref,
                  wf_ref, wa_ref, wm_ref, o_ref):
    def gate(g_ref):
        return jax.nn.sigmoid(g_ref[...].astype(F32))
    yf = jnp.dot(yf_ref[...], wf_ref[...], preferred_element_type=F32)
    ya = jnp.dot(oa_ref[...], wa_ref[...], preferred_element_type=F32)
    ym = jnp.dot(om_ref[...], wm_ref[...], preferred_element_type=F32)
    o_ref[...] = (gate(g0_ref) * yf + gate(g1_ref) * ya + gate(g2_ref) * ym
                  ).astype(o_ref.dtype)


def _merge(yf, oa, om, z1, wf, wa, wm, tm, tn):
    t = yf.shape[0]
    tm = min(tm, t)
    nb = D_MODEL // tn
    def gspec(k):
        return pl.BlockSpec((tm, tn), lambda i, j: (i, k * nb + j))
    def aspec(w):
        return pl.BlockSpec((tm, w), lambda i, j: (i, 0))
    def wspec(w):
        return pl.BlockSpec((w, tn), lambda i, j: (0, j))
    return pl.pallas_call(
        _merge_kernel,
        grid=(t // tm, nb),
        in_specs=[aspec(FOURIER_DIM), aspec(MLA_HEADS * V_HEAD), aspec(MEM_DIM),
                  gspec(0), gspec(1), gspec(2),
                  wspec(FOURIER_DIM), wspec(MLA_HEADS * V_HEAD), wspec(MEM_DIM)],
        out_specs=pl.BlockSpec((tm, tn), lambda i, j: (i, j)),
        out_shape=jax.ShapeDtypeStruct((t, D_MODEL), BF16),
        compiler_params=_params(("parallel", "arbitrary")),
        name="branch_merge",
    )(yf, oa, om, z1, z1, z1, wf, wa, wm)


def _proj_norm_kernel(a_ref, w_ref, x_ref, gpost_ref, gpre_ref, x1_ref, *h_ref, ts):
    for r0 in range(0, a_ref.shape[0], ts):
        rows = slice(r0, r0 + ts)
        y = jnp.dot(a_ref[rows, :], w_ref[...], preferred_element_type=F32)
        x1 = x_ref[rows, :] + _rms(y, gpost_ref[...])
        x1_ref[rows, :] = x1
        if h_ref:
            h_ref[0][rows, :] = _rms(x1, gpre_ref[...]).astype(h_ref[0].dtype)


def _proj_norm(a, w, x, gpost, gpre, tm, ts, emit_h, name):
    t, d = x.shape
    k = a.shape[1]
    tm = min(tm, t)
    row = lambda i: (i, 0)
    fix = lambda i: (0, 0)
    out_specs = [pl.BlockSpec((tm, d), row)]
    out_shape = [jax.ShapeDtypeStruct((t, d), F32)]
    if emit_h:
        out_specs.append(pl.BlockSpec((tm, d), row))
        out_shape.append(jax.ShapeDtypeStruct((t, d), BF16))
    outs = pl.pallas_call(
        functools.partial(_proj_norm_kernel, ts=min(ts, tm)),
        grid=(t // tm,),
        in_specs=[pl.BlockSpec((tm, k), row),
                  pl.BlockSpec((k, d), fix, pipeline_mode=pl.Buffered(1)),
                  pl.BlockSpec((tm, d), row), pl.BlockSpec((1, d), fix),
                  pl.BlockSpec((1, d), fix)],
        out_specs=out_specs,
        out_shape=out_shape,
        compiler_params=_params(("parallel",)),
        name=name,
    )(a, w, x, gpost.reshape(1, d), gpre.reshape(1, d))
    return outs if emit_h else (outs[0], None)


def _gelu_tanh(x):
    return 0.5 * x * (1.0 + jnp.tanh(0.7978845608028654 * (x + 0.044715 * x * x * x)))


def _ffn_act_kernel(a_ref, ap_ref, an_ref, wg_ref, wu_ref, cw_ref, cb_ref, o_ref, ax_ref, *,
                    tiles_per_seq, halo):
    tm = a_ref.shape[0]
    tc = o_ref.shape[1]

    pos = pl.program_id(0) % tiles_per_seq

    @pl.when(pl.program_id(1) == 0)
    def _():
        zero = jnp.zeros_like(ap_ref[...])
        ax_ref[0:halo] = jnp.where(pos == 0, zero, ap_ref[...])
        ax_ref[halo:halo + tm] = a_ref[...]
        ax_ref[halo + tm:] = jnp.where(pos == tiles_per_seq - 1, zero, an_ref[...])

    g = jnp.dot(ax_ref[...], wg_ref[...], preferred_element_type=F32)
    u = jnp.dot(ax_ref[halo:halo + tm], wu_ref[...], preferred_element_type=F32)
    g_prev = pltpu.roll(g, 1, 0)[halo:halo + tm]
    g_next = pltpu.roll(g, tm + 2 * halo - 1, 0)[halo:halo + tm]
    cw = cw_ref[...]
    conv = g_prev * cw[0:1] + g[halo:halo + tm] * cw[1:2] + g_next * cw[2:3] + cb_ref[...]
    o_ref[...] = (_gelu_tanh(conv) * u).astype(o_ref.dtype)


def _ffn_act(h, w_gate, w_up, cw, cb, seq, tm, tc):
    t, d = h.shape
    n = cw.shape[1]
    tm = min(tm, seq)
    halo = BF16_ROWS
    tiles_per_seq = seq // tm
    hb = tm // halo
    last_hb = t // halo - 1
    return pl.pallas_call(
        functools.partial(_ffn_act_kernel, tiles_per_seq=tiles_per_seq, halo=halo),
        grid=(t // tm, n // tc),
        in_specs=[pl.BlockSpec((tm, d), lambda i, j: (i, 0)),
                  pl.BlockSpec((halo, d), lambda i, j: (jnp.maximum(i * hb - 1, 0), 0)),
                  pl.BlockSpec((halo, d), lambda i, j: (jnp.minimum((i + 1) * hb, last_hb), 0)),
                  pl.BlockSpec((d, tc), lambda i, j: (0, j)),
                  pl.BlockSpec((d, tc), lambda i, j: (0, j)),
                  pl.BlockSpec((3, tc), lambda i, j: (0, j)),
                  pl.BlockSpec((1, tc), lambda i, j: (0, j))],
        out_specs=pl.BlockSpec((tm, tc), lambda i, j: (i, j)),
        out_shape=jax.ShapeDtypeStruct((t, n), BF16),
        scratch_shapes=[pltpu.VMEM((tm + 2 * halo, d), BF16)],
        compiler_params=_params(("parallel", "arbitrary")),
        name="ffn_gate_up_act",
    )(h, h, h, w_gate, w_up, cw, cb.reshape(1, n))


def _pack_weights(p, l):
    s0 = FOURIER_DIM
    s1 = s0 + Q_LORA
    s2 = s1 + KV_LORA
    s3 = s2 + QK_ROPE
    s4 = s3 + MEM_DIM
    w_in = p["w_in"][l]
    d = w_in.shape[0]
    bf = lambda a: a.astype(BF16)
    w_gq = bf(jnp.concatenate([w_in[:, s4:], w_in[:, s3:s4]], axis=1))
    w_f = bf(w_in[:, :s0])
    w_lat = bf(jnp.concatenate([w_in[:, s0:s3], jnp.zeros((d, KPE_PAD - QK_ROPE), F32)],
                               axis=1))
    wq = p["w_uq"][l].reshape(Q_LORA, MLA_HEADS, QK_NOPE + QK_ROPE)
    wq = jnp.pad(wq, ((0, 0), (0, 0), (0, QK_PAD - QK_NOPE - QK_ROPE)))
    wq = bf(wq.reshape(Q_LORA, MLA_HEADS * QK_PAD))
    wkv = p["w_ukv"][l].reshape(KV_LORA, MLA_HEADS, QK_NOPE + V_HEAD)
    wk = bf(wkv[:, :, :QK_NOPE].reshape(KV_LORA, MLA_HEADS * QK_NOPE))
    wvt = bf(wkv[:, :, QK_NOPE:].reshape(KV_LORA, MLA_HEADS * V_HEAD).T)
    return dict(
        w_gq=w_gq, w_f=w_f, w_lat=w_lat, wq=wq, wk=wk, wvt=wvt,
        w_gate=bf(p["w_ffn_gate"][l]), w_up=bf(p["w_ffn_up"][l]),
        w_mem_kv=bf(p["w_mem_kv"][l]), w_fo=bf(p["w_fourier_out"][l]),
        w_ao=bf(p["w_attn_out"][l]), w_mo=bf(p["w_mem_out"][l]), w_o=bf(p["w_o"][l]),
        w_down=bf(p["w_ffn_down"][l]), conv_w=p["ffn_conv_w"][l], conv_b=p["ffn_conv_b"][l],
        q_norm=p["q_norm"][l], kv_norm=p["kv_norm"][l], mem_norm=p["mem_norm"][l],
        pre_mix=p["pre_mix_norm"][l], post_mix=p["post_mix_norm"][l],
        pre_ffn=p["pre_ffn_norm"][l], post_ffn=p["post_ffn_norm"][l])


def _rope_tables(seq):
    inv_freq = 1.0 / (ROPE_THETA ** (jnp.arange(0, QK_ROPE, 2, dtype=F32) / QK_ROPE))
    ang = jnp.arange(seq, dtype=F32)[:, None] * inv_freq[None, :]
    cos, sin = jnp.cos(ang), jnp.sin(ang)
    half = QK_ROPE // 2
    zh = jnp.zeros((seq, half), F32)
    zr = jnp.zeros((seq, LANES - QK_ROPE), F32)
    c = jnp.concatenate([cos, cos, zr], axis=1)
    s1 = jnp.concatenate([-sin, zh, zr], axis=1)
    s2 = jnp.concatenate([zh, sin, zr], axis=1)
    return c, s1, s2


def _trunk(x, mem, layers):
    b, s, d = x.shape
    t = b * s
    n_mem = mem.shape[1]
    tabs = _rope_tables(s)
    q_scale = (QK_NOPE + QK_ROPE) ** -0.5 * LOG2E
    gate_w = 3 * D_MODEL
    xf = x.reshape(t, d)
    memf = mem.reshape(b * n_mem, d)
    h = None
    for l, w in enumerate(layers):
        if l == 0:
            h, z1 = _norm_mm(xf, w["pre_mix"], w["w_gq"], MM_ROWS, MM_COLS, "in_proj_gates")
        else:
            z1 = _mm(h, w["w_gq"], MM_ROWS, MM_COLS, "in_proj_gates")
        f_in = _mm(h, w["w_f"], MM_ROWS, MM_COLS, "in_proj_fourier")
        q, kc, vt = _attn_prep(h, w, tabs, b, s, PREP_ROWS, q_scale)
        q_tile = ATT_Q_BLOCK if s <= ATT_Q_BLOCK else ATT_Q_TILE_LONG
        oa = _mla(q.reshape(b, s, -1), kc.reshape(b, s, -1), vt, ATT_Q_BLOCK, q_tile,
                  ATT_K_CHUNK)
        mem_n = _norm(memf, w["mem_norm"], n_mem)
        kvm = _mm(mem_n, w["w_mem_kv"], MM_ROWS, MM_COLS, "mem_kv")
        om = _mem_attn(z1.reshape(b, s, -1), kvm.reshape(b, n_mem, -1),
                       gate_w // MEM_DIM, MEM_Q_TILE)
        yf = _fourier(f_in.reshape(b, s, -1), FFT_STEP, FFT_STEP)
        merged = _merge(yf.reshape(t, -1), oa.reshape(t, -1), om.reshape(t, -1), z1,
                        w["w_fo"], w["w_ao"], w["w_mo"], MERGE_ROWS, MERGE_COLS)
        x1, h2 = _proj_norm(merged, w["w_o"], xf, w["post_mix"], w["pre_ffn"],
                            OUT_PROJ_ROWS, NORM_SUB_ROWS, True, "out_proj_norm")
        act = _ffn_act(h2, w["w_gate"], w["w_up"], w["conv_w"], w["conv_b"], s, FFN_ROWS,
                       FF_CHUNK)
        last = l == len(layers) - 1
        nxt = layers[0 if last else l + 1]["pre_mix"]
        xf, h = _proj_norm(act, w["w_down"], x1, w["post_ffn"], nxt, DOWN_ROWS,
                           NORM_SUB_ROWS, not last, "ffn_down_norm")
    return xf.reshape(b, s, d)


def kernel(x_prompt, x_sample, mem_prompt, mem_sample, pre_mix_norm, w_in, q_norm, w_uq,
           kv_norm, w_ukv, mem_norm, w_mem_kv, w_fourier_out, w_attn_out, w_mem_out, w_o,
           post_mix_norm, pre_ffn_norm, w_ffn_gate, w_ffn_up, ffn_conv_w, ffn_conv_b,
           w_ffn_down, post_ffn_norm):
    p = dict(pre_mix_norm=pre_mix_norm, w_in=w_in, q_norm=q_norm, w_uq=w_uq,
             kv_norm=kv_norm, w_ukv=w_ukv, mem_norm=mem_norm, w_mem_kv=w_mem_kv,
             w_fourier_out=w_fourier_out, w_attn_out=w_attn_out, w_mem_out=w_mem_out,
             w_o=w_o, post_mix_norm=post_mix_norm, pre_ffn_norm=pre_ffn_norm,
             w_ffn_gate=w_ffn_gate, w_ffn_up=w_ffn_up, ffn_conv_w=ffn_conv_w,
             ffn_conv_b=ffn_conv_b, w_ffn_down=w_ffn_down, post_ffn_norm=post_ffn_norm)
    layers = [_pack_weights(p, l) for l in range(w_in.shape[0])]
    return (_trunk(x_prompt, mem_prompt, layers), _trunk(x_sample, mem_sample, layers))
```

```python
import functools
import math

import numpy as np
import jax
import jax.numpy as jnp
from jax import lax
from jax.experimental import pallas as pl
from jax.experimental.pallas import tpu as pltpu

D_MODEL = 2048
N_GROUPS = 4
GROUP_DIM = 256
FOURIER_DIM = N_GROUPS * GROUP_DIM
MLA_HEADS = 16
Q_LORA = 512
KV_LORA = 512
QK_NOPE = 128
QK_ROPE = 64
V_HEAD = 128
ROPE_THETA = 10000.0
MEM_HEADS = 4
MEM_HEAD_DIM = 256
MEM_DIM = MEM_HEADS * MEM_HEAD_DIM
EPS = 1e-6

LANES = 128
BF16_ROWS = 16
QK_PAD = 256
KPE_PAD = LANES
V_ROWS = V_HEAD + BF16_ROWS
N_SLOTS = 3
FFT_N2 = 128
VMEM_LIMIT = 56 * 1024 * 1024
LOG2E = 1.4426950408889634

MM_ROWS, MM_COLS = 1024, 1024
PREP_ROWS = 256
ATT_Q_BLOCK = 2048
ATT_Q_TILE_LONG = 1024
ATT_K_CHUNK = 256
MEM_Q_TILE = 1024
FFT_STEP = 16
MERGE_ROWS, MERGE_COLS = 1024, 512
OUT_PROJ_ROWS, DOWN_ROWS = 512, 256
NORM_SUB_ROWS = 128
FFN_ROWS, FF_CHUNK = 1024, 512

BF16 = jnp.bfloat16
F32 = jnp.float32


def _params(sem):
    return pltpu.CompilerParams(dimension_semantics=sem, vmem_limit_bytes=VMEM_LIMIT)


def _rms(x, g):
    ms = jnp.mean(x * x, axis=-1, keepdims=True)
    return x * lax.rsqrt(ms + EPS) * g


def _rope128(x, c, s1, s2):
    return x * c + pltpu.roll(x, 96, 1) * s1 + pltpu.roll(x, 32, 1) * s2


def _norm_kernel(x_ref, g_ref, o_ref):
    o_ref[...] = _rms(x_ref[...].astype(F32), g_ref[...]).astype(o_ref.dtype)


def _norm(x, g, tm):
    t, d = x.shape
    return pl.pallas_call(
        _norm_kernel,
        grid=(t // tm,),
        in_specs=[pl.BlockSpec((tm, d), lambda i: (i, 0)),
                  pl.BlockSpec((1, d), lambda i: (0, 0))],
        out_specs=pl.BlockSpec((tm, d), lambda i: (i, 0)),
        out_shape=jax.ShapeDtypeStruct((t, d), BF16),
        compiler_params=_params(("parallel",)),
        name="rmsnorm",
    )(x, g.reshape(1, d))


def _mm_kernel(a_ref, w_ref, o_ref):
    o_ref[...] = jnp.dot(a_ref[...], w_ref[...],
                         preferred_element_type=F32).astype(o_ref.dtype)


def _mm(a, w, tm, tn, name):
    m, k = a.shape
    n = w.shape[1]
    tm = min(tm, m)
    tn = min(tn, n)
    return pl.pallas_call(
        _mm_kernel,
        grid=(m // tm, n // tn),
        in_specs=[pl.BlockSpec((tm, k), lambda i, j: (i, 0)),
                  pl.BlockSpec((k, tn), lambda i, j: (0, j))],
        out_specs=pl.BlockSpec((tm, tn), lambda i, j: (i, j)),
        out_shape=jax.ShapeDtypeStruct((m, n), BF16),
        compiler_params=_params(("parallel", "arbitrary")),
        name=name,
    )(a, w)


def _norm_mm_kernel(x_ref, g_ref, w_ref, h_ref, o_ref):
    @pl.when(pl.program_id(1) == 0)
    def _():
        h_ref[...] = _rms(x_ref[...], g_ref[...]).astype(h_ref.dtype)

    o_ref[...] = jnp.dot(h_ref[...], w_ref[...],
                         preferred_element_type=F32).astype(o_ref.dtype)


def _norm_mm(x, g, w, tm, tn, name):
    m, k = x.shape
    n = w.shape[1]
    tm = min(tm, m)
    return pl.pallas_call(
        _norm_mm_kernel,
        grid=(m // tm, n // tn),
        in_specs=[pl.BlockSpec((tm, k), lambda i, j: (i, 0)),
                  pl.BlockSpec((1, k), lambda i, j: (0, 0)),
                  pl.BlockSpec((k, tn), lambda i, j: (0, j))],
        out_specs=[pl.BlockSpec((tm, k), lambda i, j: (i, 0)),
                   pl.BlockSpec((tm, tn), lambda i, j: (i, j))],
        out_shape=[jax.ShapeDtypeStruct((m, k), BF16), jax.ShapeDtypeStruct((m, n), BF16)],
        compiler_params=_params(("parallel", "arbitrary")),
        name=name,
    )(x, g.reshape(1, k), w)


def _attn_prep_kernel(h_ref, wl_ref, qg_ref, kvg_ref, wq_ref, wk_ref, wvt_ref,
                      c_ref, s1_ref, s2_ref, q_ref, k_ref, vt_ref, *, scale):
    c, s1, s2 = c_ref[...], s1_ref[...], s2_ref[...]
    z = jnp.dot(h_ref[...], wl_ref[...], preferred_element_type=F32)
    cq = _rms(z[:, :Q_LORA], qg_ref[...]).astype(BF16)
    lat = _rms(z[:, Q_LORA:Q_LORA + KV_LORA], kvg_ref[...]).astype(BF16)
    kpe = _rope128(z[:, Q_LORA + KV_LORA:], c, s1, s2).astype(BF16)
    zq = jnp.dot(cq, wq_ref[...], preferred_element_type=F32) * scale
    kn = jnp.dot(lat, wk_ref[...], preferred_element_type=F32).astype(BF16)
    vt = lax.dot_general(wvt_ref[...], lat, (((1,), (1,)), ((), ())),
                         preferred_element_type=F32).astype(BF16)
    pad_rows = V_ROWS - V_HEAD
    row = lax.broadcasted_iota(jnp.int32, (pad_rows, vt.shape[1]), 0)
    ones_rows = jnp.where(row == 0, 1.0, 0.0).astype(BF16)
    for h in range(MLA_HEADS):
        lo = h * QK_PAD
        q_ref[:, lo:lo + QK_NOPE] = zq[:, lo:lo + QK_NOPE].astype(BF16)
        pe = _rope128(zq[:, lo + QK_NOPE:lo + QK_PAD], c, s1, s2)
        q_ref[:, lo + QK_NOPE:lo + QK_PAD] = pe.astype(BF16)
        k_ref[:, lo:lo + QK_NOPE] = kn[:, h * QK_NOPE:(h + 1) * QK_NOPE]
        k_ref[:, lo + QK_NOPE:lo + QK_PAD] = kpe
        vt_ref[h * V_ROWS:h * V_ROWS + V_HEAD, :] = vt[h * V_HEAD:(h + 1) * V_HEAD, :]
        vt_ref[h * V_ROWS + V_HEAD:(h + 1) * V_ROWS, :] = ones_rows


def _attn_prep(h, w, tabs, b, seq, tm, scale):
    d = h.shape[1]
    tm = min(tm, seq)
    nblk = seq // tm
    nq = MLA_HEADS * QK_PAD
    nv = MLA_HEADS * V_ROWS
    row = lambda bi, i: (bi * nblk + i, 0)
    fix = lambda bi, i: (0, 0)
    tab = pl.BlockSpec((tm, LANES), lambda bi, i: (i, 0))
    def wspec(arr):
        return pl.BlockSpec(arr.shape, fix, pipeline_mode=pl.Buffered(1))
    return pl.pallas_call(
        functools.partial(_attn_prep_kernel, scale=scale),
        grid=(b, nblk),
        in_specs=[pl.BlockSpec((tm, d), row), wspec(w["w_lat"]),
                  pl.BlockSpec((1, Q_LORA), fix), pl.BlockSpec((1, KV_LORA), fix),
                  wspec(w["wq"]), wspec(w["wk"]), wspec(w["wvt"]), tab, tab, tab],
        out_specs=[pl.BlockSpec((tm, nq), row), pl.BlockSpec((tm, nq), row),
                   pl.BlockSpec((None, nv, tm), lambda bi, i: (bi, 0, i))],
        out_shape=[jax.ShapeDtypeStruct((b * seq, nq), BF16),
                   jax.ShapeDtypeStruct((b * seq, nq), BF16),
                   jax.ShapeDtypeStruct((b, nv, seq), BF16)],
        compiler_params=_params(("parallel", "arbitrary")),
        name="attn_prep",
    )(h, w["w_lat"], w["q_norm"].reshape(1, -1), w["kv_norm"].reshape(1, -1),
      w["wq"], w["wk"], w["wvt"], *tabs)


def _mla_kernel(q_ref, k_ref, vt_ref, o_ref, s_ref, p_ref, qt_ref, *, tq, tk, exp_dtype):
    n_q = q_ref.shape[0] // tq
    n_k = k_ref.shape[0] // tk

    def q_tile(qi, carry):
        q0 = pl.multiple_of(qi * tq, tq)
        qt_ref[...] = q_ref[pl.ds(q0, tq), :].T

        def scores(c, slot):
            k = k_ref[pl.ds(pl.multiple_of(c * tk, tk), tk), :]
            s = jnp.dot(k, qt_ref[...], preferred_element_type=F32)
            s_ref[slot] = s
            return jnp.max(s, axis=0, keepdims=True)

        def softmax(slot, cmax, m):
            m_new = jnp.maximum(m, cmax)
            p_ref[slot] = jnp.exp2((s_ref[slot] - m_new).astype(exp_dtype)).astype(BF16)
            return m_new, jnp.exp2(m - m_new)

        def pv(c, slot):
            vt = vt_ref[:, pl.ds(pl.multiple_of(c * tk, tk), tk)]
            return jnp.dot(vt, p_ref[slot], preferred_element_type=F32)

        def stage(c, u, m, acc, cmax):
            part = pv(c - 1, (u - 1) % N_SLOTS)
            cmax_next = scores(c + 1, (u + 1) % N_SLOTS)
            m, alpha = softmax(u % N_SLOTS, cmax, m)
            return m, (acc + part) * alpha, cmax_next

        cmax = scores(0, 0)
        m, _ = softmax(0, cmax, jnp.full((1, tq), -1e30, F32))
        cmax = scores(1, 1)
        acc = jnp.zeros((V_ROWS, tq), F32)

        n_mid = n_k - 2
        unroll = 4 * N_SLOTS
        peel = n_mid % unroll
        st = (m, acc, cmax)
        for u in range(1, peel + 1):
            st = stage(u, u, *st)

        def group(j, st):
            for u in range(1, unroll + 1):
                st = stage(peel + unroll * j + u, peel + u, *st)
            return st

        trips = n_mid // unroll
        if trips == 1:
            st = group(0, st)
        elif trips > 1:
            st = lax.fori_loop(0, trips, group, st)
        m, acc, cmax = st
        part = pv(n_k - 2, (n_k - 2) % N_SLOTS)
        m, alpha = softmax((n_k - 1) % N_SLOTS, cmax, m)
        acc = (acc + part) * alpha + pv(n_k - 1, (n_k - 1) % N_SLOTS)
        out = acc[:V_HEAD] * (1.0 / acc[V_HEAD:V_HEAD + 1])
        o_ref[pl.ds(q0, tq), :] = out.T.astype(o_ref.dtype)
        return carry

    lax.fori_loop(0, n_q, q_tile, 0)


def _mla(q, k, vt, tq_outer, tq, tk):
    b, s, _ = q.shape
    tq_outer = min(tq_outer, s)
    tq = min(tq, tq_outer)
    tk = min(tk, s // 2)
    assert (s // tk) % 2 == 0
    return pl.pallas_call(
        functools.partial(_mla_kernel, tq=tq, tk=tk,
                          exp_dtype=F32 if tq == tq_outer == s else BF16),
        grid=(b, MLA_HEADS, s // tq_outer),
        in_specs=[pl.BlockSpec((None, tq_outer, QK_PAD), lambda bi, h, qi: (bi, qi, h)),
                  pl.BlockSpec((None, s, QK_PAD), lambda bi, h, qi: (bi, 0, h)),
                  pl.BlockSpec((None, V_ROWS, s), lambda bi, h, qi: (bi, h, 0))],
        out_specs=pl.BlockSpec((None, tq_outer, V_HEAD), lambda bi, h, qi: (bi, qi, h)),
        out_shape=jax.ShapeDtypeStruct((b, s, MLA_HEADS * V_HEAD), BF16),
        scratch_shapes=[pltpu.VMEM((N_SLOTS, tk, tq), F32),
                        pltpu.VMEM((N_SLOTS, tk, tq), BF16),
                        pltpu.VMEM((QK_PAD, tq), BF16)],
        compiler_params=_params(("parallel", "parallel", "arbitrary")),
        name="mla_attention",
    )(q, k, vt)


def _mem_kernel(q_ref, kv_ref, o_ref, *, scale):
    for h in range(MEM_HEADS):
        lo = h * MEM_HEAD_DIM
        q = q_ref[:, lo:lo + MEM_HEAD_DIM]
        k = kv_ref[:, lo:lo + MEM_HEAD_DIM]
        v = kv_ref[:, MEM_DIM + lo:MEM_DIM + lo + MEM_HEAD_DIM]
        s = lax.dot_general(q, k, (((1,), (1,)), ((), ())),
                            preferred_element_type=F32) * scale
        p = jnp.exp2(s - jnp.max(s, axis=-1, keepdims=True))
        l = jnp.sum(p, axis=-1, keepdims=True)
        o = jnp.dot(p.astype(BF16), v, preferred_element_type=F32)
        o_ref[:, lo:lo + MEM_HEAD_DIM] = (o * (1.0 / l)).astype(o_ref.dtype)


def _mem_attn(z1, kvm, q_blk, tq):
    b, s, _ = z1.shape
    n_mem = kvm.shape[1]
    tq = min(tq, s)
    scale = MEM_HEAD_DIM ** -0.5 * LOG2E
    return pl.pallas_call(
        functools.partial(_mem_kernel, scale=scale),
        grid=(b, s // tq),
        in_specs=[pl.BlockSpec((None, tq, MEM_DIM), lambda bi, i: (bi, i, q_blk)),
                  pl.BlockSpec((None, n_mem, 2 * MEM_DIM), lambda bi, i: (bi, 0, 0))],
        out_specs=pl.BlockSpec((None, tq, MEM_DIM), lambda bi, i: (bi, i, 0)),
        out_shape=jax.ShapeDtypeStruct((b, s, MEM_DIM), BF16),
        compiler_params=_params(("parallel", "arbitrary")),
        name="mem_attention",
    )(z1, kvm)


def _fft_a_kernel(u_ref, f_ref, tc_ref, ts_ref, o_ref, *, n1, grp):
    f = f_ref[...]
    c = u_ref.shape[-1]
    for j in range(0, u_ref.shape[0], grp):
        x = u_ref[j:j + grp].reshape(grp * n1, c)
        r = jnp.dot(f, x, preferred_element_type=F32).reshape(grp, 2, n1, c)
        br, bi = r[:, 0], r[:, 1]
        tc, ts = tc_ref[j:j + grp], ts_ref[j:j + grp]
        o_ref[j:j + grp, 0] = (br * tc + bi * ts).astype(o_ref.dtype)
        o_ref[j:j + grp, 1] = (bi * tc - br * ts).astype(o_ref.dtype)


def _fft_b_kernel(b_ref, m_ref, cs_ref, o_ref):
    m2 = m_ref[...]
    cs = cs_ref[...]
    n2 = o_ref.shape[1]
    for j in range(b_ref.shape[0]):
        x = b_ref[j].reshape(2 * n2, FOURIER_DIM)
        z = jnp.dot(m2, x, preferred_element_type=F32).astype(BF16)
        for g in range(N_GROUPS):
            lo = g * GROUP_DIM
            zz = jnp.concatenate([z[:n2, lo:lo + GROUP_DIM], z[n2:, lo:lo + GROUP_DIM]],
                                 axis=1)
            y = jnp.dot(zz, cs, preferred_element_type=F32)
            o_ref[j, :, lo:lo + GROUP_DIM] = y.astype(o_ref.dtype)


def _dft_tables(seq, grp):
    n2 = FFT_N2
    n1 = seq // n2
    a1 = 2.0 * np.pi * np.outer(np.arange(n1), np.arange(n1)) / n1
    f1 = np.concatenate([np.cos(a1), -np.sin(a1)], axis=0) / math.sqrt(n1)
    f1 = np.kron(np.eye(grp), f1)
    th = 2.0 * np.pi * np.outer(np.arange(n2), np.arange(n1)) / seq
    a2 = 2.0 * np.pi * np.outer(np.arange(n2), np.arange(n2)) / n2
    c2, s2 = np.cos(a2), np.sin(a2)
    m2 = np.block([[c2, s2], [-s2, c2]]) / math.sqrt(n2)
    ac = 2.0 * np.pi * np.outer(np.arange(GROUP_DIM), np.arange(GROUP_DIM)) / GROUP_DIM
    cs = np.concatenate([np.cos(ac), np.sin(ac)], axis=0) / math.sqrt(GROUP_DIM)
    return (jnp.asarray(f1, BF16), jnp.asarray(np.cos(th)[:, :, None], F32),
            jnp.asarray(np.sin(th)[:, :, None], F32), jnp.asarray(m2, BF16),
            jnp.asarray(cs, BF16))


def _fourier(u, t2, tk1):
    b, s, c = u.shape
    n2 = FFT_N2
    n1 = s // n2
    t2 = min(t2, n2)
    tk1 = min(tk1, n1)
    grp = min(t2, max(1, LANES // n1))
    t2 = min(t2 * grp, n2)
    f1, tc, ts, m2, cs = _dft_tables(s, grp)
    ut = u.reshape(b, n1, n2, c).transpose(0, 2, 1, 3)
    bt = pl.pallas_call(
        functools.partial(_fft_a_kernel, n1=n1, grp=grp),
        grid=(b, n2 // t2),
        in_specs=[pl.BlockSpec((None, t2, n1, c), lambda bi, i: (bi, i, 0, 0)),
                  pl.BlockSpec(f1.shape, lambda bi, i: (0, 0)),
                  pl.BlockSpec((t2, n1, 1), lambda bi, i: (i, 0, 0)),
                  pl.BlockSpec((t2, n1, 1), lambda bi, i: (i, 0, 0))],
        out_specs=pl.BlockSpec((None, t2, 2, n1, c), lambda bi, i: (bi, i, 0, 0, 0)),
        out_shape=jax.ShapeDtypeStruct((b, n2, 2, n1, c), BF16),
        compiler_params=_params(("parallel", "arbitrary")),
        name="fft_stage_a",
    )(ut, f1, tc, ts)
    bk = bt.transpose(0, 3, 2, 1, 4)
    y = pl.pallas_call(
        _fft_b_kernel,
        grid=(b, n1 // tk1),
        in_specs=[pl.BlockSpec((None, tk1, 2, n2, c), lambda bi, i: (bi, i, 0, 0, 0)),
                  pl.BlockSpec((2 * n2, 2 * n2), lambda bi, i: (0, 0)),
                  pl.BlockSpec((2 * GROUP_DIM, GROUP_DIM), lambda bi, i: (0, 0))],
        out_specs=pl.BlockSpec((None, tk1, n2, c), lambda bi, i: (bi, i, 0, 0)),
        out_shape=jax.ShapeDtypeStruct((b, n1, n2, c), BF16),
        compiler_params=_params(("parallel", "arbitrary")),
        name="fft_stage_b",
    )(bk, m2, cs)
    return y.transpose(0, 2, 1, 3).reshape(b, s, c)


def _merge_kernel(yf_ref, oa_ref, om_ref, g0_ref, g1_ref, g2_ref,
                  wf_ref, wa_ref, wm_ref, o_ref):
    def gate(g_ref):
        return jax.nn.sigmoid(g_ref[...].astype(F32))
    yf = jnp.dot(yf_ref[...], wf_ref[...], preferred_element_type=F32)
    ya = jnp.dot(oa_ref[...], wa_ref[...], preferred_element_type=F32)
    ym = jnp.dot(om_ref[...], wm_ref[...], preferred_element_type=F32)
    o_ref[...] = (gate(g0_ref) * yf + gate(g1_ref) * ya + gate(g2_ref) * ym
                  ).astype(o_ref.dtype)


def _merge(yf, oa, om, z1, wf, wa, wm, tm, tn):
    t = yf.shape[0]
    tm = min(tm, t)
    nb = D_MODEL // tn
    def gspec(k):
        return pl.BlockSpec((tm, tn), lambda i, j: (i, k * nb + j))
    def aspec(w):
        return pl.BlockSpec((tm, w), lambda i, j: (i, 0))
    def wspec(w):
        return pl.BlockSpec((w, tn), lambda i, j: (0, j))
    return pl.pallas_call(
        _merge_kernel,
        grid=(t // tm, nb),
        in_specs=[aspec(FOURIER_DIM), aspec(MLA_HEADS * V_HEAD), aspec(MEM_DIM),
                  gspec(0), gspec(1), gspec(2),
                  wspec(FOURIER_DIM), wspec(MLA_HEADS * V_HEAD), wspec(MEM_DIM)],
        out_specs=pl.BlockSpec((tm, tn), lambda i, j: (i, j)),
        out_shape=jax.ShapeDtypeStruct((t, D_MODEL), BF16),
        compiler_params=_params(("parallel", "arbitrary")),
        name="branch_merge",
    )(yf, oa, om, z1, z1, z1, wf, wa, wm)


def _proj_norm_kernel(a_ref, w_ref, x_ref, gpost_ref, gpre_ref, x1_ref, *h_ref, ts):
    for r0 in range(0, a_ref.shape[0], ts):
        rows = slice(r0, r0 + ts)
        y = jnp.dot(a_ref[rows, :], w_ref[...], preferred_element_type=F32)
        x1 = x_ref[rows, :] + _rms(y, gpost_ref[...])
        x1_ref[rows, :] = x1
        if h_ref:
            h_ref[0][rows, :] = _rms(x1, gpre_ref[...]).astype(h_ref[0].dtype)


def _proj_norm(a, w, x, gpost, gpre, tm, ts, emit_h, name):
    t, d = x.shape
    k = a.shape[1]
    tm = min(tm, t)
    row = lambda i: (i, 0)
    fix = lambda i: (0, 0)
    out_specs = [pl.BlockSpec((tm, d), row)]
    out_shape = [jax.ShapeDtypeStruct((t, d), F32)]
    if emit_h:
        out_specs.append(pl.BlockSpec((tm, d), row))
        out_shape.append(jax.ShapeDtypeStruct((t, d), BF16))
    outs = pl.pallas_call(
        functools.partial(_proj_norm_kernel, ts=min(ts, tm)),
        grid=(t // tm,),
        in_specs=[pl.BlockSpec((tm, k), row),
                  pl.BlockSpec((k, d), fix, pipeline_mode=pl.Buffered(1)),
                  pl.BlockSpec((tm, d), row), pl.BlockSpec((1, d), fix),
                  pl.BlockSpec((1, d), fix)],
        out_specs=out_specs,
        out_shape=out_shape,
        compiler_params=_params(("parallel",)),
        name=name,
    )(a, w, x, gpost.reshape(1, d), gpre.reshape(1, d))
    return outs if emit_h else (outs[0], None)


def _gelu_tanh(x):
    return 0.5 * x * (1.0 + jnp.tanh(0.7978845608028654 * (x + 0.044715 * x * x * x)))


def _ffn_act_kernel(a_ref, ap_ref, an_ref, wg_ref, wu_ref, cw_ref, cb_ref, o_ref, ax_ref, *,
                    tiles_per_seq, halo):
    tm = a_ref.shape[0]
    tc = o_ref.shape[1]

    pos = pl.program_id(0) % tiles_per_seq

    @pl.when(pl.program_id(1) == 0)
    def _():
        zero = jnp.zeros_like(ap_ref[...])
        ax_ref[0:halo] = jnp.where(pos == 0, zero, ap_ref[...])
        ax_ref[halo:halo + tm] = a_ref[...]
        ax_ref[halo + tm:] = jnp.where(pos == tiles_per_seq - 1, zero, an_ref[...])

    g = jnp.dot(ax_ref[...], wg_ref[...], preferred_element_type=F32)
    u = jnp.dot(ax_ref[halo:halo + tm], wu_ref[...], preferred_element_type=F32)
    g_prev = pltpu.roll(g, 1, 0)[halo:halo + tm]
    g_next = pltpu.roll(g, tm + 2 * halo - 1, 0)[halo:halo + tm]
    cw = cw_ref[...]
    conv = g_prev * cw[0:1] + g[halo:halo + tm] * cw[1:2] + g_next * cw[2:3] + cb_ref[...]
    o_ref[...] = (_gelu_tanh(conv) * u).astype(o_ref.dtype)


def _ffn_act(h, w_gate, w_up, cw, cb, seq, tm, tc):
    t, d = h.shape
    n = cw.shape[1]
    tm = min(tm, seq)
    halo = BF16_ROWS
    tiles_per_seq = seq // tm
    hb = tm // halo
    last_hb = t // halo - 1
    return pl.pallas_call(
        functools.partial(_ffn_act_kernel, tiles_per_seq=tiles_per_seq, halo=halo),
        grid=(t // tm, n // tc),
        in_specs=[pl.BlockSpec((tm, d), lambda i, j: (i, 0)),
                  pl.BlockSpec((halo, d), lambda i, j: (jnp.maximum(i * hb - 1, 0), 0)),
                  pl.BlockSpec((halo, d), lambda i, j: (jnp.minimum((i + 1) * hb, last_hb), 0)),
                  pl.BlockSpec((d, tc), lambda i, j: (0, j)),
                  pl.BlockSpec((d, tc), lambda i, j: (0, j)),
                  pl.BlockSpec((3, tc), lambda i, j: (0, j)),
                  pl.BlockSpec((1, tc), lambda i, j: (0, j))],
        out_specs=pl.BlockSpec((tm, tc), lambda i, j: (i, j)),
        out_shape=jax.ShapeDtypeStruct((t, n), BF16),
        scratch_shapes=[pltpu.VMEM((tm + 2 * halo, d), BF16)],
        compiler_params=_params(("parallel", "arbitrary")),
        name="ffn_gate_up_act",
    )(h, h, h, w_gate, w_up, cw, cb.reshape(1, n))


def _pack_weights(p, l):
    s0 = FOURIER_DIM
    s1 = s0 + Q_LORA
    s2 = s1 + KV_LORA
    s3 = s2 + QK_ROPE
    s4 = s3 + MEM_DIM
    w_in = p["w_in"][l]
    d = w_in.shape[0]
    bf = lambda a: a.astype(BF16)
    w_gq = bf(jnp.concatenate([w_in[:, s4:], w_in[:, s3:s4]], axis=1))
    w_f = bf(w_in[:, :s0])
    w_lat = bf(jnp.concatenate([w_in[:, s0:s3], jnp.zeros((d, KPE_PAD - QK_ROPE), F32)],
                               axis=1))
    wq = p["w_uq"][l].reshape(Q_LORA, MLA_HEADS, QK_NOPE + QK_ROPE)
    wq = jnp.pad(wq, ((0, 0), (0, 0), (0, QK_PAD - QK_NOPE - QK_ROPE)))
    wq = bf(wq.reshape(Q_LORA, MLA_HEADS * QK_PAD))
    wkv = p["w_ukv"][l].reshape(KV_LORA, MLA_HEADS, QK_NOPE + V_HEAD)
    wk = bf(wkv[:, :, :QK_NOPE].reshape(KV_LORA, MLA_HEADS * QK_NOPE))
    wvt = bf(wkv[:, :, QK_NOPE:].reshape(KV_LORA, MLA_HEADS * V_HEAD).T)
    return dict(
        w_gq=w_gq, w_f=w_f, w_lat=w_lat, wq=wq, wk=wk, wvt=wvt,
        w_gate=bf(p["w_ffn_gate"][l]), w_up=bf(p["w_ffn_up"][l]),
        w_mem_kv=bf(p["w_mem_kv"][l]), w_fo=bf(p["w_fourier_out"][l]),
        w_ao=bf(p["w_attn_out"][l]), w_mo=bf(p["w_mem_out"][l]), w_o=bf(p["w_o"][l]),
        w_down=bf(p["w_ffn_down"][l]), conv_w=p["ffn_conv_w"][l], conv_b=p["ffn_conv_b"][l],
        q_norm=p["q_norm"][l], kv_norm=p["kv_norm"][l], mem_norm=p["mem_norm"][l],
        pre_mix=p["pre_mix_norm"][l], post_mix=p["post_mix_norm"][l],
        pre_ffn=p["pre_ffn_norm"][l], post_ffn=p["post_ffn_norm"][l])


def _rope_tables(seq):
    inv_freq = 1.0 / (ROPE_THETA ** (jnp.arange(0, QK_ROPE, 2, dtype=F32) / QK_ROPE))
    ang = jnp.arange(seq, dtype=F32)[:, None] * inv_freq[None, :]
    cos, sin = jnp.cos(ang), jnp.sin(ang)
    half = QK_ROPE // 2
    zh = jnp.zeros((seq, half), F32)
    zr = jnp.zeros((seq, LANES - QK_ROPE), F32)
    c = jnp.concatenate([cos, cos, zr], axis=1)
    s1 = jnp.concatenate([-sin, zh, zr], axis=1)
    s2 = jnp.concatenate([zh, sin, zr], axis=1)
    return c, s1, s2


def _trunk(x, mem, layers):
    b, s, d = x.shape
    t = b * s
    n_mem = mem.shape[1]
    tabs = _rope_tables(s)
    q_scale = (QK_NOPE + QK_ROPE) ** -0.5 * LOG2E
    gate_w = 3 * D_MODEL
    xf = x.reshape(t, d)
    memf = mem.reshape(b * n_mem, d)
    h = None
    for l, w in enumerate(layers):
        if l == 0:
            h, z1 = _norm_mm(xf, w["pre_mix"], w["w_gq"], MM_ROWS, MM_COLS, "in_proj_gates")
        else:
            z1 = _mm(h, w["w_gq"], MM_ROWS, MM_COLS, "in_proj_gates")
        f_in = _mm(h, w["w_f"], MM_ROWS, MM_COLS, "in_proj_fourier")
        q, kc, vt = _attn_prep(h, w, tabs, b, s, PREP_ROWS, q_scale)
        q_tile = ATT_Q_BLOCK if s <= ATT_Q_BLOCK else ATT_Q_TILE_LONG
        oa = _mla(q.reshape(b, s, -1), kc.reshape(b, s, -1), vt, ATT_Q_BLOCK, q_tile,
                  ATT_K_CHUNK)
        mem_n = _norm(memf, w["mem_norm"], n_mem)
        kvm = _mm(mem_n, w["w_mem_kv"], MM_ROWS, MM_COLS, "mem_kv")
        om = _mem_attn(z1.reshape(b, s, -1), kvm.reshape(b, n_mem, -1),
                       gate_w // MEM_DIM, MEM_Q_TILE)
        yf = _fourier(f_in.reshape(b, s, -1), FFT_STEP, FFT_STEP)
        merged = _merge(yf.reshape(t, -1), oa.reshape(t, -1), om.reshape(t, -1), z1,
                        w["w_fo"], w["w_ao"], w["w_mo"], MERGE_ROWS, MERGE_COLS)
        x1, h2 = _proj_norm(merged, w["w_o"], xf, w["post_mix"], w["pre_ffn"],
                            OUT_PROJ_ROWS, NORM_SUB_ROWS, True, "out_proj_norm")
        act = _ffn_act(h2, w["w_gate"], w["w_up"], w["conv_w"], w["conv_b"], s, FFN_ROWS,
                       FF_CHUNK)
        last = l == len(layers) - 1
        nxt = layers[0 if last else l + 1]["pre_mix"]
        xf, h = _proj_norm(act, w["w_down"], x1, w["post_ffn"], nxt, DOWN_ROWS,
                           NORM_SUB_ROWS, not last, "ffn_down_norm")
    return xf.reshape(b, s, d)


def kernel(x_prompt, x_sample, mem_prompt, mem_sample, pre_mix_norm, w_in, q_norm, w_uq,
           kv_norm, w_ukv, mem_norm, w_mem_kv, w_fourier_out, w_attn_out, w_mem_out, w_o,
           post_mix_norm, pre_ffn_norm, w_ffn_gate, w_ffn_up, ffn_conv_w, ffn_conv_b,
           w_ffn_down, post_ffn_norm):
    p = dict(pre_mix_norm=pre_mix_norm, w_in=w_in, q_norm=q_norm, w_uq=w_uq,
             kv_norm=kv_norm, w_ukv=w_ukv, mem_norm=mem_norm, w_mem_kv=w_mem_kv,
             w_fourier_out=w_fourier_out, w_attn_out=w_attn_out, w_mem_out=w_mem_out,
             w_o=w_o, post_mix_norm=post_mix_norm, pre_ffn_norm=pre_ffn_norm,
             w_ffn_gate=w_ffn_gate, w_ffn_up=w_ffn_up, ffn_conv_w=ffn_conv_w,
             ffn_conv_b=ffn_conv_b, w_ffn_down=w_ffn_down, post_ffn_norm=post_ffn_norm)
    layers = [_pack_weights(p, l) for l in range(w_in.shape[0])]
    return (_trunk(x_prompt, mem_prompt, layers), _trunk(x_sample, mem_sample, layers))
```

```python
import functools
import math

import numpy as np
import jax
import jax.numpy as jnp
from jax import lax
from jax.experimental import pallas as pl
from jax.experimental.pallas import tpu as pltpu

D_MODEL = 2048
N_GROUPS = 4
GROUP_DIM = 256
FOURIER_DIM = N_GROUPS * GROUP_DIM
MLA_HEADS = 16
Q_LORA = 512
KV_LORA = 512
QK_NOPE = 128
QK_ROPE = 64
V_HEAD = 128
ROPE_THETA = 10000.0
MEM_HEADS = 4
MEM_HEAD_DIM = 256
MEM_DIM = MEM_HEADS * MEM_HEAD_DIM
EPS = 1e-6

LANES = 128
BF16_ROWS = 16
QK_PAD = 256
KPE_PAD = LANES
V_ROWS = V_HEAD + BF16_ROWS
N_SLOTS = 3
FFT_N2 = 128
VMEM_LIMIT = 56 * 1024 * 1024
LOG2E = 1.4426950408889634

MM_ROWS, MM_COLS = 1024, 1024
PREP_ROWS = 256
ATT_Q_BLOCK = 2048
ATT_Q_TILE_LONG = 1024
ATT_K_CHUNK = 256
MEM_Q_TILE = 1024
FFT_STEP = 16
MERGE_ROWS, MERGE_COLS = 1024, 512
OUT_PROJ_ROWS, DOWN_ROWS = 512, 256
NORM_SUB_ROWS = 128
FFN_ROWS, FF_CHUNK = 1024, 512

BF16 = jnp.bfloat16
F32 = jnp.float32


def _params(sem):
    return pltpu.CompilerParams(dimension_semantics=sem, vmem_limit_bytes=VMEM_LIMIT)


def _rms(x, g):
    ms = jnp.mean(x * x, axis=-1, keepdims=True)
    return x * lax.rsqrt(ms + EPS) * g


def _rope128(x, c, s1, s2):
    return x * c + pltpu.roll(x, 96, 1) * s1 + pltpu.roll(x, 32, 1) * s2


def _norm_kernel(x_ref, g_ref, o_ref):
    o_ref[...] = _rms(x_ref[...].astype(F32), g_ref[...]).astype(o_ref.dtype)


def _norm(x, g, tm):
    t, d = x.shape
    return pl.pallas_call(
        _norm_kernel,
        grid=(t // tm,),
        in_specs=[pl.BlockSpec((tm, d), lambda i: (i, 0)),
                  pl.BlockSpec((1, d), lambda i: (0, 0))],
        out_specs=pl.BlockSpec((tm, d), lambda i: (i, 0)),
        out_shape=jax.ShapeDtypeStruct((t, d), BF16),
        compiler_params=_params(("parallel",)),
        name="rmsnorm",
    )(x, g.reshape(1, d))


def _mm_kernel(a_ref, w_ref, o_ref):
    o_ref[...] = jnp.dot(a_ref[...], w_ref[...],
                         preferred_element_type=F32).astype(o_ref.dtype)


def _mm(a, w, tm, tn, name):
    m, k = a.shape
    n = w.shape[1]
    tm = min(tm, m)
    tn = min(tn, n)
    return pl.pallas_call(
        _mm_kernel,
        grid=(m // tm, n // tn),
        in_specs=[pl.BlockSpec((tm, k), lambda i, j: (i, 0)),
                  pl.BlockSpec((k, tn), lambda i, j: (0, j))],
        out_specs=pl.BlockSpec((tm, tn), lambda i, j: (i, j)),
        out_shape=jax.ShapeDtypeStruct((m, n), BF16),
        compiler_params=_params(("parallel", "arbitrary")),
        name=name,
    )(a, w)


def _norm_mm_kernel(x_ref, g_ref, w_ref, h_ref, o_ref):
    @pl.when(pl.program_id(1) == 0)
    def _():
        h_ref[...] = _rms(x_ref[...], g_ref[...]).astype(h_ref.dtype)

    o_ref[...] = jnp.dot(h_ref[...], w_ref[...],
                         preferred_element_type=F32).astype(o_ref.dtype)


def _norm_mm(x, g, w, tm, tn, name):
    m, k = x.shape
    n = w.shape[1]
    tm = min(tm, m)
    return pl.pallas_call(
        _norm_mm_kernel,
        grid=(m // tm, n // tn),
        in_specs=[pl.BlockSpec((tm, k), lambda i, j: (i, 0)),
                  pl.BlockSpec((1, k), lambda i, j: (0, 0)),
                  pl.BlockSpec((k, tn), lambda i, j: (0, j))],
        out_specs=[pl.BlockSpec((tm, k), lambda i, j: (i, 0)),
                   pl.BlockSpec((tm, tn), lambda i, j: (i, j))],
        out_shape=[jax.ShapeDtypeStruct((m, k), BF16), jax.ShapeDtypeStruct((m, n), BF16)],
        compiler_params=_params(("parallel", "arbitrary")),
        name=name,
    )(x, g.reshape(1, k), w)


def _attn_prep_kernel(h_ref, wl_ref, qg_ref, kvg_ref, wq_ref, wk_ref, wvt_ref,
                      c_ref, s1_ref, s2_ref, q_ref, k_ref, vt_ref, *, scale):
    c, s1, s2 = c_ref[...], s1_ref[...], s2_ref[...]
    z = jnp.dot(h_ref[...], wl_ref[...], preferred_element_type=F32)
    cq = _rms(z[:, :Q_LORA], qg_ref[...]).astype(BF16)
    lat = _rms(z[:, Q_LORA:Q_LORA + KV_LORA], kvg_ref[...]).astype(BF16)
    kpe = _rope128(z[:, Q_LORA + KV_LORA:], c, s1, s2).astype(BF16)
    zq = jnp.dot(cq, wq_ref[...], preferred_element_type=F32) * scale
    kn = jnp.dot(lat, wk_ref[...], preferred_element_type=F32).astype(BF16)
    vt = lax.dot_general(wvt_ref[...], lat, (((1,), (1,)), ((), ())),
                         preferred_element_type=F32).astype(BF16)
    pad_rows = V_ROWS - V_HEAD
    row = lax.broadcasted_iota(jnp.int32, (pad_rows, vt.shape[1]), 0)
    ones_rows = jnp.where(row == 0, 1.0, 0.0).astype(BF16)
    for h in range(MLA_HEADS):
        lo = h * QK_PAD
        q_ref[:, lo:lo + QK_NOPE] = zq[:, lo:lo + QK_NOPE].astype(BF16)
        pe = _rope128(zq[:, lo + QK_NOPE:lo + QK_PAD], c, s1, s2)
        q_ref[:, lo + QK_NOPE:lo + QK_PAD] = pe.astype(BF16)
        k_ref[:, lo:lo + QK_NOPE] = kn[:, h * QK_NOPE:(h + 1) * QK_NOPE]
        k_ref[:, lo + QK_NOPE:lo + QK_PAD] = kpe
        vt_ref[h * V_ROWS:h * V_ROWS + V_HEAD, :] = vt[h * V_HEAD:(h + 1) * V_HEAD, :]
        vt_ref[h * V_ROWS + V_HEAD:(h + 1) * V_ROWS, :] = ones_rows


def _attn_prep(h, w, tabs, b, seq, tm, scale):
    d = h.shape[1]
    tm = min(tm, seq)
    nblk = seq // tm
    nq = MLA_HEADS * QK_PAD
    nv = MLA_HEADS * V_ROWS
    row = lambda bi, i: (bi * nblk + i, 0)
    fix = lambda bi, i: (0, 0)
    tab = pl.BlockSpec((tm, LANES), lambda bi, i: (i, 0))
    def wspec(arr):
        return pl.BlockSpec(arr.shape, fix, pipeline_mode=pl.Buffered(1))
    return pl.pallas_call(
        functools.partial(_attn_prep_kernel, scale=scale),
        grid=(b, nblk),
        in_specs=[pl.BlockSpec((tm, d), row), wspec(w["w_lat"]),
                  pl.BlockSpec((1, Q_LORA), fix), pl.BlockSpec((1, KV_LORA), fix),
                  wspec(w["wq"]), wspec(w["wk"]), wspec(w["wvt"]), tab, tab, tab],
        out_specs=[pl.BlockSpec((tm, nq), row), pl.BlockSpec((tm, nq), row),
                   pl.BlockSpec((None, nv, tm), lambda bi, i: (bi, 0, i))],
        out_shape=[jax.ShapeDtypeStruct((b * seq, nq), BF16),
                   jax.ShapeDtypeStruct((b * seq, nq), BF16),
                   jax.ShapeDtypeStruct((b, nv, seq), BF16)],
        compiler_params=_params(("parallel", "arbitrary")),
        name="attn_prep",
    )(h, w["w_lat"], w["q_norm"].reshape(1, -1), w["kv_norm"].reshape(1, -1),
      w["wq"], w["wk"], w["wvt"], *tabs)


def _mla_kernel(q_ref, k_ref, vt_ref, o_ref, s_ref, p_ref, qt_ref, *, tq, tk, exp_dtype):
    n_q = q_ref.shape[0] // tq
    n_k = k_ref.shape[0] // tk

    def q_tile(qi, carry):
        q0 = pl.multiple_of(qi * tq, tq)
        qt_ref[...] = q_ref[pl.ds(q0, tq), :].T

        def scores(c, slot):
            k = k_ref[pl.ds(pl.multiple_of(c * tk, tk), tk), :]
            s = jnp.dot(k, qt_ref[...], preferred_element_type=F32)
            s_ref[slot] = s
            return jnp.max(s, axis=0, keepdims=True)

        def softmax(slot, cmax, m):
            m_new = jnp.maximum(m, cmax)
            p_ref[slot] = jnp.exp2((s_ref[slot] - m_new).astype(exp_dtype)).astype(BF16)
            return m_new, jnp.exp2(m - m_new)

        def pv(c, slot):
            vt = vt_ref[:, pl.ds(pl.multiple_of(c * tk, tk), tk)]
            return jnp.dot(vt, p_ref[slot], preferred_element_type=F32)

        def stage(c, u, m, acc, cmax):
            part = pv(c - 1, (u - 1) % N_SLOTS)
            cmax_next = scores(c + 1, (u + 1) % N_SLOTS)
            m, alpha = softmax(u % N_SLOTS, cmax, m)
            return m, (acc + part) * alpha, cmax_next

        cmax = scores(0, 0)
        m, _ = softmax(0, cmax, jnp.full((1, tq), -1e30, F32))
        cmax = scores(1, 1)
        acc = jnp.zeros((V_ROWS, tq), F32)

        n_mid = n_k - 2
        unroll = 4 * N_SLOTS
        peel = n_mid % unroll
        st = (m, acc, cmax)
        for u in range(1, peel + 1):
            st = stage(u, u, *st)

        def group(j, st):
            for u in range(1, unroll + 1):
                st = stage(peel + unroll * j + u, peel + u, *st)
            return st

        trips = n_mid // unroll
        if trips == 1:
            st = group(0, st)
        elif trips > 1:
            st = lax.fori_loop(0, trips, group, st)
        m, acc, cmax = st
        part = pv(n_k - 2, (n_k - 2) % N_SLOTS)
        m, alpha = softmax((n_k - 1) % N_SLOTS, cmax, m)
        acc = (acc + part) * alpha + pv(n_k - 1, (n_k - 1) % N_SLOTS)
        out = acc[:V_HEAD] * (1.0 / acc[V_HEAD:V_HEAD + 1])
        o_ref[pl.ds(q0, tq), :] = out.T.astype(o_ref.dtype)
        return carry

    lax.fori_loop(0, n_q, q_tile, 0)


def _mla(q, k, vt, tq_outer, tq, tk):
    b, s, _ = q.shape
    tq_outer = min(tq_outer, s)
    tq = min(tq, tq_outer)
    tk = min(tk, s // 2)
    assert (s // tk) % 2 == 0
    return pl.pallas_call(
        functools.partial(_mla_kernel, tq=tq, tk=tk,
                          exp_dtype=F32 if tq == tq_outer == s else BF16),
        grid=(b, MLA_HEADS, s // tq_outer),
        in_specs=[pl.BlockSpec((None, tq_outer, QK_PAD), lambda bi, h, qi: (bi, qi, h)),
                  pl.BlockSpec((None, s, QK_PAD), lambda bi, h, qi: (bi, 0, h)),
                  pl.BlockSpec((None, V_ROWS, s), lambda bi, h, qi: (bi, h, 0))],
        out_specs=pl.BlockSpec((None, tq_outer, V_HEAD), lambda bi, h, qi: (bi, qi, h)),
        out_shape=jax.ShapeDtypeStruct((b, s, MLA_HEADS * V_HEAD), BF16),
        scratch_shapes=[pltpu.VMEM((N_SLOTS, tk, tq), F32),
                        pltpu.VMEM((N_SLOTS, tk, tq), BF16),
                        pltpu.VMEM((QK_PAD, tq), BF16)],
        compiler_params=_params(("parallel", "parallel", "arbitrary")),
        name="mla_attention",
    )(q, k, vt)


def _mem_kernel(q_ref, kv_ref, o_ref, *, scale):
    for h in range(MEM_HEADS):
        lo = h * MEM_HEAD_DIM
        q = q_ref[:, lo:lo + MEM_HEAD_DIM]
        k = kv_ref[:, lo:lo + MEM_HEAD_DIM]
        v = kv_ref[:, MEM_DIM + lo:MEM_DIM + lo + MEM_HEAD_DIM]
        s = lax.dot_general(q, k, (((1,), (1,)), ((), ())),
                            preferred_element_type=F32) * scale
        p = jnp.exp2(s - jnp.max(s, axis=-1, keepdims=True))
        l = jnp.sum(p, axis=-1, keepdims=True)
        o = jnp.dot(p.astype(BF16), v, preferred_element_type=F32)
        o_ref[:, lo:lo + MEM_HEAD_DIM] = (o * (1.0 / l)).astype(o_ref.dtype)


def _mem_attn(z1, kvm, q_blk, tq):
    b, s, _ = z1.shape
    n_mem = kvm.shape[1]
    tq = min(tq, s)
    scale = MEM_HEAD_DIM ** -0.5 * LOG2E
    return pl.pallas_call(
        functools.partial(_mem_kernel, scale=scale),
        grid=(b, s // tq),
        in_specs=[pl.BlockSpec((None, tq, MEM_DIM), lambda bi, i: (bi, i, q_blk)),
                  pl.BlockSpec((None, n_mem, 2 * MEM_DIM), lambda bi, i: (bi, 0, 0))],
        out_specs=pl.BlockSpec((None, tq, MEM_DIM), lambda bi, i: (bi, i, 0)),
        out_shape=jax.ShapeDtypeStruct((b, s, MEM_DIM), BF16),
        compiler_params=_params(("parallel", "arbitrary")),
        name="mem_attention",
    )(z1, kvm)


def _fft_a_kernel(u_ref, f_ref, tc_ref, ts_ref, o_ref, *, n1, grp):
    f = f_ref[...]
    c = u_ref.shape[-1]
    for j in range(0, u_ref.shape[0], grp):
        x = u_ref[j:j + grp].reshape(grp * n1, c)
        r = jnp.dot(f, x, preferred_element_type=F32).reshape(grp, 2, n1, c)
        br, bi = r[:, 0], r[:, 1]
        tc, ts = tc_ref[j:j + grp], ts_ref[j:j + grp]
        o_ref[j:j + grp, 0] = (br * tc + bi * ts).astype(o_ref.dtype)
        o_ref[j:j + grp, 1] = (bi * tc - br * ts).astype(o_ref.dtype)


def _fft_b_kernel(b_ref, m_ref, cs_ref, o_ref):
    m2 = m_ref[...]
    cs = cs_ref[...]
    n2 = o_ref.shape[1]
    for j in range(b_ref.shape[0]):
        x = b_ref[j].reshape(2 * n2, FOURIER_DIM)
        z = jnp.dot(m2, x, preferred_element_type=F32).astype(BF16)
        for g in range(N_GROUPS):
            lo = g * GROUP_DIM
            zz = jnp.concatenate([z[:n2, lo:lo + GROUP_DIM], z[n2:, lo:lo + GROUP_DIM]],
                                 axis=1)
            y = jnp.dot(zz, cs, preferred_element_type=F32)
            o_ref[j, :, lo:lo + GROUP_DIM] = y.astype(o_ref.dtype)


def _dft_tables(seq, grp):
    n2 = FFT_N2
    n1 = seq // n2
    a1 = 2.0 * np.pi * np.outer(np.arange(n1), np.arange(n1)) / n1
    f1 = np.concatenate([np.cos(a1), -np.sin(a1)], axis=0) / math.sqrt(n1)
    f1 = np.kron(np.eye(grp), f1)
    th = 2.0 * np.pi * np.outer(np.arange(n2), np.arange(n1)) / seq
    a2 = 2.0 * np.pi * np.outer(np.arange(n2), np.arange(n2)) / n2
    c2, s2 = np.cos(a2), np.sin(a2)
    m2 = np.block([[c2, s2], [-s2, c2]]) / math.sqrt(n2)
    ac = 2.0 * np.pi * np.outer(np.arange(GROUP_DIM), np.arange(GROUP_DIM)) / GROUP_DIM
    cs = np.concatenate([np.cos(ac), np.sin(ac)], axis=0) / math.sqrt(GROUP_DIM)
    return (jnp.asarray(f1, BF16), jnp.asarray(np.cos(th)[:, :, None], F32),
            jnp.asarray(np.sin(th)[:, :, None], F32), jnp.asarray(m2, BF16),
            jnp.asarray(cs, BF16))


def _fourier(u, t2, tk1):
    b, s, c = u.shape
    n2 = FFT_N2
    n1 = s // n2
    t2 = min(t2, n2)
    tk1 = min(tk1, n1)
    grp = min(t2, max(1, LANES // n1))
    t2 = min(t2 * grp, n2)
    f1, tc, ts, m2, cs = _dft_tables(s, grp)
    ut = u.reshape(b, n1, n2, c).transpose(0, 2, 1, 3)
    bt = pl.pallas_call(
        functools.partial(_fft_a_kernel, n1=n1, grp=grp),
        grid=(b, n2 // t2),
        in_specs=[pl.BlockSpec((None, t2, n1, c), lambda bi, i: (bi, i, 0, 0)),
                  pl.BlockSpec(f1.shape, lambda bi, i: (0, 0)),
                  pl.BlockSpec((t2, n1, 1), lambda bi, i: (i, 0, 0)),
                  pl.BlockSpec((t2, n1, 1), lambda bi, i: (i, 0, 0))],
        out_specs=pl.BlockSpec((None, t2, 2, n1, c), lambda bi, i: (bi, i, 0, 0, 0)),
        out_shape=jax.ShapeDtypeStruct((b, n2, 2, n1, c), BF16),
        compiler_params=_params(("parallel", "arbitrary")),
        name="fft_stage_a",
    )(ut, f1, tc, ts)
    bk = bt.transpose(0, 3, 2, 1, 4)
    y = pl.pallas_call(
        _fft_b_kernel,
        grid=(b, n1 // tk1),
        in_specs=[pl.BlockSpec((None, tk1, 2, n2, c), lambda bi, i: (bi, i, 0, 0, 0)),
                  pl.BlockSpec((2 * n2, 2 * n2), lambda bi, i: (0, 0)),
                  pl.BlockSpec((2 * GROUP_DIM, GROUP_DIM), lambda bi, i: (0, 0))],
        out_specs=pl.BlockSpec((None, tk1, n2, c), lambda bi, i: (bi, i, 0, 0)),
        out_shape=jax.ShapeDtypeStruct((b, n1, n2, c), BF16),
        compiler_params=_params(("parallel", "arbitrary")),
        name="fft_stage_b",
    )(bk, m2, cs)
    return y.transpose(0, 2, 1, 3).reshape(b, s, c)


def _merge_kernel(yf_ref, oa_ref, om_ref, g0_ref, g1_ref, g2_ref,
                  wf_ref, wa_ref, wm_ref, o_ref):
    def gate(g_ref):
        return jax.nn.sigmoid(g_ref[...].astype(F32))
    yf = jnp.dot(yf_ref[...], wf_ref[...], preferred_element_type=F32)
    ya = jnp.dot(oa_ref[...], wa_ref[...], preferred_element_type=F32)
    ym = jnp.dot(om_ref[...], wm_ref[...], preferred_element_type=F32)
    o_ref[...] = (gate(g0_ref) * yf + gate(g1_ref) * ya + gate(g2_ref) * ym
                  ).astype(o_ref.dtype)


def _merge(yf, oa, om, z1, wf, wa, wm, tm, tn):
    t = yf.shape[0]
    tm = min(tm, t)
    nb = D_MODEL // tn
    def gspec(k):
        return pl.BlockSpec((tm, tn), lambda i, j: (i, k * nb + j))
    def aspec(w):
        return pl.BlockSpec((tm, w), lambda i, j: (i, 0))
    def wspec(w):
        return pl.BlockSpec((w, tn), lambda i, j: (0, j))
    return pl.pallas_call(
        _merge_kernel,
        grid=(t // tm, nb),
        in_specs=[aspec(FOURIER_DIM), aspec(MLA_HEADS * V_HEAD), aspec(MEM_DIM),
                  gspec(0), gspec(1), gspec(2),
                  wspec(FOURIER_DIM), wspec(MLA_HEADS * V_HEAD), wspec(MEM_DIM)],
        out_specs=pl.BlockSpec((tm, tn), lambda i, j: (i, j)),
        out_shape=jax.ShapeDtypeStruct((t, D_MODEL), BF16),
        compiler_params=_params(("parallel", "arbitrary")),
        name="branch_merge",
    )(yf, oa, om, z1, z1, z1, wf, wa, wm)


def _proj_norm_kernel(a_ref, w_ref, x_ref, gpost_ref, gpre_ref, x1_ref, *h_ref, ts):
    for r0 in range(0, a_ref.shape[0], ts):
        rows = slice(r0, r0 + ts)
        y = jnp.dot(a_ref[rows, :], w_ref[...], preferred_element_type=F32)
        x1 = x_ref[rows, :] + _rms(y, gpost_ref[...])
        x1_ref[rows, :] = x1
        if h_ref:
            h_ref[0][rows, :] = _rms(x1, gpre_ref[...]).astype(h_ref[0].dtype)


def _proj_norm(a, w, x, gpost, gpre, tm, ts, emit_h, name):
    t, d = x.shape
    k = a.shape[1]
    tm = min(tm, t)
    row = lambda i: (i, 0)
    fix = lambda i: (0, 0)
    out_specs = [pl.BlockSpec((tm, d), row)]
    out_shape = [jax.ShapeDtypeStruct((t, d), F32)]
    if emit_h:
        out_specs.append(pl.BlockSpec((tm, d), row))
        out_shape.append(jax.ShapeDtypeStruct((t, d), BF16))
    outs = pl.pallas_call(
        functools.partial(_proj_norm_kernel, ts=min(ts, tm)),
        grid=(t // tm,),
        in_specs=[pl.BlockSpec((tm, k), row),
                  pl.BlockSpec((k, d), fix, pipeline_mode=pl.Buffered(1)),
                  pl.BlockSpec((tm, d), row), pl.BlockSpec((1, d), fix),
                  pl.BlockSpec((1, d), fix)],
        out_specs=out_specs,
        out_shape=out_shape,
        compiler_params=_params(("parallel",)),
        name=name,
    )(a, w, x, gpost.reshape(1, d), gpre.reshape(1, d))
    return outs if emit_h else (outs[0], None)


def _gelu_tanh(x):
    return 0.5 * x * (1.0 + jnp.tanh(0.7978845608028654 * (x + 0.044715 * x * x * x)))


def _ffn_act_kernel(a_ref, ap_ref, an_ref, wg_ref, wu_ref, cw_ref, cb_ref, o_ref, ax_ref, *,
                    tiles_per_seq, halo):
    tm = a_ref.shape[0]
    tc = o_ref.shape[1]

    pos = pl.program_id(0) % tiles_per_seq

    @pl.when(pl.program_id(1) == 0)
    def _():
        zero = jnp.zeros_like(ap_ref[...])
        ax_ref[0:halo] = jnp.where(pos == 0, zero, ap_ref[...])
        ax_ref[halo:halo + tm] = a_ref[...]
        ax_ref[halo + tm:] = jnp.where(pos == tiles_per_seq - 1, zero, an_ref[...])

    g = jnp.dot(ax_ref[...], wg_ref[...], preferred_element_type=F32)
    u = jnp.dot(ax_ref[halo:halo + tm], wu_ref[...], preferred_element_type=F32)
    g_prev = pltpu.roll(g, 1, 0)[halo:halo + tm]
    g_next = pltpu.roll(g, tm + 2 * halo - 1, 0)[halo:halo + tm]
    cw = cw_ref[...]
    conv = g_prev * cw[0:1] + g[halo:halo + tm] * cw[1:2] + g_next * cw[2:3] + cb_ref[...]
    o_ref[...] = (_gelu_tanh(conv) * u).astype(o_ref.dtype)


def _ffn_act(h, w_gate, w_up, cw, cb, seq, tm, tc):
    t, d = h.shape
    n = cw.shape[1]
    tm = min(tm, seq)
    halo = BF16_ROWS
    tiles_per_seq = seq // tm
    hb = tm // halo
    last_hb = t // halo - 1
    return pl.pallas_call(
        functools.partial(_ffn_act_kernel, tiles_per_seq=tiles_per_seq, halo=halo),
        grid=(t // tm, n // tc),
        in_specs=[pl.BlockSpec((tm, d), lambda i, j: (i, 0)),
                  pl.BlockSpec((halo, d), lambda i, j: (jnp.maximum(i * hb - 1, 0), 0)),
                  pl.BlockSpec((halo, d), lambda i, j: (jnp.minimum((i + 1) * hb, last_hb), 0)),
                  pl.BlockSpec((d, tc), lambda i, j: (0, j)),
                  pl.BlockSpec((d, tc), lambda i, j: (0, j)),
                  pl.BlockSpec((3, tc), lambda i, j: (0, j)),
                  pl.BlockSpec((1, tc), lambda i, j: (0, j))],
        out_specs=pl.BlockSpec((tm, tc), lambda i, j: (i, j)),
        out_shape=jax.ShapeDtypeStruct((t, n), BF16),
        scratch_shapes=[pltpu.VMEM((tm + 2 * halo, d), BF16)],
        compiler_params=_params(("parallel", "arbitrary")),
        name="ffn_gate_up_act",
    )(h, h, h, w_gate, w_up, cw, cb.reshape(1, n))


def _pack_weights(p, l):
    s0 = FOURIER_DIM
    s1 = s0 + Q_LORA
    s2 = s1 + KV_LORA
    s3 = s2 + QK_ROPE
    s4 = s3 + MEM_DIM
    w_in = p["w_in"][l]
    d = w_in.shape[0]
    bf = lambda a: a.astype(BF16)
    w_gq = bf(jnp.concatenate([w_in[:, s4:], w_in[:, s3:s4]], axis=1))
    w_f = bf(w_in[:, :s0])
    w_lat = bf(jnp.concatenate([w_in[:, s0:s3], jnp.zeros((d, KPE_PAD - QK_ROPE), F32)],
                               axis=1))
    wq = p["w_uq"][l].reshape(Q_LORA, MLA_HEADS, QK_NOPE + QK_ROPE)
    wq = jnp.pad(wq, ((0, 0), (0, 0), (0, QK_PAD - QK_NOPE - QK_ROPE)))
    wq = bf(wq.reshape(Q_LORA, MLA_HEADS * QK_PAD))
    wkv = p["w_ukv"][l].reshape(KV_LORA, MLA_HEADS, QK_NOPE + V_HEAD)
    wk = bf(wkv[:, :, :QK_NOPE].reshape(KV_LORA, MLA_HEADS * QK_NOPE))
    wvt = bf(wkv[:, :, QK_NOPE:].reshape(KV_LORA, MLA_HEADS * V_HEAD).T)
    return dict(
        w_gq=w_gq, w_f=w_f, w_lat=w_lat, wq=wq, wk=wk, wvt=wvt,
        w_gate=bf(p["w_ffn_gate"][l]), w_up=bf(p["w_ffn_up"][l]),
        w_mem_kv=bf(p["w_mem_kv"][l]), w_fo=bf(p["w_fourier_out"][l]),
        w_ao=bf(p["w_attn_out"][l]), w_mo=bf(p["w_mem_out"][l]), w_o=bf(p["w_o"][l]),
        w_down=bf(p["w_ffn_down"][l]), conv_w=p["ffn_conv_w"][l], conv_b=p["ffn_conv_b"][l],
        q_norm=p["q_norm"][l], kv_norm=p["kv_norm"][l], mem_norm=p["mem_norm"][l],
        pre_mix=p["pre_mix_norm"][l], post_mix=p["post_mix_norm"][l],
        pre_ffn=p["pre_ffn_norm"][l], post_ffn=p["post_ffn_norm"][l])


def _rope_tables(seq):
    inv_freq = 1.0 / (ROPE_THETA ** (jnp.arange(0, QK_ROPE, 2, dtype=F32) / QK_ROPE))
    ang = jnp.arange(seq, dtype=F32)[:, None] * inv_freq[None, :]
    cos, sin = jnp.cos(ang), jnp.sin(ang)
    half = QK_ROPE // 2
    zh = jnp.zeros((seq, half), F32)
    zr = jnp.zeros((seq, LANES - QK_ROPE), F32)
    c = jnp.concatenate([cos, cos, zr], axis=1)
    s1 = jnp.concatenate([-sin, zh, zr], axis=1)
    s2 = jnp.concatenate([zh, sin, zr], axis=1)
    return c, s1, s2


def _trunk(x, mem, layers):
    b, s, d = x.shape
    t = b * s
    n_mem = mem.shape[1]
    tabs = _rope_tables(s)
    q_scale = (QK_NOPE + QK_ROPE) ** -0.5 * LOG2E
    gate_w = 3 * D_MODEL
    xf = x.reshape(t, d)
    memf = mem.reshape(b * n_mem, d)
    h = None
    for l, w in enumerate(layers):
        if l == 0:
            h, z1 = _norm_mm(xf, w["pre_mix"], w["w_gq"], MM_ROWS, MM_COLS, "in_proj_gates")
        else:
            z1 = _mm(h, w["w_gq"], MM_ROWS, MM_COLS, "in_proj_gates")
        f_in = _mm(h, w["w_f"], MM_ROWS, MM_COLS, "in_proj_fourier")
        q, kc, vt = _attn_prep(h, w, tabs, b, s, PREP_ROWS, q_scale)
        q_tile = ATT_Q_BLOCK if s <= ATT_Q_BLOCK else ATT_Q_TILE_LONG
        oa = _mla(q.reshape(b, s, -1), kc.reshape(b, s, -1), vt, q_tile, q_tile,
                  ATT_K_CHUNK)
        mem_n = _norm(memf, w["mem_norm"], n_mem)
        kvm = _mm(mem_n, w["w_mem_kv"], MM_ROWS, MM_COLS, "mem_kv")
        om = _mem_attn(z1.reshape(b, s, -1), kvm.reshape(b, n_mem, -1),
                       gate_w // MEM_DIM, MEM_Q_TILE)
        yf = _fourier(f_in.reshape(b, s, -1), FFT_STEP, FFT_STEP)
        merged = _merge(yf.reshape(t, -1), oa.reshape(t, -1), om.reshape(t, -1), z1,
                        w["w_fo"], w["w_ao"], w["w_mo"], MERGE_ROWS, MERGE_COLS)
        x1, h2 = _proj_norm(merged, w["w_o"], xf, w["post_mix"], w["pre_ffn"],
                            OUT_PROJ_ROWS, NORM_SUB_ROWS, True, "out_proj_norm")
        act = _ffn_act(h2, w["w_gate"], w["w_up"], w["conv_w"], w["conv_b"], s, FFN_ROWS,
                       FF_CHUNK)
        last = l == len(layers) - 1
        nxt = layers[0 if last else l + 1]["pre_mix"]
        xf, h = _proj_norm(act, w["w_down"], x1, w["post_ffn"], nxt, DOWN_ROWS,
                           NORM_SUB_ROWS, not last, "ffn_down_norm")
    return xf.reshape(b, s, d)


def kernel(x_prompt, x_sample, mem_prompt, mem_sample, pre_mix_norm, w_in, q_norm, w_uq,
           kv_norm, w_ukv, mem_norm, w_mem_kv, w_fourier_out, w_attn_out, w_mem_out, w_o,
           post_mix_norm, pre_ffn_norm, w_ffn_gate, w_ffn_up, ffn_conv_w, ffn_conv_b,
           w_ffn_down, post_ffn_norm):
    p = dict(pre_mix_norm=pre_mix_norm, w_in=w_in, q_norm=q_norm, w_uq=w_uq,
             kv_norm=kv_norm, w_ukv=w_ukv, mem_norm=mem_norm, w_mem_kv=w_mem_kv,
             w_fourier_out=w_fourier_out, w_attn_out=w_attn_out, w_mem_out=w_mem_out,
             w_o=w_o, post_mix_norm=post_mix_norm, pre_ffn_norm=pre_ffn_norm,
             w_ffn_gate=w_ffn_gate, w_ffn_up=w_ffn_up, ffn_conv_w=ffn_conv_w,
             ffn_conv_b=ffn_conv_b, w_ffn_down=w_ffn_down, post_ffn_norm=post_ffn_norm)
    layers = [_pack_weights(p, l) for l in range(w_in.shape[0])]
    return (_trunk(x_prompt, mem_prompt, layers), _trunk(x_sample, mem_sample, layers))
```
